```python
import jax
import jax.numpy as jnp
from jax import lax
import numpy as np

D_MODEL = 2048
BATCH = 4
SEQ = 2048
DEPTH = 4

GRID_W = 64
CTX_LEN = 256
NORM_EPS = 1e-6
N_MOD = 6

GLA_HEADS = 4
GLA_DK = D_MODEL // 2
GLA_DV = D_MODEL
GLA_HDK = GLA_DK // GLA_HEADS
GLA_HDV = GLA_DV // GLA_HEADS
GLA_RANK = 16
GLA_GATE_NORM = 16.0
GLA_CHUNK = 64

ATT_HEADS = 16
ATT_KV_HEADS = 4
ATT_HD = 128
ATT_GROUP = ATT_HEADS // ATT_KV_HEADS
ATT_Q = ATT_HEADS * ATT_HD
ATT_KV = ATT_KV_HEADS * ATT_HD
Q_BLOCK = 128
ROPE_THETA = 10000.0
ROPE_AXIS_PAIRS = ATT_HD // 4

LRU_W = D_MODEL
LRU_BLOCKS = 16
LRU_BS = LRU_W // LRU_BLOCKS
LRU_C = 8.0
CONV_W = 4
CONV_LEFT = 2

N_BRANCH = 3
FFN_HIDDEN = -(-8 * D_MODEL // (3 * 256)) * 256

SPLITS = (GLA_DK, GLA_DK, GLA_DV, GLA_DV, 2 * GLA_RANK, ATT_Q, ATT_KV, ATT_KV, LRU_W, LRU_W, N_BRANCH * D_MODEL)
N_IN = sum(SPLITS)

kernel_name = 'hybrid_gla_gqa_rglru_prefix_dit'


def rmsnorm(x, g):
    xf = x.astype(jnp.float32)
    y = xf * lax.rsqrt(jnp.mean(xf * xf, axis=-1, keepdims=True) + NORM_EPS)
    return (y * g.astype(jnp.float32)).astype(x.dtype)


def modulate(x, shift, scale):
    return x * (1 + scale) + shift


def split_cols(p):
    return jnp.split(p, np.cumsum(SPLITS)[:-1].tolist(), axis=-1)


def flip(z):
    return jnp.flip(z, axis=1)


def axial_rope(n_tokens):
    rows = n_tokens // GRID_W
    row = jnp.repeat(jnp.arange(rows, dtype=jnp.float32), GRID_W)
    col = jnp.tile(jnp.arange(GRID_W, dtype=jnp.float32), rows)
    inv = ROPE_THETA ** (-jnp.arange(ROPE_AXIS_PAIRS, dtype=jnp.float32) / ROPE_AXIS_PAIRS)
    ang = jnp.concatenate([row[:, None] * inv, col[:, None] * inv], axis=-1)
    return jnp.cos(ang), jnp.sin(ang)


def apply_rope(x, cos, sin):
    c_ = cos[None, :, None, :].astype(x.dtype)
    s_ = sin[None, :, None, :].astype(x.dtype)
    x1, x2 = x[..., 0::2], x[..., 1::2]
    return jnp.stack([x1 * c_ - x2 * s_, x1 * s_ + x2 * c_], axis=-1).reshape(x.shape)


def gla_scan(q, k, v, log_a, s0):
    B, T, H, _ = q.shape
    n = T // GLA_CHUNK

    def chunks(z):
        return z.astype(jnp.float32).reshape(B, n, GLA_CHUNK, H, z.shape[-1]).transpose(1, 0, 3, 2, 4)

    qc, kc, vc, gc = chunks(q), chunks(k), chunks(v), chunks(log_a)
    b = jnp.cumsum(gc, axis=3)
    b_last = b[:, :, :, -1:, :]
    q_dec = qc * jnp.exp(b)
    k_dec = kc * jnp.exp(b_last - b)
    lower = jnp.tril(jnp.ones((GLA_CHUNK, GLA_CHUNK), dtype=bool))
    att = jnp.einsum('nbhid,nbhjd->nbhij', q_dec, kc * jnp.exp(-b))
    att = jnp.where(lower, att, 0.0)
    o_intra = jnp.einsum('nbhij,nbhjv->nbhiv', att, vc)

    def step(s, xs):
        qd, kd, vv, dl = xs
        o = jnp.einsum('bhid,bhdv->bhiv', qd, s)
        s = s * dl[:, :, 0, :, None] + jnp.einsum('bhjd,bhjv->bhdv', kd, vv)
        return s, o

    s_fin, o_inter = lax.scan(step, s0, (q_dec, k_dec, vc, jnp.exp(b_last)))
    o = (o_intra + o_inter).transpose(1, 0, 3, 2, 4).reshape(B, T, H, v.shape[-1])
    return o.astype(v.dtype), s_fin


def gla_prep(q, k, v, dec, w_decay, b_decay):
    B, T, _ = q.shape
    la = jax.nn.log_sigmoid(
        jnp.einsum('btnr,nrk->btnk', dec.reshape(B, T, 2, GLA_RANK).astype(jnp.float32), w_decay.astype(jnp.float32))
        + b_decay.astype(jnp.float32)) / GLA_GATE_NORM
    la = la.reshape(B, T, 2, GLA_HEADS, GLA_HDK)
    qh = q.reshape(B, T, GLA_HEADS, GLA_HDK) * (GLA_HDK ** -0.5)
    kh = k.reshape(B, T, GLA_HEADS, GLA_HDK)
    vh = v.reshape(B, T, GLA_HEADS, GLA_HDV)
    return qh, kh, vh, la[:, :, 0], la[:, :, 1]


def gla_out(o, r, norm_g):
    B, T = o.shape[:2]
    return rmsnorm(o, norm_g).reshape(B, T, GLA_DV).astype(r.dtype) * jax.nn.silu(r)


def gla_branch(q, k, v, r, dec, cq, ck, cv, cr, cdec, w_decay, b_decay, norm_g, need_ctx):
    qh, kh, vh, lf, lb = gla_prep(q, k, v, dec, w_decay, b_decay)
    cqh, ckh, cvh, clf, clb = gla_prep(cq, ck, cv, cdec, w_decay, b_decay)
    B = q.shape[0]
    s0 = jnp.zeros((B, GLA_HEADS, GLA_HDK, GLA_HDV), jnp.float32)
    oc_f, s_f = gla_scan(cqh, ckh, cvh, clf, s0)
    oc_b, s_b = gla_scan(flip(cqh), flip(ckh), flip(cvh), flip(clb), s0)
    o_f, _ = gla_scan(qh, kh, vh, lf, s_f)
    o_b, _ = gla_scan(flip(qh), flip(kh), flip(vh), flip(lb), s_b)
    out = gla_out(o_f + flip(o_b), r, norm_g)
    out_c = gla_out(oc_f + flip(oc_b), cr, norm_g) if need_ctx else None
    return out, out_c


def gqa_attend(q, k, v):
    s = jnp.einsum('bqkgd,bskd->bkgqs', q.astype(jnp.float32), k.astype(jnp.float32)) * (ATT_HD ** -0.5)
    p = jax.nn.softmax(s, axis=-1)
    return jnp.einsum('bkgqs,bskd->bqkgd', p.astype(v.dtype), v)


def gqa_branch(q, k, v, cq, ck, cv, q_g, k_g, cos, sin, need_ctx):
    B, T, _ = q.shape
    Lc = cq.shape[1]

    def heads(z, n):
        return z.reshape(z.shape[0], z.shape[1], n, ATT_HD)

    qh = apply_rope(rmsnorm(heads(q, ATT_HEADS), q_g), cos, sin)
    kh = apply_rope(rmsnorm(heads(k, ATT_KV_HEADS), k_g), cos, sin)
    vh = heads(v, ATT_KV_HEADS)
    cqh = rmsnorm(heads(cq, ATT_HEADS), q_g)
    ckh = rmsnorm(heads(ck, ATT_KV_HEADS), k_g)
    cvh = heads(cv, ATT_KV_HEADS)
    k_all = jnp.concatenate([kh, ckh], axis=1)
    v_all = jnp.concatenate([vh, cvh], axis=1)
    nb = T // Q_BLOCK
    qb = qh.reshape(B, nb, Q_BLOCK, ATT_KV_HEADS, ATT_GROUP, ATT_HD).transpose(1, 0, 2, 3, 4, 5)
    o = lax.map(lambda blk: gqa_attend(blk, k_all, v_all), qb)
    out = o.transpose(1, 0, 2, 3, 4, 5).reshape(B, T, ATT_Q)
    out_c = None
    if need_ctx:
        out_c = gqa_attend(cqh.reshape(B, Lc, ATT_KV_HEADS, ATT_GROUP, ATT_HD), ckh, cvh).reshape(B, Lc, ATT_Q)
    return out, out_c


def dwconv_centred(x, w, b):
    T = x.shape[1]
    xp = jnp.pad(x, ((0, 0), (CONV_LEFT, CONV_W - 1 - CONV_LEFT), (0, 0)))
    out = b
    for j in range(CONV_W):
        out = out + xp[:, j:j + T] * w[j]
    return out


def _lin_combine(left, right):
    a_l, b_l = left
    a_r, b_r = right
    return a_l * a_r, a_r * b_l + b_r


def rglru_scan(x, w_a, b_a, w_i, b_i, lam, h0, reset_first):
    B, T, W = x.shape
    xf = x.astype(jnp.float32)
    xb = xf.reshape(B, T, LRU_BLOCKS, LRU_BS)
    r = jax.nn.sigmoid(jnp.einsum('btnk,nkj->btnj', xb, w_a.astype(jnp.float32)).reshape(B, T, W) + b_a)
    i = jax.nn.sigmoid(jnp.einsum('btnk,nkj->btnj', xb, w_i.astype(jnp.float32)).reshape(B, T, W) + b_i)
    log_a = -LRU_C * r * jax.nn.softplus(-lam.astype(jnp.float32))
    a = jnp.exp(log_a)
    mult = jnp.sqrt(-jnp.expm1(2.0 * log_a))
    if reset_first:
        mult = mult.at[:, 0].set(1.0)
    a_cum, h = lax.associative_scan(_lin_combine, (a, mult * i * xf), axis=1)
    h = h + a_cum * h0[:, None]
    return h.astype(x.dtype), h[:, -1]


def lru_branch(x_lat, y_lat, x_ctx, y_ctx, conv_w, conv_b, w_a, b_a, w_i, b_i, lam, need_ctx):
    xl = dwconv_centred(x_lat, conv_w, conv_b)
    xc = dwconv_centred(x_ctx, conv_w, conv_b)
    h0 = jnp.zeros((x_lat.shape[0], LRU_W), jnp.float32)
    hc_f, s_f = rglru_scan(xc, w_a[0], b_a[0], w_i[0], b_i[0], lam[0], h0, True)
    hc_b, s_b = rglru_scan(flip(xc), w_a[1], b_a[1], w_i[1], b_i[1], lam[1], h0, True)
    hl_f, _ = rglru_scan(xl, w_a[0], b_a[0], w_i[0], b_i[0], lam[0], s_f, False)
    hl_b, _ = rglru_scan(flip(xl), w_a[1], b_a[1], w_i[1], b_i[1], lam[1], s_b, False)
    out = (hl_f + flip(hl_b)) * jax.nn.gelu(y_lat)
    out_c = (hc_f + flip(hc_b)) * jax.nn.gelu(y_ctx) if need_ctx else None
    return out, out_c


def merge_project(o_a, o_b, o_c, gate_logits, b_merge, w_branch, w_out):
    B, T, _ = o_a.shape
    gates = jax.nn.sigmoid(gate_logits.reshape(B, T, N_BRANCH, D_MODEL).astype(jnp.float32) + b_merge)
    proj = jnp.einsum('btnw,nwd->btnd', jnp.stack([o_a, o_b, o_c], axis=2), w_branch)
    merged = jnp.einsum('btnd,btnd->btd', gates.astype(proj.dtype), proj)
    return merged @ w_out


def hybrid_mixer(u, uc, w_in, gla_w_decay, gla_b_decay, gla_norm_g, q_norm_g, k_norm_g,
                 conv_w, conv_b, lru_w_a, lru_b_a, lru_w_i, lru_b_i, lru_lambda,
                 b_merge, w_branch, w_out, cos, sin, need_ctx):
    g_q, g_k, g_v, g_r, g_dec, a_q, a_k, a_v, l_x, l_y, gates = split_cols(u @ w_in)
    cg_q, cg_k, cg_v, cg_r, cg_dec, ca_q, ca_k, ca_v, cl_x, cl_y, cgates = split_cols(uc @ w_in)
    o_gla, oc_gla = gla_branch(g_q, g_k, g_v, g_r, g_dec, cg_q, cg_k, cg_v, cg_r, cg_dec,
                               gla_w_decay, gla_b_decay, gla_norm_g, need_ctx)
    o_att, oc_att = gqa_branch(a_q, a_k, a_v, ca_q, ca_k, ca_v, q_norm_g, k_norm_g, cos, sin, need_ctx)
    o_lru, oc_lru = lru_branch(l_x, l_y, cl_x, cl_y, conv_w, conv_b, lru_w_a, lru_b_a,
                               lru_w_i, lru_b_i, lru_lambda, need_ctx)
    y = merge_project(o_gla, o_att, o_lru, gates, b_merge, w_branch, w_out)
    yc = merge_project(oc_gla, oc_att, oc_lru, cgates, b_merge, w_branch, w_out) if need_ctx else None
    return y, yc


def swiglu(u, w_in, w_out):
    g, up = jnp.split(u @ w_in, 2, axis=-1)
    return (jax.nn.silu(g) * up) @ w_out


def setup_inputs(seed: int = 0) -> dict:
    key = jax.random.key(seed)
    ks = iter(jax.random.split(key, 32))
    f32 = jnp.float32
    D = D_MODEL

    def nrm(shape, scale):
        return jax.random.normal(next(ks), shape, f32) * scale

    u = jax.random.uniform(next(ks), (DEPTH, 2, LRU_W), f32, 0.9, 0.999)
    p = u ** (1.0 / LRU_C)
    return {
        'x': nrm((BATCH, SEQ, D), 1.0),
        'c': nrm((BATCH, D), 1.0),
        'ctx': nrm((BATCH, CTX_LEN, D), 1.0),
        'c_ctx': nrm((D,), 1.0),
        'w_mod': nrm((DEPTH, D, N_MOD * D), 0.5 * D ** -0.5),
        'b_mod': nrm((DEPTH, N_MOD * D), 0.02),
        'norm_mix_g': 1.0 + nrm((DEPTH, D), 0.02),
        'norm_ffn_g': 1.0 + nrm((DEPTH, D), 0.02),
        'w_in': nrm((DEPTH, D, N_IN), D ** -0.5),
        'gla_w_decay': nrm((DEPTH, 2, GLA_RANK, GLA_DK), GLA_RANK ** -0.5),
        'gla_b_decay': nrm((DEPTH, 2, GLA_DK), 0.5),
        'gla_norm_g': 1.0 + nrm((DEPTH, GLA_HDV), 0.02),
        'q_norm_g': 1.0 + nrm((DEPTH, ATT_HD), 0.02),
        'k_norm_g': 1.0 + nrm((DEPTH, ATT_HD), 0.02),
        'conv_w': nrm((DEPTH, CONV_W, LRU_W), CONV_W ** -0.5),
        'conv_b': nrm((DEPTH, LRU_W), 0.02),
        'lru_w_a': nrm((DEPTH, 2, LRU_BLOCKS, LRU_BS, LRU_BS), LRU_BS ** -0.5),
        'lru_b_a': nrm((DEPTH, 2, LRU_W), 0.02),
        'lru_w_i': nrm((DEPTH, 2, LRU_BLOCKS, LRU_BS, LRU_BS), LRU_BS ** -0.5),
        'lru_b_i': nrm((DEPTH, 2, LRU_W), 0.02),
        'lru_lambda': jnp.log(p) - jnp.log1p(-p),
        'b_merge': nrm((DEPTH, N_BRANCH, D), 0.02),
        'w_branch': nrm((DEPTH, N_BRANCH, GLA_DV, D), GLA_DV ** -0.5),
        'w_out': nrm((DEPTH, D, D), D ** -0.5),
        'w_ffn_in': nrm((DEPTH, D, 2 * FFN_HIDDEN), D ** -0.5),
        'w_ffn_out': nrm((DEPTH, FFN_HIDDEN, D), FFN_HIDDEN ** -0.5),
        'final_norm_g': 1.0 + nrm((D,), 0.02),
    }


def reference(x, c, ctx, c_ctx, w_mod, b_mod, norm_mix_g, norm_ffn_g, w_in, gla_w_decay, gla_b_decay,
              gla_norm_g, q_norm_g, k_norm_g, conv_w, conv_b, lru_w_a, lru_b_a, lru_w_i, lru_b_i,
              lru_lambda, b_merge, w_branch, w_out, w_ffn_in, w_ffn_out, final_norm_g):
    T = x.shape[1]
    cos, sin = axial_rope(T)
    sc = jax.nn.silu(c)
    scc = jax.nn.silu(c_ctx)
    h, hc = x, ctx
    for l in range(DEPTH):
        need_ctx = l < DEPTH - 1
        m = sc @ w_mod[l] + b_mod[l]
        mc = scc @ w_mod[l] + b_mod[l]
        sh1, s1, g1, sh2, s2, g2 = [z[:, None] for z in jnp.split(m, N_MOD, axis=-1)]
        csh1, cs1, cg1, csh2, cs2, cg2 = jnp.split(mc, N_MOD, axis=-1)
        u = modulate(rmsnorm(h, norm_mix_g[l]), sh1, s1)
        uc = modulate(rmsnorm(hc, norm_mix_g[l]), csh1, cs1)
        y, yc = hybrid_mixer(u, uc, w_in[l], gla_w_decay[l], gla_b_decay[l], gla_norm_g[l],
                             q_norm_g[l], k_norm_g[l], conv_w[l], conv_b[l], lru_w_a[l], lru_b_a[l],
                             lru_w_i[l], lru_b_i[l], lru_lambda[l], b_merge[l], w_branch[l], w_out[l],
                             cos, sin, need_ctx)
        h = h + g1 * y
        h = h + g2 * swiglu(modulate(rmsnorm(h, norm_ffn_g[l]), sh2, s2), w_ffn_in[l], w_ffn_out[l])
        if need_ctx:
            hc = hc + cg1 * yc
            hc = hc + cg2 * swiglu(modulate(rmsnorm(hc, norm_ffn_g[l]), csh2, cs2), w_ffn_in[l], w_ffn_out[l])
    return rmsnorm(h, final_norm_g)
```

```python
import functools

import jax
import jax.numpy as jnp
import numpy as np
from jax import lax
from jax.experimental import pallas as pl
from jax.experimental.pallas import tpu as pltpu

F32 = jnp.float32
BF16 = jnp.bfloat16

D_MODEL = 2048
GRID_W = 64
NORM_EPS = 1e-6
N_MOD = 6

GLA_HEADS = 4
GLA_DK = D_MODEL // 2
GLA_DV = D_MODEL
GLA_HDK = GLA_DK // GLA_HEADS
GLA_HDV = GLA_DV // GLA_HEADS
GLA_RANK = 16
GLA_GATE_NORM = 16.0
GLA_CHUNK = 64

ATT_HEADS = 16
ATT_KV_HEADS = 4
ATT_HD = 128
ATT_GROUP = ATT_HEADS // ATT_KV_HEADS
ATT_Q = ATT_HEADS * ATT_HD
ATT_KV = ATT_KV_HEADS * ATT_HD
ROPE_THETA = 10000.0
ROPE_AXIS_PAIRS = ATT_HD // 4

LRU_W = D_MODEL
LRU_BLOCKS = 16
LRU_BS = LRU_W // LRU_BLOCKS
LRU_C = 8.0
CONV_W = 4
CONV_LEFT = 2

N_BRANCH = 3
FFN_HIDDEN = -(-8 * D_MODEL // (3 * 256)) * 256

COL_GQ = 0
COL_GK = COL_GQ + GLA_DK
COL_GV = COL_GK + GLA_DK
COL_GR = COL_GV + GLA_DV
COL_AQ = COL_GR + GLA_DV
COL_AK = COL_AQ + ATT_Q
COL_AV = COL_AK + ATT_KV
COL_LX = COL_AV + ATT_KV
COL_LY = COL_LX + LRU_W
COL_GT = COL_LY + LRU_W
N_MAIN = COL_GT + N_BRANCH * D_MODEL
DEC_ORIG = 2 * GLA_DK + 2 * GLA_DV
DEC_PAD = 128

V7X_VMEM_LIMIT_BYTES = 56 * 1024 * 1024
MOD_ROWS = 16


def _cparams(n_axes, vmem=V7X_VMEM_LIMIT_BYTES):
    return pltpu.CompilerParams(dimension_semantics=("arbitrary",) * n_axes, vmem_limit_bytes=vmem)


def _sigmoid(x):
    return 1.0 / (1.0 + jnp.exp(-x))


def _mods_kernel(c_ref, w_ref, b_ref, o_ref):
    c = c_ref[...]
    sc = (c * _sigmoid(c)).astype(BF16)
    o_ref[...] = jnp.dot(sc, w_ref[...].astype(BF16), preferred_element_type=F32) + b_ref[...]


def _mods_call(c16, w_mod, b_mod):
    depth, d, n = w_mod.shape
    tn = 1024
    return pl.pallas_call(
        _mods_kernel,
        grid=(depth, n // tn),
        in_specs=[
            pl.BlockSpec((MOD_ROWS, d), lambda l, j: (0, 0)),
            pl.BlockSpec((None, d, tn), lambda l, j: (l, 0, j)),
            pl.BlockSpec((None, 1, tn), lambda l, j: (l, 0, j)),
        ],
        out_specs=pl.BlockSpec((None, MOD_ROWS, tn), lambda l, j: (l, 0, j)),
        out_shape=jax.ShapeDtypeStruct((depth, MOD_ROWS, n), F32),
        compiler_params=_cparams(2),
        name="mods",
    )(c16, w_mod, b_mod.reshape(depth, 1, n))


def _normmod_kernel(h_ref, g_ref, sh_ref, sc_ref, o_ref):
    x = h_ref[...]
    var = jnp.mean(x * x, axis=-1, keepdims=True)
    y = x * lax.rsqrt(var + NORM_EPS) * g_ref[...]
    o_ref[...] = (y * (1.0 + sc_ref[...]) + sh_ref[...]).astype(o_ref.dtype)


def _normmod_call(h, g, mods4, layer, k_shift, k_scale, n_ctx, ctx_row):
    b, r, d = h.shape
    tr = n_ctx

    def mod_map(k):
        return lambda bi, i: (layer, jnp.where(i == 0, ctx_row, bi), 0, k)

    return pl.pallas_call(
        _normmod_kernel,
        grid=(b, r // tr),
        in_specs=[
            pl.BlockSpec((None, tr, d), lambda bi, i: (bi, i, 0)),
            pl.BlockSpec((None, 1, d), lambda bi, i: (layer, 0, 0)),
            pl.BlockSpec((None, None, 1, d), mod_map(k_shift)),
            pl.BlockSpec((None, None, 1, d), mod_map(k_scale)),
        ],
        out_specs=pl.BlockSpec((None, tr, d), lambda bi, i: (bi, i, 0)),
        out_shape=jax.ShapeDtypeStruct((b, r, d), BF16),
        compiler_params=_cparams(2),
        name="normmod",
    )(h, g, mods4, mods4)


def _linear_kernel(x_ref, w_ref, o_ref):
    o_ref[...] = jnp.dot(x_ref[...], w_ref[...], preferred_element_type=F32).astype(o_ref.dtype)


def _linear_call(x, w3, layer, out_dtype, tm, tn, name):
    b, r, k = x.shape
    n = w3.shape[-1]
    return pl.pallas_call(
        _linear_kernel,
        grid=(n // tn, b, r // tm),
        in_specs=[
            pl.BlockSpec((None, tm, k), lambda j, bi, i: (bi, i, 0)),
            pl.BlockSpec((None, k, tn), lambda j, bi, i: (layer, 0, j)),
        ],
        out_specs=pl.BlockSpec((None, tm, tn), lambda j, bi, i: (bi, i, j)),
        out_shape=jax.ShapeDtypeStruct((b, r, n), out_dtype),
        compiler_params=_cparams(3),
        name=name,
    )(x, w3)


def _linear_res_kernel(x_ref, w_ref, h_ref, gc_ref, gl_ref, o_ref, *, tm, n_ctx):
    i = pl.program_id(2)
    y = jnp.dot(x_ref[...], w_ref[...], preferred_element_type=F32)
    rows = i * tm + lax.broadcasted_iota(jnp.int32, (tm, 1), 0)
    gate = jnp.where(rows < n_ctx, gc_ref[...], gl_ref[...])
    o_ref[...] = h_ref[...] + gate * y


def _linear_res_call(x, w3, layer, h, mods4, k_gate, n_ctx, ctx_row, tm, tn, name):
    b, r, k = x.shape
    n = w3.shape[-1]
    ngate = n // tn
    return pl.pallas_call(
        functools.partial(_linear_res_kernel, tm=tm, n_ctx=n_ctx),
        grid=(n // tn, b, r // tm),
        in_specs=[
            pl.BlockSpec((None, tm, k), lambda j, bi, i: (bi, i, 0)),
            pl.BlockSpec((None, k, tn), lambda j, bi, i: (layer, 0, j)),
            pl.BlockSpec((None, tm, tn), lambda j, bi, i: (bi, i, j)),
            pl.BlockSpec((None, None, 1, tn), lambda j, bi, i: (layer, ctx_row, 0, k_gate * ngate + j)),
            pl.BlockSpec((None, None, 1, tn), lambda j, bi, i: (layer, bi, 0, k_gate * ngate + j)),
        ],
        out_specs=pl.BlockSpec((None, tm, tn), lambda j, bi, i: (bi, i, j)),
        out_shape=jax.ShapeDtypeStruct((b, r, n), F32),
        compiler_params=_cparams(3),
        name=name,
    )(x, w3, h, mods4, mods4)


def _swiglu_kernel(x_ref, wg_ref, wu_ref, o_ref):
    x = x_ref[...]
    g = jnp.dot(x, wg_ref[...], preferred_element_type=F32)
    up = jnp.dot(x, wu_ref[...], preferred_element_type=F32)
    o_ref[...] = (g * _sigmoid(g) * up).astype(o_ref.dtype)


def _swiglu_call(x, w3, layer, tm, tn):
    b, r, k = x.shape
    hid = w3.shape[-1] // 2
    nb = hid // tn
    return pl.pallas_call(
        _swiglu_kernel,
        grid=(nb, b, r // tm),
        in_specs=[
            pl.BlockSpec((None, tm, k), lambda j, bi, i: (bi, i, 0)),
            pl.BlockSpec((None, k, tn), lambda j, bi, i: (layer, 0, j)),
            pl.BlockSpec((None, k, tn), lambda j, bi, i: (layer, 0, nb + j)),
        ],
        out_specs=pl.BlockSpec((None, tm, tn), lambda j, bi, i: (bi, i, j)),
        out_shape=jax.ShapeDtypeStruct((b, r, hid), BF16),
        compiler_params=_cparams(3),
        name="ffn_in",
    )(x, w3, w3)


def _norm_rope(x, gain, cs, sn):
    var = jnp.mean(x * x, axis=-1, keepdims=True)
    y = x * lax.rsqrt(var + NORM_EPS) * gain
    if cs is None:
        return y
    lane = lax.broadcasted_iota(jnp.int32, y.shape, 1)
    nxt = pltpu.roll(y, ATT_HD - 1, 1)
    prv = pltpu.roll(y, 1, 1)
    partner = jnp.where(lane % 2 == 0, nxt, prv)
    return y * cs + partner * sn


def _attn_kernel(q_ref, k_ref, v_ref, qg_ref, kg_ref, cos_ref, sin_ref, o_ref, kt_s, v_s, *, n_ctx, tq):
    qi = pl.program_id(2)
    rows = k_ref.shape[0]
    scale = ATT_HD ** -0.5

    @pl.when(qi == 0)
    def _prepare_kv():
        kg = kg_ref[...]
        for c in range(rows // tq):
            kc = k_ref[c * tq:(c + 1) * tq, :]
            if c == 0:
                kn = _norm_rope(kc, kg, None, None)
            else:
                kn = _norm_rope(kc, kg, cos_ref[(c - 1) * tq:c * tq, :], sin_ref[(c - 1) * tq:c * tq, :])
            kt_s[:, c * tq:(c + 1) * tq] = kn.T.astype(BF16)
        v_s[...] = v_ref[...].astype(BF16)

    def attend(cs, sn, n_keys):
        qg = qg_ref[...]
        for g in range(ATT_GROUP):
            q = _norm_rope(q_ref[:, g * ATT_HD:(g + 1) * ATT_HD], qg, cs, sn) * scale
            s = jnp.dot(q.astype(BF16), kt_s[:, :n_keys], preferred_element_type=F32)
            m = jnp.max(s, axis=-1, keepdims=True)
            p = jnp.exp(s - m)
            den = jnp.sum(p, axis=-1, keepdims=True)
            o = jnp.dot(p.astype(BF16), v_s[:n_keys, :], preferred_element_type=F32) / den
            o_ref[:, g * ATT_HD:(g + 1) * ATT_HD] = o.astype(o_ref.dtype)

    @pl.when(qi == 0)
    def _context_queries():
        attend(None, None, n_ctx)

    @pl.when(qi > 0)
    def _latent_queries():
        off = pl.multiple_of((qi - 1) * tq, tq)
        attend(cos_ref[pl.ds(off, tq), :], sin_ref[pl.ds(off, tq), :], rows)


def _attn_call(p, q_g, k_g, cos_t, sin_t, layer, n_ctx):
    b, r, _ = p.shape
    tq = n_ctx
    t = r - n_ctx
    qw = ATT_GROUP * ATT_HD
    return pl.pallas_call(
        functools.partial(_attn_kernel, n_ctx=n_ctx, tq=tq),
        grid=(b, ATT_KV_HEADS, r // tq),
        in_specs=[
            pl.BlockSpec((None, tq, qw), lambda bi, kv, qi: (bi, qi, COL_AQ // qw + kv)),
            pl.BlockSpec((None, r, ATT_HD), lambda bi, kv, qi: (bi, 0, COL_AK // ATT_HD + kv)),
            pl.BlockSpec((None, r, ATT_HD), lambda bi, kv, qi: (bi, 0, COL_AV // ATT_HD + kv)),
            pl.BlockSpec((None, 1, ATT_HD), lambda bi, kv, qi: (layer, 0, 0)),
            pl.BlockSpec((None, 1, ATT_HD), lambda bi, kv, qi: (layer, 0, 0)),
            pl.BlockSpec((t, ATT_HD), lambda bi, kv, qi: (0, 0)),
            pl.BlockSpec((t, ATT_HD), lambda bi, kv, qi: (0, 0)),
        ],
        out_specs=pl.BlockSpec((None, tq, qw), lambda bi, kv, qi: (bi, qi, kv)),
        out_shape=jax.ShapeDtypeStruct((b, r, ATT_Q), BF16),
        scratch_shapes=[pltpu.VMEM((ATT_HD, r), BF16), pltpu.VMEM((r, ATT_HD), BF16)],
        compiler_params=_cparams(3),
        name="gqa",
    )(p, p, p, q_g, k_g, cos_t, sin_t)


def _gla_kernel(*refs, reverse, fuse_out, n_chunks):
    if fuse_out:
        q_ref, k_ref, v_ref, dec_ref, wd_ref, bd_ref, ob_ref, r_ref, ng_ref, o_ref, st_s = refs
    else:
        q_ref, k_ref, v_ref, dec_ref, wd_ref, bd_ref, o_ref, st_s = refs
    c_len = GLA_CHUNK

    @pl.when(pl.program_id(2) == 0)
    def _zero_state():
        st_s[...] = jnp.zeros_like(st_s)

    ri = lax.broadcasted_iota(jnp.int32, (c_len, c_len), 0)
    ci = lax.broadcasted_iota(jnp.int32, (c_len, c_len), 1)
    tri = (ci >= ri) if reverse else (ri >= ci)
    tri_b = jnp.where(tri, 1.0, 0.0).astype(BF16)
    wd = wd_ref[...].astype(BF16)
    bd = bd_ref[...]
    nt = (((1,), (1,)), ((), ()))
    tn = (((0,), (0,)), ((), ()))
    chunk_order = range(n_chunks - 1, -1, -1) if reverse else range(n_chunks)
    for c in chunk_order:
        sl = slice(c * c_len, (c + 1) * c_len)
        x = jnp.dot(dec_ref[sl, :].astype(BF16), wd, preferred_element_type=F32) + bd
        g = (jnp.minimum(x, 0.0) - jnp.log1p(jnp.exp(-jnp.abs(x)))) * (1.0 / GLA_GATE_NORM)
        g_hi = g.astype(BF16)
        g_lo = (g - g_hi.astype(F32)).astype(BF16)
        bcum = (jnp.dot(tri_b, g_hi, preferred_element_type=F32)
                + jnp.dot(tri_b, g_lo, preferred_element_type=F32))
        b_last = bcum[0:1, :] if reverse else bcum[c_len - 1:c_len, :]
        q = q_ref[sl, :] * (GLA_HDK ** -0.5)
        k = k_ref[sl, :]
        v = v_ref[sl, :].astype(BF16)
        q_dec = (q * jnp.exp(bcum)).astype(BF16)
        k_inv = (k * jnp.exp(-bcum)).astype(BF16)
        k_dec = (k * jnp.exp(b_last - bcum)).astype(BF16)
        att = lax.dot_general(q_dec, k_inv, nt, preferred_element_type=F32)
        att = jnp.where(tri, att, 0.0).astype(BF16)
        st = st_s[...]
        o = (jnp.dot(att, v, preferred_element_type=F32)
             + lax.dot_general(q_dec, st.astype(BF16), nt, preferred_element_type=F32))
        st_s[...] = st * jnp.exp(b_last) + lax.dot_general(v, k_dec, tn, preferred_element_type=F32)
        if fuse_out:
            o = o + ob_ref[sl, :]
            var = jnp.mean(o * o, axis=-1, keepdims=True)
            y = o * lax.rsqrt(var + NORM_EPS) * ng_ref[...]
            r = r_ref[sl, :]
            o_ref[sl, :] = (y * (r * _sigmoid(r))).astype(o_ref.dtype)
        else:
            o_ref[sl, :] = o


def _gla_call(p, dec, wd_pad, bd, layer, direction, n_ctx, o_back=None, norm_g=None):
    b, r, _ = p.shape
    tb = n_ctx
    nblk = r // tb
    reverse = direction == 1
    fuse_out = o_back is not None

    def blk(t):
        if not reverse:
            return t
        return jnp.where(t == 0, 0, nblk - t)

    in_specs = [
        pl.BlockSpec((None, tb, GLA_HDK), lambda bi, h, t: (bi, blk(t), COL_GQ // GLA_HDK + h)),
        pl.BlockSpec((None, tb, GLA_HDK), lambda bi, h, t: (bi, blk(t), COL_GK // GLA_HDK + h)),
        pl.BlockSpec((None, tb, GLA_HDV), lambda bi, h, t: (bi, blk(t), COL_GV // GLA_HDV + h)),
        pl.BlockSpec((None, tb, DEC_PAD), lambda bi, h, t: (bi, blk(t), 0)),
        pl.BlockSpec((None, None, DEC_PAD, GLA_HDK), lambda bi, h, t: (layer, direction, 0, h)),
        pl.BlockSpec((None, None, 1, GLA_HDK), lambda bi, h, t: (layer, direction, 0, h)),
    ]
    args = [p, p, p, dec, wd_pad, bd]
    if fuse_out:
        in_specs += [
            pl.BlockSpec((None, tb, GLA_HDV), lambda bi, h, t: (bi, blk(t), h)),
            pl.BlockSpec((None, tb, GLA_HDV), lambda bi, h, t: (bi, blk(t), COL_GR // GLA_HDV + h)),
            pl.BlockSpec((None, 1, GLA_HDV), lambda bi, h, t: (layer, 0, 0)),
        ]
        args += [o_back, p, norm_g]
    return pl.pallas_call(
        functools.partial(_gla_kernel, reverse=reverse, fuse_out=fuse_out, n_chunks=tb // GLA_CHUNK),
        grid=(b, GLA_HEADS, nblk),
        in_specs=in_specs,
        out_specs=pl.BlockSpec((None, tb, GLA_HDV), lambda bi, h, t: (bi, blk(t), h)),
        out_shape=jax.ShapeDtypeStruct((b, r, GLA_DV), BF16 if fuse_out else F32),
        scratch_shapes=[pltpu.VMEM((GLA_HDV, GLA_HDK), F32)],
        compiler_params=_cparams(3),
        name="gla_fwd" if fuse_out else "gla_bwd",
    )(*args)


LRU_PAD = 8


def _lru_kernel(x_ref, y_ref, cw_ref, cb_ref, wa_ref, ba_ref, wi_ref, bi_ref, lam_ref, o_ref,
                xp_s, af_s, bf_s, ab_s, bb_s, *, n_ctx, tc, tr):
    rows = x_ref.shape[0]
    n_lat = rows - n_ctx
    ctx0 = LRU_PAD
    lat0 = 2 * LRU_PAD + n_ctx
    zpad = jnp.zeros((LRU_PAD, tc), F32)
    xp_s[0:LRU_PAD, :] = zpad
    xp_s[ctx0 + n_ctx:lat0, :] = zpad
    xp_s[lat0 + n_lat:lat0 + n_lat + LRU_PAD, :] = zpad
    xp_s[ctx0:ctx0 + n_ctx, :] = x_ref[0:n_ctx, :]
    xp_s[lat0:lat0 + n_lat, :] = x_ref[n_ctx:rows, :]

    cw = cw_ref[...]
    cb = cb_ref[...]
    lam = lam_ref[...]
    neg_sp = -(jnp.maximum(-lam, 0.0) + jnp.log1p(jnp.exp(-jnp.abs(lam))))
    row_id = lax.broadcasted_iota(jnp.int32, (tr, 1), 0)

    for c in range(rows // tr):
        r0 = c * tr
        p0 = (ctx0 if r0 < n_ctx else lat0 - n_ctx) + r0
        xc = cb
        for j in range(CONV_W):
            xc = xc + xp_s[p0 + j - CONV_LEFT:p0 + j - CONV_LEFT + tr, :] * cw[j:j + 1, :]
        for d, (a_s, b_s) in enumerate(((af_s, bf_s), (ab_s, bb_s))):
            ra, ig = [], []
            for kb in range(tc // LRU_BS):
                xb = xc[:, kb * LRU_BS:(kb + 1) * LRU_BS].astype(BF16)
                ra.append(jnp.dot(xb, wa_ref[d, kb].astype(BF16), preferred_element_type=F32))
                ig.append(jnp.dot(xb, wi_ref[d, kb].astype(BF16), preferred_element_type=F32))
            ra = _sigmoid(jnp.concatenate(ra, axis=1) + ba_ref[d:d + 1, :])
            ig = _sigmoid(jnp.concatenate(ig, axis=1) + bi_ref[d:d + 1, :])
            log_a = LRU_C * ra * neg_sp[d:d + 1, :]
            th = jnp.tanh(log_a)
            mult = jnp.sqrt(-2.0 * th / (1.0 - th))
            first = 0 if d == 0 else n_ctx - 1
            if r0 <= first < r0 + tr:
                mult = jnp.where(row_id == first - r0, 1.0, mult)
            a_s[r0:r0 + tr, :] = jnp.exp(log_a)
            b_s[r0:r0 + tr, :] = mult * ig * xc

    def step(tf, tb, carry):
        hf, hb = carry
        hf = af_s[pl.ds(tf, 1), :] * hf + bf_s[pl.ds(tf, 1), :]
        hb = ab_s[pl.ds(tb, 1), :] * hb + bb_s[pl.ds(tb, 1), :]
        bf_s[pl.ds(tf, 1), :] = hf
        bb_s[pl.ds(tb, 1), :] = hb
        return hf, hb

    zero = jnp.zeros((1, tc), F32)
    carry = lax.fori_loop(0, n_ctx, lambda s, cr: step(s, n_ctx - 1 - s, cr), (zero, zero), unroll=8)
    lax.fori_loop(n_ctx, rows, lambda s, cr: step(s, rows - 1 + n_ctx - s, cr), carry, unroll=8)

    k0 = 0.7978845608028654
    for c in range(rows // tr):
        sl = slice(c * tr, (c + 1) * tr)
        y = y_ref[sl, :]
        gelu = 0.5 * y * (1.0 + jnp.tanh(k0 * (y + 0.044715 * (y * y * y))))
        o_ref[sl, :] = ((bf_s[sl, :] + bb_s[sl, :]) * gelu).astype(o_ref.dtype)


def _lru_call(p, conv_w, conv_b, w_a, b_a, w_i, b_i, lam, layer, n_ctx):
    b, r, _ = p.shape
    tc = 256
    nb = tc // LRU_BS
    chan = lambda bi, j: (layer, 0, j)
    return pl.pallas_call(
        functools.partial(_lru_kernel, n_ctx=n_ctx, tc=tc, tr=n_ctx),
        grid=(b, LRU_W // tc),
        in_specs=[
            pl.BlockSpec((None, r, tc), lambda bi, j: (bi, 0, COL_LX // tc + j)),
            pl.BlockSpec((None, r, tc), lambda bi, j: (bi, 0, COL_LY // tc + j)),
            pl.BlockSpec((None, CONV_W, tc), chan),
            pl.BlockSpec((None, 1, tc), chan),
            pl.BlockSpec((None, 2, nb, LRU_BS, LRU_BS), lambda bi, j: (layer, 0, j, 0, 0)),
            pl.BlockSpec((None, 2, tc), chan),
            pl.BlockSpec((None, 2, nb, LRU_BS, LRU_BS), lambda bi, j: (layer, 0, j, 0, 0)),
            pl.BlockSpec((None, 2, tc), chan),
            pl.BlockSpec((None, 2, tc), chan),
        ],
        out_specs=pl.BlockSpec((None, r, tc), lambda bi, j: (bi, 0, j)),
        out_shape=jax.ShapeDtypeStruct((b, r, LRU_W), BF16),
        scratch_shapes=[pltpu.VMEM((r + 3 * LRU_PAD, tc), F32)] + [pltpu.VMEM((r, tc), F32)] * 4,
        compiler_params=_cparams(2),
        name="rglru",
    )(p, p, conv_w, conv_b, w_a, b_a, w_i, b_i, lam)


def _merge_kernel(xa_ref, xb_ref, xc_ref, wa_ref, wb_ref, wc_ref, ga_ref, gb_ref, gc_ref, bm_ref, o_ref):
    acc = None
    for n, (x_ref, w_ref, g_ref) in enumerate(((xa_ref, wa_ref, ga_ref), (xb_ref, wb_ref, gb_ref),
                                               (xc_ref, wc_ref, gc_ref))):
        proj = jnp.dot(x_ref[...], w_ref[...], preferred_element_type=F32)
        term = _sigmoid(g_ref[...] + bm_ref[n:n + 1, :]) * proj
        acc = term if acc is None else acc + term
    o_ref[...] = acc.astype(o_ref.dtype)


def _merge_call(o_gla, o_att, o_lru, wbr, p, b_merge, layer, tm, tn):
    b, r, d = o_gla.shape
    ncol = d // tn
    x_spec = pl.BlockSpec((None, tm, d), lambda j, bi, i: (bi, i, 0))

    def w_spec(n):
        return pl.BlockSpec((None, None, d, tn), lambda j, bi, i: (layer, n, 0, j))

    def g_spec(n):
        return pl.BlockSpec((None, tm, tn), lambda j, bi, i: (bi, i, COL_GT // tn + n * ncol + j))

    return pl.pallas_call(
        _merge_kernel,
        grid=(ncol, b, r // tm),
        in_specs=[x_spec, x_spec, x_spec, w_spec(0), w_spec(1), w_spec(2), g_spec(0), g_spec(1), g_spec(2),
                  pl.BlockSpec((None, N_BRANCH, tn), lambda j, bi, i: (layer, 0, j))],
        out_specs=pl.BlockSpec((None, tm, tn), lambda j, bi, i: (bi, i, j)),
        out_shape=jax.ShapeDtypeStruct((b, r, d), BF16),
        compiler_params=_cparams(3),
        name="merge",
    )(o_gla, o_att, o_lru, wbr, wbr, wbr, p, p, p, b_merge)


def _final_norm_kernel(h_ref, g_ref, o_ref):
    x = h_ref[...]
    var = jnp.mean(x * x, axis=-1, keepdims=True)
    o_ref[...] = x * lax.rsqrt(var + NORM_EPS) * g_ref[...]


def _final_norm_call(h, g, n_ctx):
    b, r, d = h.shape
    tr = n_ctx
    return pl.pallas_call(
        _final_norm_kernel,
        grid=(b, (r - n_ctx) // tr),
        in_specs=[
            pl.BlockSpec((None, tr, d), lambda bi, i: (bi, i + 1, 0)),
            pl.BlockSpec((1, d), lambda bi, i: (0, 0)),
        ],
        out_specs=pl.BlockSpec((None, tr, d), lambda bi, i: (bi, i, 0)),
        out_shape=jax.ShapeDtypeStruct((b, r - n_ctx, d), F32),
        compiler_params=_cparams(2),
        name="final_norm",
    )(h, g)


def _rope_tables(n_tokens):
    rows = n_tokens // GRID_W
    row = jnp.repeat(jnp.arange(rows, dtype=F32), GRID_W)
    col = jnp.tile(jnp.arange(GRID_W, dtype=F32), rows)
    inv = ROPE_THETA ** (-jnp.arange(ROPE_AXIS_PAIRS, dtype=F32) / ROPE_AXIS_PAIRS)
    ang = jnp.concatenate([row[:, None] * inv, col[:, None] * inv], axis=-1)
    cos, sin = jnp.cos(ang), jnp.sin(ang)
    cos_t = jnp.repeat(cos, 2, axis=-1)
    sin_t = jnp.stack([-sin, sin], axis=-1).reshape(n_tokens, ATT_HD)
    return cos_t, sin_t


def kernel(x, c, ctx, c_ctx, w_mod, b_mod, norm_mix_g, norm_ffn_g, w_in, gla_w_decay, gla_b_decay, gla_norm_g, q_norm_g, k_norm_g, conv_w, conv_b, lru_w_a, lru_b_a, lru_w_i, lru_b_i, lru_lambda, b_merge, w_branch, w_out, w_ffn_in, w_ffn_out, final_norm_g):
    depth = w_in.shape[0]
    batch, t_lat, d = x.shape
    n_ctx = ctx.shape[1]
    rows = n_ctx + t_lat
    ctx_row = batch
    assert batch < MOD_ROWS and rows % n_ctx == 0 and n_ctx % GLA_CHUNK == 0
    tm = 768 if rows % 768 == 0 else n_ctx

    w_main = jnp.concatenate([w_in[:, :, :DEC_ORIG], w_in[:, :, DEC_ORIG + 2 * GLA_RANK:]], axis=-1).astype(BF16)
    w_dec = jnp.pad(w_in[:, :, DEC_ORIG:DEC_ORIG + 2 * GLA_RANK],
                    ((0, 0), (0, 0), (0, DEC_PAD - 2 * GLA_RANK))).astype(BF16)
    w_branch_b = w_branch.astype(BF16)
    w_out_b = w_out.astype(BF16)
    w_ffn_in_b = w_ffn_in.astype(BF16)
    w_ffn_out_b = w_ffn_out.astype(BF16)
    wd_pad = jnp.zeros((depth, 2, DEC_PAD, GLA_DK), F32)
    wd_pad = wd_pad.at[:, 0, :GLA_RANK].set(gla_w_decay[:, 0])
    wd_pad = wd_pad.at[:, 1, GLA_RANK:2 * GLA_RANK].set(gla_w_decay[:, 1])
    bd = gla_b_decay.reshape(depth, 2, 1, GLA_DK)
    cos_t, sin_t = _rope_tables(t_lat)

    c16 = jnp.zeros((MOD_ROWS, d), F32).at[:batch].set(c).at[ctx_row].set(c_ctx)
    mods4 = _mods_call(c16, w_mod, b_mod).reshape(depth, MOD_ROWS, 1, N_MOD * d)

    g_mix = norm_mix_g.reshape(depth, 1, d)
    g_ffn = norm_ffn_g.reshape(depth, 1, d)
    g_gla = gla_norm_g.reshape(depth, 1, GLA_HDV)
    g_q = q_norm_g.reshape(depth, 1, ATT_HD)
    g_k = k_norm_g.reshape(depth, 1, ATT_HD)
    cb3 = conv_b.reshape(depth, 1, LRU_W)

    h = jnp.concatenate([ctx, x], axis=1)
    for l in range(depth):
        u = _normmod_call(h, g_mix, mods4, l, 0, 1, n_ctx, ctx_row)
        p = _linear_call(u, w_main, l, F32, tm, 512, "in_proj")
        dec = _linear_call(u, w_dec, l, F32, tm, DEC_PAD, "dec_proj")
        o_att = _attn_call(p, g_q, g_k, cos_t, sin_t, l, n_ctx)
        o_back = _gla_call(p, dec, wd_pad, bd, l, 1, n_ctx)
        o_gla = _gla_call(p, dec, wd_pad, bd, l, 0, n_ctx, o_back, g_gla)
        o_lru = _lru_call(p, conv_w, cb3, lru_w_a, lru_b_a, lru_w_i, lru_b_i, lru_lambda, l, n_ctx)
        merged = _merge_call(o_gla, o_att, o_lru, w_branch_b, p, b_merge, l, tm, 512)
        h = _linear_res_call(merged, w_out_b, l, h, mods4, 2, n_ctx, ctx_row, tm, 512, "out_proj")
        u2 = _normmod_call(h, g_ffn, mods4, l, 3, 4, n_ctx, ctx_row)
        a = _swiglu_call(u2, w_ffn_in_b, l, tm, 512)
        h = _linear_res_call(a, w_ffn_out_b, l, h, mods4, 5, n_ctx, ctx_row, tm, 512, "ffn_out")
    return _final_norm_call(h, final_norm_g.reshape(1, d), n_ctx)
```

```python
import functools

import jax
import jax.numpy as jnp
import numpy as np
from jax import lax
from jax.experimental import pallas as pl
from jax.experimental.pallas import tpu as pltpu

F32 = jnp.float32
BF16 = jnp.bfloat16

D_MODEL = 2048
GRID_W = 64
NORM_EPS = 1e-6
N_MOD = 6

GLA_HEADS = 4
GLA_DK = D_MODEL // 2
GLA_DV = D_MODEL
GLA_HDK = GLA_DK // GLA_HEADS
GLA_HDV = GLA_DV // GLA_HEADS
GLA_RANK = 16
GLA_GATE_NORM = 16.0
GLA_CHUNK = 64

ATT_HEADS = 16
ATT_KV_HEADS = 4
ATT_HD = 128
ATT_GROUP = ATT_HEADS // ATT_KV_HEADS
ATT_Q = ATT_HEADS * ATT_HD
ATT_KV = ATT_KV_HEADS * ATT_HD
ROPE_THETA = 10000.0
ROPE_AXIS_PAIRS = ATT_HD // 4

LRU_W = D_MODEL
LRU_BLOCKS = 16
LRU_BS = LRU_W // LRU_BLOCKS
LRU_C = 8.0
CONV_W = 4
CONV_LEFT = 2

N_BRANCH = 3
FFN_HIDDEN = -(-8 * D_MODEL // (3 * 256)) * 256

COL_GQ = 0
COL_GK = COL_GQ + GLA_DK
COL_GV = COL_GK + GLA_DK
COL_GR = COL_GV + GLA_DV
COL_AQ = COL_GR + GLA_DV
COL_AK = COL_AQ + ATT_Q
COL_AV = COL_AK + ATT_KV
COL_LX = COL_AV + ATT_KV
COL_LY = COL_LX + LRU_W
COL_GT = COL_LY + LRU_W
N_MAIN = COL_GT + N_BRANCH * D_MODEL
DEC_ORIG = 2 * GLA_DK + 2 * GLA_DV
DEC_PAD = 128

V7X_VMEM_LIMIT_BYTES = 56 * 1024 * 1024
MOD_ROWS = 16


def _cparams(n_axes, vmem=V7X_VMEM_LIMIT_BYTES):
    return pltpu.CompilerParams(dimension_semantics=("arbitrary",) * n_axes, vmem_limit_bytes=vmem)


def _sigmoid(x):
    return 0.5 * jnp.tanh(0.5 * x) + 0.5


def _mods_kernel(c_ref, w_ref, b_ref, o_ref):
    c = c_ref[...]
    sc = (c * _sigmoid(c)).astype(BF16)
    o_ref[...] = jnp.dot(sc, w_ref[...].astype(BF16), preferred_element_type=F32) + b_ref[...]


def _mods_call(c16, w_mod, b_mod):
    depth, d, n = w_mod.shape
    tn = 1024
    return pl.pallas_call(
        _mods_kernel,
        grid=(depth, n // tn),
        in_specs=[
            pl.BlockSpec((MOD_ROWS, d), lambda l, j: (0, 0)),
            pl.BlockSpec((None, d, tn), lambda l, j: (l, 0, j)),
            pl.BlockSpec((None, 1, tn), lambda l, j: (l, 0, j)),
        ],
        out_specs=pl.BlockSpec((None, MOD_ROWS, tn), lambda l, j: (l, 0, j)),
        out_shape=jax.ShapeDtypeStruct((depth, MOD_ROWS, n), F32),
        compiler_params=_cparams(2),
        name="mods",
    )(c16, w_mod, b_mod.reshape(depth, 1, n))


def _win_prep_kernel(a_ref, b_ref, o_ref, *, n_plain, shift):
    j = pl.program_id(1)

    @pl.when(j < n_plain)
    def _before_decay_columns():
        o_ref[...] = a_ref[...].astype(o_ref.dtype)

    @pl.when(j >= n_plain)
    def _after_decay_columns():
        o_ref[...] = jnp.concatenate([a_ref[:, shift:], b_ref[:, :shift]], axis=1).astype(o_ref.dtype)


def _win_prep_call(w_in):
    depth, d, _ = w_in.shape
    tn, tail = 512, 128
    assert DEC_ORIG % tn == 0 and N_MAIN % tn == 0 and 2 * GLA_RANK <= tail
    return pl.pallas_call(
        functools.partial(_win_prep_kernel, n_plain=DEC_ORIG // tn, shift=2 * GLA_RANK),
        grid=(depth, N_MAIN // tn),
        in_specs=[
            pl.BlockSpec((None, d, tn), lambda l, j: (l, 0, j)),
            pl.BlockSpec((None, d, tail), lambda l, j: (l, 0, (j + 1) * (tn // tail))),
        ],
        out_specs=pl.BlockSpec((None, d, tn), lambda l, j: (l, 0, j)),
        out_shape=jax.ShapeDtypeStruct((depth, d, N_MAIN), BF16),
        compiler_params=_cparams(2),
        name="w_in_prep",
    )(w_in, w_in)


def _normmod_kernel(h_ref, g_ref, sh_ref, sc_ref, o_ref):
    x = h_ref[...]
    var = jnp.mean(x * x, axis=-1, keepdims=True)
    y = x * lax.rsqrt(var + NORM_EPS) * g_ref[...]
    o_ref[...] = (y * (1.0 + sc_ref[...]) + sh_ref[...]).astype(o_ref.dtype)


def _normmod_call(h, g, mods4, layer, k_shift, k_scale, n_ctx, ctx_row):
    b, r, d = h.shape
    tr = n_ctx

    def mod_map(k):
        return lambda bi, i: (layer, jnp.where(i == 0, ctx_row, bi), 0, k)

    return pl.pallas_call(
        _normmod_kernel,
        grid=(b, r // tr),
        in_specs=[
            pl.BlockSpec((None, tr, d), lambda bi, i: (bi, i, 0)),
            pl.BlockSpec((None, 1, d), lambda bi, i: (layer, 0, 0)),
            pl.BlockSpec((None, None, 1, d), mod_map(k_shift)),
            pl.BlockSpec((None, None, 1, d), mod_map(k_scale)),
        ],
        out_specs=pl.BlockSpec((None, tr, d), lambda bi, i: (bi, i, 0)),
        out_shape=jax.ShapeDtypeStruct((b, r, d), BF16),
        compiler_params=_cparams(2),
        name="normmod",
    )(h, g, mods4, mods4)


def _linear_kernel(x_ref, w_ref, o_ref):
    o_ref[...] = jnp.dot(x_ref[...], w_ref[...], preferred_element_type=F32).astype(o_ref.dtype)


def _linear_call(x, w3, layer, out_dtype, tm, tn, name):
    b, r, k = x.shape
    n = w3.shape[-1]
    return pl.pallas_call(
        _linear_kernel,
        grid=(n // tn, b, r // tm),
        in_specs=[
            pl.BlockSpec((None, tm, k), lambda j, bi, i: (bi, i, 0)),
            pl.BlockSpec((None, k, tn), lambda j, bi, i: (layer, 0, j)),
        ],
        out_specs=pl.BlockSpec((None, tm, tn), lambda j, bi, i: (bi, i, j)),
        out_shape=jax.ShapeDtypeStruct((b, r, n), out_dtype),
        compiler_params=_cparams(3),
        name=name,
    )(x, w3)


def _linear_res_kernel(x_ref, w_ref, h_ref, gc_ref, gl_ref, o_ref, *, tm, n_ctx):
    i = pl.program_id(2)
    y = jnp.dot(x_ref[...], w_ref[...], preferred_element_type=F32)
    rows = i * tm + lax.broadcasted_iota(jnp.int32, (tm, 1), 0)
    gate = jnp.where(rows < n_ctx, gc_ref[...], gl_ref[...])
    o_ref[...] = h_ref[...] + gate * y


def _linear_res_call(x, w3, layer, h, mods4, k_gate, n_ctx, ctx_row, tm, tn, name):
    b, r, k = x.shape
    n = w3.shape[-1]
    ngate = n // tn
    return pl.pallas_call(
        functools.partial(_linear_res_kernel, tm=tm, n_ctx=n_ctx),
        grid=(n // tn, b, r // tm),
        in_specs=[
            pl.BlockSpec((None, tm, k), lambda j, bi, i: (bi, i, 0)),
            pl.BlockSpec((None, k, tn), lambda j, bi, i: (layer, 0, j)),
            pl.BlockSpec((None, tm, tn), lambda j, bi, i: (bi, i, j)),
            pl.BlockSpec((None, None, 1, tn), lambda j, bi, i: (layer, ctx_row, 0, k_gate * ngate + j)),
            pl.BlockSpec((None, None, 1, tn), lambda j, bi, i: (layer, bi, 0, k_gate * ngate + j)),
        ],
        out_specs=pl.BlockSpec((None, tm, tn), lambda j, bi, i: (bi, i, j)),
        out_shape=jax.ShapeDtypeStruct((b, r, n), F32),
        compiler_params=_cparams(3),
        name=name,
    )(x, w3, h, mods4, mods4)


def _swiglu_kernel(x_ref, wg_ref, wu_ref, o_ref):
    x = x_ref[...]
    g = jnp.dot(x, wg_ref[...], preferred_element_type=F32)
    up = jnp.dot(x, wu_ref[...], preferred_element_type=F32)
    o_ref[...] = (g * _sigmoid(g) * up).astype(o_ref.dtype)


def _swiglu_call(x, w3, layer, tm, tn):
    b, r, k = x.shape
    hid = w3.shape[-1] // 2
    nb = hid // tn
    return pl.pallas_call(
        _swiglu_kernel,
        grid=(nb, b, r // tm),
        in_specs=[
            pl.BlockSpec((None, tm, k), lambda j, bi, i: (bi, i, 0)),
            pl.BlockSpec((None, k, tn), lambda j, bi, i: (layer, 0, j)),
            pl.BlockSpec((None, k, tn), lambda j, bi, i: (layer, 0, nb + j)),
        ],
        out_specs=pl.BlockSpec((None, tm, tn), lambda j, bi, i: (bi, i, j)),
        out_shape=jax.ShapeDtypeStruct((b, r, hid), BF16),
        compiler_params=_cparams(3),
        name="ffn_in",
    )(x, w3, w3)


def _norm_rope(x, gain, cs, sn):
    var = jnp.mean(x * x, axis=-1, keepdims=True)
    y = x * lax.rsqrt(var + NORM_EPS) * gain
    if cs is None:
        return y
    lane = lax.broadcasted_iota(jnp.int32, y.shape, 1)
    nxt = pltpu.roll(y, ATT_HD - 1, 1)
    prv = pltpu.roll(y, 1, 1)
    partner = jnp.where(lane % 2 == 0, nxt, prv)
    return y * cs + partner * sn


def _attn_kernel(q_ref, k_ref, v_ref, qg_ref, kg_ref, cos_ref, sin_ref, o_ref, kt_s, *, n_ctx, tq):
    qi = pl.program_id(2)
    rows = k_ref.shape[0]
    scale = ATT_HD ** -0.5

    @pl.when(qi == 0)
    def _prepare_k():
        kg = kg_ref[...]
        for c in range(rows // tq):
            kc = k_ref[c * tq:(c + 1) * tq, :].astype(F32)
            if c == 0:
                kn = _norm_rope(kc, kg, None, None)
            else:
                kn = _norm_rope(kc, kg, cos_ref[(c - 1) * tq:c * tq, :], sin_ref[(c - 1) * tq:c * tq, :])
            kt_s[:, c * tq:(c + 1) * tq] = kn.T.astype(BF16)

    def attend(cs, sn, n_keys):
        qg = qg_ref[...]
        for g in range(ATT_GROUP):
            q = _norm_rope(q_ref[:, g * ATT_HD:(g + 1) * ATT_HD].astype(F32), qg, cs, sn) * scale
            s = jnp.dot(q.astype(BF16), kt_s[:, :n_keys], preferred_element_type=F32)
            m = jnp.max(s, axis=-1, keepdims=True)
            p = jnp.exp(s - m)
            den = jnp.sum(p, axis=-1, keepdims=True)
            o = jnp.dot(p.astype(BF16), v_ref[:n_keys, :], preferred_element_type=F32) / den
            o_ref[:, g * ATT_HD:(g + 1) * ATT_HD] = o.astype(o_ref.dtype)

    @pl.when(qi == 0)
    def _context_queries():
        attend(None, None, n_ctx)

    @pl.when(qi > 0)
    def _latent_queries():
        off = pl.multiple_of((qi - 1) * tq, tq)
        attend(cos_ref[pl.ds(off, tq), :], sin_ref[pl.ds(off, tq), :], rows)


def _attn_call(p, q_g, k_g, cos_t, sin_t, layer, n_ctx):
    b, r, _ = p.shape
    tq = n_ctx
    t = r - n_ctx
    qw = ATT_GROUP * ATT_HD
    return pl.pallas_call(
        functools.partial(_attn_kernel, n_ctx=n_ctx, tq=tq),
        grid=(b, ATT_KV_HEADS, r // tq),
        in_specs=[
            pl.BlockSpec((None, tq, qw), lambda bi, kv, qi: (bi, qi, COL_AQ // qw + kv)),
            pl.BlockSpec((None, r, ATT_HD), lambda bi, kv, qi: (bi, 0, COL_AK // ATT_HD + kv)),
            pl.BlockSpec((None, r, ATT_HD), lambda bi, kv, qi: (bi, 0, COL_AV // ATT_HD + kv)),
            pl.BlockSpec((None, 1, ATT_HD), lambda bi, kv, qi: (layer, 0, 0)),
            pl.BlockSpec((None, 1, ATT_HD), lambda bi, kv, qi: (layer, 0, 0)),
            pl.BlockSpec((t, ATT_HD), lambda bi, kv, qi: (0, 0)),
            pl.BlockSpec((t, ATT_HD), lambda bi, kv, qi: (0, 0)),
        ],
        out_specs=pl.BlockSpec((None, tq, qw), lambda bi, kv, qi: (bi, qi, kv)),
        out_shape=jax.ShapeDtypeStruct((b, r, ATT_Q), BF16),
        scratch_shapes=[pltpu.VMEM((ATT_HD, r), BF16)],
        compiler_params=_cparams(3),
        name="gqa",
    )(p, p, p, q_g, k_g, cos_t, sin_t)


def _gla_kernel(*refs, reverse, fuse_out, n_chunks):
    if fuse_out:
        q_ref, k_ref, v_ref, dec_ref, wd_ref, bd_ref, ob_ref, r_ref, ng_ref, o_ref, st_s = refs
    else:
        q_ref, k_ref, v_ref, dec_ref, wd_ref, bd_ref, o_ref, st_s = refs
    c_len = GLA_CHUNK
    tb = n_chunks * c_len

    @pl.when(pl.program_id(2) == 0)
    def _zero_state():
        st_s[...] = jnp.zeros_like(st_s)

    ri = lax.broadcasted_iota(jnp.int32, (tb, tb), 0)
    ci = lax.broadcasted_iota(jnp.int32, (tb, tb), 1)
    log2_c = c_len.bit_length() - 1
    same = jnp.right_shift(ri, log2_c) == jnp.right_shift(ci, log2_c)
    tri = same & ((ci >= ri) if reverse else (ri >= ci))
    tri_b = jnp.where(tri, 1.0, 0.0).astype(BF16)
    nt = (((1,), (1,)), ((), ()))
    tn = (((0,), (0,)), ((), ()))

    x = jnp.dot(dec_ref[...].astype(BF16), wd_ref[...].astype(BF16), preferred_element_type=F32) + bd_ref[...]
    g = (jnp.minimum(x, 0.0) - jnp.log1p(jnp.exp(-jnp.abs(x)))) * (1.0 / GLA_GATE_NORM)
    g_hi = g.astype(BF16)
    g_lo = (g - g_hi.astype(F32)).astype(BF16)
    bcum = jnp.dot(tri_b, g_hi, preferred_element_type=F32) + jnp.dot(tri_b, g_lo, preferred_element_type=F32)
    q = q_ref[...].astype(F32) * (GLA_HDK ** -0.5)
    k = k_ref[...].astype(F32)
    v = v_ref[...]
    q_dec = (q * jnp.exp(bcum)).astype(BF16)
    k_inv = (k * jnp.exp(-bcum)).astype(BF16)
    att = lax.dot_general(q_dec, k_inv, nt, preferred_element_type=F32)
    att = jnp.where(tri, att, 0.0).astype(BF16)
    o_intra = jnp.dot(att, v, preferred_element_type=F32)
    b_last, k_dec = [], []
    for c in range(n_chunks):
        sl = slice(c * c_len, (c + 1) * c_len)
        end = c * c_len if reverse else (c + 1) * c_len - 1
        b_last.append(bcum[end:end + 1, :])
        k_dec.append((k[sl, :] * jnp.exp(b_last[c] - bcum[sl, :])).astype(BF16))

    outs = [None] * n_chunks
    chunk_order = range(n_chunks - 1, -1, -1) if reverse else range(n_chunks)
    for c in chunk_order:
        sl = slice(c * c_len, (c + 1) * c_len)
        st = st_s[...]
        outs[c] = o_intra[sl, :] + lax.dot_general(q_dec[sl, :], st.astype(BF16), nt, preferred_element_type=F32)
        st_s[...] = (st * jnp.exp(b_last[c])
                     + lax.dot_general(v[sl, :], k_dec[c], tn, preferred_element_type=F32))
    o = jnp.concatenate(outs, axis=0)
    if fuse_out:
        o = o + ob_ref[...]
        var = jnp.mean(o * o, axis=-1, keepdims=True)
        y = o * lax.rsqrt(var + NORM_EPS) * ng_ref[...]
        r = r_ref[...].astype(F32)
        o_ref[...] = (y * (r * _sigmoid(r))).astype(o_ref.dtype)
    else:
        o_ref[...] = o


def _gla_call(p, dec, wd_pad, bd, layer, direction, n_ctx, o_back=None, norm_g=None):
    b, r, _ = p.shape
    tb = n_ctx
    nblk = r // tb
    reverse = direction == 1
    fuse_out = o_back is not None

    def blk(t):
        if not reverse:
            return t
        return jnp.where(t == 0, 0, nblk - t)

    in_specs = [
        pl.BlockSpec((None, tb, GLA_HDK), lambda bi, h, t: (bi, blk(t), COL_GQ // GLA_HDK + h)),
        pl.BlockSpec((None, tb, GLA_HDK), lambda bi, h, t: (bi, blk(t), COL_GK // GLA_HDK + h)),
        pl.BlockSpec((None, tb, GLA_HDV), lambda bi, h, t: (bi, blk(t), COL_GV // GLA_HDV + h)),
        pl.BlockSpec((None, tb, DEC_PAD), lambda bi, h, t: (bi, blk(t), 0)),
        pl.BlockSpec((None, None, DEC_PAD, GLA_HDK), lambda bi, h, t: (layer, direction, 0, h)),
        pl.BlockSpec((None, None, 1, GLA_HDK), lambda bi, h, t: (layer, direction, 0, h)),
    ]
    args = [p, p, p, dec, wd_pad, bd]
    if fuse_out:
        in_specs += [
            pl.BlockSpec((None, tb, GLA_HDV), lambda bi, h, t: (bi, blk(t), h)),
            pl.BlockSpec((None, tb, GLA_HDV), lambda bi, h, t: (bi, blk(t), COL_GR // GLA_HDV + h)),
            pl.BlockSpec((None, 1, GLA_HDV), lambda bi, h, t: (layer, 0, 0)),
        ]
        args += [o_back, p, norm_g]
    return pl.pallas_call(
        functools.partial(_gla_kernel, reverse=reverse, fuse_out=fuse_out, n_chunks=tb // GLA_CHUNK),
        grid=(b, GLA_HEADS, nblk),
        in_specs=in_specs,
        out_specs=pl.BlockSpec((None, tb, GLA_HDV), lambda bi, h, t: (bi, blk(t), h)),
        out_shape=jax.ShapeDtypeStruct((b, r, GLA_DV), BF16 if fuse_out else F32),
        scratch_shapes=[pltpu.VMEM((GLA_HDV, GLA_HDK), F32)],
        compiler_params=_cparams(3),
        name="gla_fwd" if fuse_out else "gla_bwd",
    )(*args)


LRU_PAD = 8


def _lru_kernel(x_ref, y_ref, cw_ref, cb_ref, wa_ref, ba_ref, wi_ref, bi_ref, lam_ref, o_ref,
                xp_s, af_s, bf_s, ab_s, bb_s, *, n_ctx, tc, tr):
    rows = x_ref.shape[0]
    n_lat = rows - n_ctx
    sub = 8
    ctx0 = LRU_PAD
    lat0 = 2 * LRU_PAD + n_ctx
    zpad = jnp.zeros((LRU_PAD, tc), F32)
    xp_s[0:LRU_PAD, :] = zpad
    xp_s[ctx0 + n_ctx:lat0, :] = zpad
    xp_s[lat0 + n_lat:lat0 + n_lat + LRU_PAD, :] = zpad
    xp_s[ctx0:ctx0 + n_ctx, :] = x_ref[0:n_ctx, :].astype(F32)
    xp_s[lat0:lat0 + n_lat, :] = x_ref[n_ctx:rows, :].astype(F32)

    cw = cw_ref[...]
    cb = cb_ref[...]
    lam = lam_ref[...]
    neg_sp = -(jnp.maximum(-lam, 0.0) + jnp.log1p(jnp.exp(-jnp.abs(lam))))
    log_a_half = (0.5 * LRU_C) * neg_sp
    row_id = lax.broadcasted_iota(jnp.int32, (tr, 1), 0)
    n_kb = tc // LRU_BS
    wa_h = [[(0.5 * wa_ref[d, kb]).astype(BF16) for kb in range(n_kb)] for d in range(2)]
    wi_h = [[(0.5 * wi_ref[d, kb]).astype(BF16) for kb in range(n_kb)] for d in range(2)]
    ba_h = 0.5 * ba_ref[...]
    bi_h = 0.5 * bi_ref[...]

    for c in range(rows // tr):
        r0 = c * tr
        p0 = (ctx0 if r0 < n_ctx else lat0 - n_ctx) + r0
        n_win = tr + 2 * sub
        win = xp_s[p0 - sub:p0 + tr + sub, :]
        xc = cb + win[sub:sub + tr, :] * cw[CONV_LEFT:CONV_LEFT + 1, :]
        for j in range(CONV_W):
            off = j - CONV_LEFT
            if off != 0:
                xc = xc + pltpu.roll(win, (-off) % n_win, 0)[sub:sub + tr, :] * cw[j:j + 1, :]
        xh = 0.5 * xc
        xb = [xc[:, kb * LRU_BS:(kb + 1) * LRU_BS].astype(BF16) for kb in range(n_kb)]
        for d, (a_s, b_s) in enumerate(((af_s, bf_s), (ab_s, bb_s))):
            za = jnp.concatenate([jnp.dot(xb[kb], wa_h[d][kb], preferred_element_type=F32)
                                  for kb in range(n_kb)], axis=1)
            zi = jnp.concatenate([jnp.dot(xb[kb], wi_h[d][kb], preferred_element_type=F32)
                                  for kb in range(n_kb)], axis=1)
            ta = jnp.tanh(za + ba_h[d:d + 1, :])
            ti = jnp.tanh(zi + bi_h[d:d + 1, :])
            la_h = log_a_half[d:d + 1, :]
            a = jnp.exp(la_h + la_h * ta)
            q = 1.0 - a * a
            mult = jnp.where(q > 0.0, q * lax.rsqrt(q), 0.0)
            first = 0 if d == 0 else n_ctx - 1
            if r0 <= first < r0 + tr:
                mult = jnp.where(row_id == first - r0, 1.0, mult)
            a_s[r0:r0 + tr, :] = a
            b_s[r0:r0 + tr, :] = (mult * xh) * (1.0 + ti)

    row8 = lax.broadcasted_iota(jnp.int32, (sub, tc), 0)

    def tile_maps(a, b, reverse):
        for dlt in (1, 2, 4):
            shift = sub - dlt if reverse else dlt
            valid = (row8 < sub - dlt) if reverse else (row8 >= dlt)
            a_sh = jnp.where(valid, pltpu.roll(a, shift, 0), 1.0)
            b_sh = jnp.where(valid, pltpu.roll(b, shift, 0), 0.0)
            b = b + a * b_sh
            a = a * a_sh
        return a, b

    def scan_tile(tile_f, tile_b, carry):
        hf, hb = carry
        rf = pl.ds(pl.multiple_of(tile_f * sub, sub), sub)
        rb = pl.ds(pl.multiple_of(tile_b * sub, sub), sub)
        a_f, b_f = tile_maps(af_s[rf, :], bf_s[rf, :], False)
        a_b, b_b = tile_maps(ab_s[rb, :], bb_s[rb, :], True)
        h_f = b_f + a_f * hf
        h_b = b_b + a_b * hb
        bf_s[rf, :] = h_f
        bb_s[rb, :] = h_b
        return h_f[sub - 1:sub, :], h_b[0:1, :]

    zero = jnp.zeros((1, tc), F32)
    ctx_tiles = n_ctx // sub
    all_tiles = rows // sub
    carry = lax.fori_loop(0, ctx_tiles, lambda i, cr: scan_tile(i, ctx_tiles - 1 - i, cr), (zero, zero),
                          unroll=2)
    lax.fori_loop(ctx_tiles, all_tiles, lambda i, cr: scan_tile(i, all_tiles - 1 + ctx_tiles - i, cr), carry,
                  unroll=2)

    k0 = 0.7978845608028654
    for c in range(rows // tr):
        sl = slice(c * tr, (c + 1) * tr)
        y = y_ref[sl, :].astype(F32)
        gelu = 0.5 * y * (1.0 + jnp.tanh(k0 * (y + 0.044715 * (y * y * y))))
        o_ref[sl, :] = ((bf_s[sl, :] + bb_s[sl, :]) * gelu).astype(o_ref.dtype)


def _lru_call(p, conv_w, conv_b, w_a, b_a, w_i, b_i, lam, layer, n_ctx):
    b, r, _ = p.shape
    tc = 256
    nb = tc // LRU_BS
    chan = lambda bi, j: (layer, 0, j)
    return pl.pallas_call(
        functools.partial(_lru_kernel, n_ctx=n_ctx, tc=tc, tr=n_ctx),
        grid=(b, LRU_W // tc),
        in_specs=[
            pl.BlockSpec((None, r, tc), lambda bi, j: (bi, 0, COL_LX // tc + j)),
            pl.BlockSpec((None, r, tc), lambda bi, j: (bi, 0, COL_LY // tc + j)),
            pl.BlockSpec((None, CONV_W, tc), chan),
            pl.BlockSpec((None, 1, tc), chan),
            pl.BlockSpec((None, 2, nb, LRU_BS, LRU_BS), lambda bi, j: (layer, 0, j, 0, 0)),
            pl.BlockSpec((None, 2, tc), chan),
            pl.BlockSpec((None, 2, nb, LRU_BS, LRU_BS), lambda bi, j: (layer, 0, j, 0, 0)),
            pl.BlockSpec((None, 2, tc), chan),
            pl.BlockSpec((None, 2, tc), chan),
        ],
        out_specs=pl.BlockSpec((None, r, tc), lambda bi, j: (bi, 0, j)),
        out_shape=jax.ShapeDtypeStruct((b, r, LRU_W), BF16),
        scratch_shapes=[pltpu.VMEM((r + 3 * LRU_PAD, tc), F32)] + [pltpu.VMEM((r, tc), F32)] * 4,
        compiler_params=_cparams(2),
        name="rglru",
    )(p, p, conv_w, conv_b, w_a, b_a, w_i, b_i, lam)


def _merge_kernel(xa_ref, xb_ref, xc_ref, wa_ref, wb_ref, wc_ref, ga_ref, gb_ref, gc_ref, bm_ref, o_ref):
    acc = None
    for n, (x_ref, w_ref, g_ref) in enumerate(((xa_ref, wa_ref, ga_ref), (xb_ref, wb_ref, gb_ref),
                                               (xc_ref, wc_ref, gc_ref))):
        proj = jnp.dot(x_ref[...], w_ref[...], preferred_element_type=F32)
        term = _sigmoid(g_ref[...].astype(F32) + bm_ref[n:n + 1, :]) * proj
        acc = term if acc is None else acc + term
    o_ref[...] = acc.astype(o_ref.dtype)


def _merge_call(o_gla, o_att, o_lru, wbr, p, b_merge, layer, tm, tn):
    b, r, d = o_gla.shape
    ncol = d // tn
    x_spec = pl.BlockSpec((None, tm, d), lambda j, bi, i: (bi, i, 0))

    def w_spec(n):
        return pl.BlockSpec((None, None, d, tn), lambda j, bi, i: (layer, n, 0, j))

    def g_spec(n):
        return pl.BlockSpec((None, tm, tn), lambda j, bi, i: (bi, i, COL_GT // tn + n * ncol + j))

    return pl.pallas_call(
        _merge_kernel,
        grid=(ncol, b, r // tm),
        in_specs=[x_spec, x_spec, x_spec, w_spec(0), w_spec(1), w_spec(2), g_spec(0), g_spec(1), g_spec(2),
                  pl.BlockSpec((None, N_BRANCH, tn), lambda j, bi, i: (layer, 0, j))],
        out_specs=pl.BlockSpec((None, tm, tn), lambda j, bi, i: (bi, i, j)),
        out_shape=jax.ShapeDtypeStruct((b, r, d), BF16),
        compiler_params=_cparams(3),
        name="merge",
    )(o_gla, o_att, o_lru, wbr, wbr, wbr, p, p, p, b_merge)


def _final_norm_kernel(h_ref, g_ref, o_ref):
    x = h_ref[...]
    var = jnp.mean(x * x, axis=-1, keepdims=True)
    o_ref[...] = x * lax.rsqrt(var + NORM_EPS) * g_ref[...]


def _final_norm_call(h, g, n_ctx):
    b, r, d = h.shape
    tr = n_ctx
    return pl.pallas_call(
        _final_norm_kernel,
        grid=(b, (r - n_ctx) // tr),
        in_specs=[
            pl.BlockSpec((None, tr, d), lambda bi, i: (bi, i + 1, 0)),
            pl.BlockSpec((1, d), lambda bi, i: (0, 0)),
        ],
        out_specs=pl.BlockSpec((None, tr, d), lambda bi, i: (bi, i, 0)),
        out_shape=jax.ShapeDtypeStruct((b, r - n_ctx, d), F32),
        compiler_params=_cparams(2),
        name="final_norm",
    )(h, g)


def _rope_tables(n_tokens):
    rows = n_tokens // GRID_W
    row = jnp.repeat(jnp.arange(rows, dtype=F32), GRID_W)
    col = jnp.tile(jnp.arange(GRID_W, dtype=F32), rows)
    inv = ROPE_THETA ** (-jnp.arange(ROPE_AXIS_PAIRS, dtype=F32) / ROPE_AXIS_PAIRS)
    ang = jnp.concatenate([row[:, None] * inv, col[:, None] * inv], axis=-1)
    cos, sin = jnp.cos(ang), jnp.sin(ang)
    cos_t = jnp.repeat(cos, 2, axis=-1)
    sin_t = jnp.stack([-sin, sin], axis=-1).reshape(n_tokens, ATT_HD)
    return cos_t, sin_t


def kernel(x, c, ctx, c_ctx, w_mod, b_mod, norm_mix_g, norm_ffn_g, w_in, gla_w_decay, gla_b_decay, gla_norm_g, q_norm_g, k_norm_g, conv_w, conv_b, lru_w_a, lru_b_a, lru_w_i, lru_b_i, lru_lambda, b_merge, w_branch, w_out, w_ffn_in, w_ffn_out, final_norm_g):
    depth = w_in.shape[0]
    batch, t_lat, d = x.shape
    n_ctx = ctx.shape[1]
    rows = n_ctx + t_lat
    ctx_row = batch
    assert batch < MOD_ROWS and rows % n_ctx == 0 and n_ctx % GLA_CHUNK == 0
    tm = 768 if rows % 768 == 0 else n_ctx
    tm_ffn_out = 384 if rows % 384 == 0 else n_ctx
    tn_in, tn_merge, tn_out, tn_ffn_in, tn_ffn_out = 1024, 512, d, 512, 1024

    w_main = _win_prep_call(w_in)
    w_dec = jnp.pad(w_in[:, :, DEC_ORIG:DEC_ORIG + 2 * GLA_RANK],
                    ((0, 0), (0, 0), (0, DEC_PAD - 2 * GLA_RANK))).astype(BF16)
    w_branch_b = w_branch.astype(BF16)
    w_out_b = w_out.astype(BF16)
    w_ffn_in_b = w_ffn_in.astype(BF16)
    w_ffn_out_b = w_ffn_out.astype(BF16)
    wd_pad = jnp.zeros((depth, 2, DEC_PAD, GLA_DK), F32)
    wd_pad = wd_pad.at[:, 0, :GLA_RANK].set(gla_w_decay[:, 0])
    wd_pad = wd_pad.at[:, 1, GLA_RANK:2 * GLA_RANK].set(gla_w_decay[:, 1])
    bd = gla_b_decay.reshape(depth, 2, 1, GLA_DK)
    cos_t, sin_t = _rope_tables(t_lat)

    c16 = jnp.zeros((MOD_ROWS, d), F32).at[:batch].set(c).at[ctx_row].set(c_ctx)
    mods4 = _mods_call(c16, w_mod, b_mod).reshape(depth, MOD_ROWS, 1, N_MOD * d)

    g_mix = norm_mix_g.reshape(depth, 1, d)
    g_ffn = norm_ffn_g.reshape(depth, 1, d)
    g_gla = gla_norm_g.reshape(depth, 1, GLA_HDV)
    g_q = q_norm_g.reshape(depth, 1, ATT_HD)
    g_k = k_norm_g.reshape(depth, 1, ATT_HD)
    cb3 = conv_b.reshape(depth, 1, LRU_W)

    h = jnp.concatenate([ctx, x], axis=1)
    for l in range(depth):
        u = _normmod_call(h, g_mix, mods4, l, 0, 1, n_ctx, ctx_row)
        p = _linear_call(u, w_main, l, BF16, tm, tn_in, "in_proj")
        dec = _linear_call(u, w_dec, l, F32, tm, DEC_PAD, "dec_proj")
        o_att = _attn_call(p, g_q, g_k, cos_t, sin_t, l, n_ctx)
        o_back = _gla_call(p, dec, wd_pad, bd, l, 1, n_ctx)
        o_gla = _gla_call(p, dec, wd_pad, bd, l, 0, n_ctx, o_back, g_gla)
        o_lru = _lru_call(p, conv_w, cb3, lru_w_a, lru_b_a, lru_w_i, lru_b_i, lru_lambda, l, n_ctx)
        merged = _merge_call(o_gla, o_att, o_lru, w_branch_b, p, b_merge, l, tm, tn_merge)
        h = _linear_res_call(merged, w_out_b, l, h, mods4, 2, n_ctx, ctx_row, tm, tn_out, "out_proj")
        u2 = _normmod_call(h, g_ffn, mods4, l, 3, 4, n_ctx, ctx_row)
        a = _swiglu_call(u2, w_ffn_in_b, l, tm, tn_ffn_in)
        h = _linear_res_call(a, w_ffn_out_b, l, h, mods4, 5, n_ctx, ctx_row, tm_ffn_out, tn_ffn_out, "ffn_out")
    return _final_norm_call(h, final_norm_g.reshape(1, d), n_ctx)
```

```python
import functools

import jax
import jax.numpy as jnp
import numpy as np
from jax import lax
from jax.experimental import pallas as pl
from jax.experimental.pallas import tpu as pltpu

F32 = jnp.float32
BF16 = jnp.bfloat16

D_MODEL = 2048
GRID_W = 64
NORM_EPS = 1e-6
N_MOD = 6

GLA_HEADS = 4
GLA_DK = D_MODEL // 2
GLA_DV = D_MODEL
GLA_HDK = GLA_DK // GLA_HEADS
GLA_HDV = GLA_DV // GLA_HEADS
GLA_RANK = 16
GLA_GATE_NORM = 16.0
GLA_CHUNK = 64

ATT_HEADS = 16
ATT_KV_HEADS = 4
ATT_HD = 128
ATT_GROUP = ATT_HEADS // ATT_KV_HEADS
ATT_Q = ATT_HEADS * ATT_HD
ATT_KV = ATT_KV_HEADS * ATT_HD
ROPE_THETA = 10000.0
ROPE_AXIS_PAIRS = ATT_HD // 4

LRU_W = D_MODEL
LRU_BLOCKS = 16
LRU_BS = LRU_W // LRU_BLOCKS
LRU_C = 8.0
CONV_W = 4
CONV_LEFT = 2

N_BRANCH = 3
FFN_HIDDEN = -(-8 * D_MODEL // (3 * 256)) * 256

COL_GQ = 0
COL_GK = COL_GQ + GLA_DK
COL_GV = COL_GK + GLA_DK
COL_GR = COL_GV + GLA_DV
COL_AQ = COL_GR + GLA_DV
COL_AK = COL_AQ + ATT_Q
COL_AV = COL_AK + ATT_KV
COL_LX = COL_AV + ATT_KV
COL_LY = COL_LX + LRU_W
COL_GT = COL_LY + LRU_W
N_MAIN = COL_GT + N_BRANCH * D_MODEL
DEC_ORIG = 2 * GLA_DK + 2 * GLA_DV
DEC_PAD = 128

V7X_VMEM_LIMIT_BYTES = 56 * 1024 * 1024
MOD_ROWS = 16


def _cparams(n_axes, vmem=V7X_VMEM_LIMIT_BYTES):
    return pltpu.CompilerParams(dimension_semantics=("arbitrary",) * n_axes, vmem_limit_bytes=vmem)


def _sigmoid(x):
    return 0.5 * jnp.tanh(0.5 * x) + 0.5


def _mods_kernel(c_ref, w_ref, b_ref, o_ref):
    c = c_ref[...]
    sc = (c * _sigmoid(c)).astype(BF16)
    o_ref[...] = jnp.dot(sc, w_ref[...].astype(BF16), preferred_element_type=F32) + b_ref[...]


def _mods_call(c16, w_mod, b_mod):
    depth, d, n = w_mod.shape
    tn = 1024
    return pl.pallas_call(
        _mods_kernel,
        grid=(depth, n // tn),
        in_specs=[
            pl.BlockSpec((MOD_ROWS, d), lambda l, j: (0, 0)),
            pl.BlockSpec((None, d, tn), lambda l, j: (l, 0, j)),
            pl.BlockSpec((None, 1, tn), lambda l, j: (l, 0, j)),
        ],
        out_specs=pl.BlockSpec((None, MOD_ROWS, tn), lambda l, j: (l, 0, j)),
        out_shape=jax.ShapeDtypeStruct((depth, MOD_ROWS, n), F32),
        compiler_params=_cparams(2),
        name="mods",
    )(c16, w_mod, b_mod.reshape(depth, 1, n))


def _win_prep_kernel(a_ref, b_ref, o_ref, *, n_plain, shift):
    j = pl.program_id(1)

    @pl.when(j < n_plain)
    def _before_decay_columns():
        o_ref[...] = a_ref[...].T.astype(o_ref.dtype)

    @pl.when(j >= n_plain)
    def _after_decay_columns():
        cols = jnp.concatenate([a_ref[shift:, :], b_ref[...]], axis=0)
        o_ref[...] = cols.T.astype(o_ref.dtype)


def _win_prep_call(w_in_t):
    depth, _, d = w_in_t.shape
    tn, shift = 512, 2 * GLA_RANK
    assert DEC_ORIG % tn == 0 and N_MAIN % tn == 0 and tn % shift == 0
    return pl.pallas_call(
        functools.partial(_win_prep_kernel, n_plain=DEC_ORIG // tn, shift=shift),
        grid=(depth, N_MAIN // tn),
        in_specs=[
            pl.BlockSpec((None, tn, d), lambda l, j: (l, j, 0)),
            pl.BlockSpec((None, shift, d), lambda l, j: (l, (j + 1) * (tn // shift), 0)),
        ],
        out_specs=pl.BlockSpec((None, d, tn), lambda l, j: (l, 0, j)),
        out_shape=jax.ShapeDtypeStruct((depth, d, N_MAIN), BF16),
        compiler_params=_cparams(2),
        name="w_in_prep",
    )(w_in_t, w_in_t)


def _normmod_kernel(h_ref, g_ref, sh_ref, sc_ref, o_ref):
    x = h_ref[...]
    var = jnp.mean(x * x, axis=-1, keepdims=True)
    y = x * lax.rsqrt(var + NORM_EPS) * g_ref[...]
    o_ref[...] = (y * (1.0 + sc_ref[...]) + sh_ref[...]).astype(o_ref.dtype)


def _normmod_call(h, g, mods4, layer, k_shift, k_scale, n_ctx, ctx_row):
    b, r, d = h.shape
    tr = n_ctx

    def mod_map(k):
        return lambda bi, i: (layer, jnp.where(i == 0, ctx_row, bi), 0, k)

    return pl.pallas_call(
        _normmod_kernel,
        grid=(b, r // tr),
        in_specs=[
            pl.BlockSpec((None, tr, d), lambda bi, i: (bi, i, 0)),
            pl.BlockSpec((None, 1, d), lambda bi, i: (layer, 0, 0)),
            pl.BlockSpec((None, None, 1, d), mod_map(k_shift)),
            pl.BlockSpec((None, None, 1, d), mod_map(k_scale)),
        ],
        out_specs=pl.BlockSpec((None, tr, d), lambda bi, i: (bi, i, 0)),
        out_shape=jax.ShapeDtypeStruct((b, r, d), BF16),
        compiler_params=_cparams(2),
        name="normmod",
    )(h, g, mods4, mods4)


def _weight_tile(w_ref, w_s):
    if w_s is None:
        return w_ref[...]

    @pl.when((pl.program_id(1) == 0) & (pl.program_id(2) == 0))
    def _cast_weight_tile():
        w_s[...] = w_ref[...].astype(BF16)

    return w_s[...]


def _weight_scratch(w, k, tn, count=1):
    return [pltpu.VMEM((k, tn), BF16)] * count if w.dtype != BF16 else []


def _linear_kernel(x_ref, w_ref, o_ref, w_s=None):
    o_ref[...] = jnp.dot(x_ref[...], _weight_tile(w_ref, w_s), preferred_element_type=F32).astype(o_ref.dtype)


def _linear_call(x, w3, layer, out_dtype, tm, tn, name, n=None, col0=0):
    b, r, k = x.shape
    n = w3.shape[-1] if n is None else n
    return pl.pallas_call(
        _linear_kernel,
        grid=(n // tn, b, r // tm),
        in_specs=[
            pl.BlockSpec((None, tm, k), lambda j, bi, i: (bi, i, 0)),
            pl.BlockSpec((None, k, tn), lambda j, bi, i: (layer, 0, col0 + j)),
        ],
        out_specs=pl.BlockSpec((None, tm, tn), lambda j, bi, i: (bi, i, j)),
        out_shape=jax.ShapeDtypeStruct((b, r, n), out_dtype),
        scratch_shapes=_weight_scratch(w3, k, tn),
        compiler_params=_cparams(3),
        name=name,
    )(x, w3)


def _decay_proj_kernel(x_ref, wt_ref, o_ref):
    o_ref[...] = lax.dot_general(x_ref[...], wt_ref[...].astype(BF16), (((1,), (1,)), ((), ())),
                                 preferred_element_type=F32)


def _decay_proj_call(x, w_in_t, layer, tm):
    b, r, k = x.shape
    return pl.pallas_call(
        _decay_proj_kernel,
        grid=(b, r // tm),
        in_specs=[
            pl.BlockSpec((None, tm, k), lambda bi, i: (bi, i, 0)),
            pl.BlockSpec((None, DEC_PAD, k), lambda bi, i: (layer, DEC_ORIG // DEC_PAD, 0)),
        ],
        out_specs=pl.BlockSpec((None, tm, DEC_PAD), lambda bi, i: (bi, i, 0)),
        out_shape=jax.ShapeDtypeStruct((b, r, DEC_PAD), F32),
        compiler_params=_cparams(2),
        name="dec_proj",
    )(x, w_in_t)


def _linear_res_kernel(*refs, tm, n_ctx, with_norm, cast_w):
    x_ref, w_ref, h_ref, gc_ref, gl_ref = refs[:5]
    rest = list(refs[5:])
    if with_norm:
        ng_ref, shc_ref, shl_ref, scc_ref, scl_ref = rest[:5]
        rest = rest[5:]
    o_ref = rest.pop(0)
    u_ref = rest.pop(0) if with_norm else None
    w_s = rest.pop(0) if cast_w else None
    i = pl.program_id(2)
    y = jnp.dot(x_ref[...], _weight_tile(w_ref, w_s), preferred_element_type=F32)
    is_ctx = (i * tm + lax.broadcasted_iota(jnp.int32, (tm, 1), 0)) < n_ctx
    h_new = h_ref[...] + jnp.where(is_ctx, gc_ref[...], gl_ref[...]) * y
    o_ref[...] = h_new
    if with_norm:
        var = jnp.mean(h_new * h_new, axis=-1, keepdims=True)
        yn = h_new * lax.rsqrt(var + NORM_EPS) * ng_ref[...]
        shift = jnp.where(is_ctx, shc_ref[...], shl_ref[...])
        scale = jnp.where(is_ctx, scc_ref[...], scl_ref[...])
        u_ref[...] = (yn * (1.0 + scale) + shift).astype(u_ref.dtype)


def _linear_res_call(x, w3, layer, h, mods4, k_gate, n_ctx, ctx_row, tm, tn, name, norm=None):
    b, r, k = x.shape
    n = w3.shape[-1]
    ngate = n // tn
    cast_w = w3.dtype != BF16

    def mod_spec(lyr, row_of, kk):
        return pl.BlockSpec((None, None, 1, tn), lambda j, bi, i: (lyr, row_of(bi), 0, kk * ngate + j))

    ctx_of = lambda bi: ctx_row
    lat_of = lambda bi: bi
    in_specs = [
        pl.BlockSpec((None, tm, k), lambda j, bi, i: (bi, i, 0)),
        pl.BlockSpec((None, k, tn), lambda j, bi, i: (layer, 0, j)),
        pl.BlockSpec((None, tm, tn), lambda j, bi, i: (bi, i, j)),
        mod_spec(layer, ctx_of, k_gate),
        mod_spec(layer, lat_of, k_gate),
    ]
    args = [x, w3, h, mods4, mods4]
    out_specs = [pl.BlockSpec((None, tm, tn), lambda j, bi, i: (bi, i, j))]
    out_shape = [jax.ShapeDtypeStruct((b, r, n), F32)]
    if norm is not None:
        gains, n_layer, k_shift, k_scale = norm
        assert tn == n
        in_specs += [
            pl.BlockSpec((None, 1, n), lambda j, bi, i: (n_layer, 0, 0)),
            mod_spec(n_layer, ctx_of, k_shift), mod_spec(n_layer, lat_of, k_shift),
            mod_spec(n_layer, ctx_of, k_scale), mod_spec(n_layer, lat_of, k_scale),
        ]
        args += [gains, mods4, mods4, mods4, mods4]
        out_specs.append(pl.BlockSpec((None, tm, tn), lambda j, bi, i: (bi, i, j)))
        out_shape.append(jax.ShapeDtypeStruct((b, r, n), BF16))
    res = pl.pallas_call(
        functools.partial(_linear_res_kernel, tm=tm, n_ctx=n_ctx, with_norm=norm is not None, cast_w=cast_w),
        grid=(n // tn, b, r // tm),
        in_specs=in_specs,
        out_specs=out_specs,
        out_shape=out_shape,
        scratch_shapes=_weight_scratch(w3, k, tn),
        compiler_params=_cparams(3),
        name=name,
    )(*args)
    return res if norm is not None else res[0]


def _swiglu_kernel(x_ref, wg_ref, wu_ref, o_ref, wg_s=None, wu_s=None):
    x = x_ref[...]
    g = jnp.dot(x, _weight_tile(wg_ref, wg_s), preferred_element_type=F32)
    up = jnp.dot(x, _weight_tile(wu_ref, wu_s), preferred_element_type=F32)
    o_ref[...] = (g * _sigmoid(g) * up).astype(o_ref.dtype)


def _swiglu_call(x, w3, layer, tm, tn):
    b, r, k = x.shape
    hid = w3.shape[-1] // 2
    nb = hid // tn
    return pl.pallas_call(
        _swiglu_kernel,
        grid=(nb, b, r // tm),
        in_specs=[
            pl.BlockSpec((None, tm, k), lambda j, bi, i: (bi, i, 0)),
            pl.BlockSpec((None, k, tn), lambda j, bi, i: (layer, 0, j)),
            pl.BlockSpec((None, k, tn), lambda j, bi, i: (layer, 0, nb + j)),
        ],
        out_specs=pl.BlockSpec((None, tm, tn), lambda j, bi, i: (bi, i, j)),
        out_shape=jax.ShapeDtypeStruct((b, r, hid), BF16),
        scratch_shapes=_weight_scratch(w3, k, tn, 2),
        compiler_params=_cparams(3),
        name="ffn_in",
    )(x, w3, w3)


def _norm_rope(x, gain, cs, sn):
    var = jnp.mean(x * x, axis=-1, keepdims=True)
    y = x * lax.rsqrt(var + NORM_EPS) * gain
    if cs is None:
        return y
    lane = lax.broadcasted_iota(jnp.int32, y.shape, 1)
    nxt = pltpu.roll(y, ATT_HD - 1, 1)
    prv = pltpu.roll(y, 1, 1)
    partner = jnp.where(lane % 2 == 0, nxt, prv)
    return y * cs + partner * sn


def _attn_kernel(q_ref, k_ref, v_ref, qg_ref, kg_ref, cos_ref, sin_ref, o_ref, kt_s, *, n_ctx, tq):
    qi = pl.program_id(2)
    rows = k_ref.shape[0]
    scale = ATT_HD ** -0.5

    @pl.when(qi == 0)
    def _prepare_k():
        kg = kg_ref[...]
        for c in range(rows // tq):
            kc = k_ref[c * tq:(c + 1) * tq, :].astype(F32)
            if c == 0:
                kn = _norm_rope(kc, kg, None, None)
            else:
                kn = _norm_rope(kc, kg, cos_ref[(c - 1) * tq:c * tq, :], sin_ref[(c - 1) * tq:c * tq, :])
            kt_s[:, c * tq:(c + 1) * tq] = kn.T.astype(BF16)

    def attend(cs, sn, n_keys):
        qg = qg_ref[...]
        for g in range(ATT_GROUP):
            q = _norm_rope(q_ref[:, g * ATT_HD:(g + 1) * ATT_HD].astype(F32), qg, cs, sn) * scale
            s = jnp.dot(q.astype(BF16), kt_s[:, :n_keys], preferred_element_type=F32)
            m = jnp.max(s, axis=-1, keepdims=True)
            p = jnp.exp(s - m)
            den = jnp.sum(p, axis=-1, keepdims=True)
            o = jnp.dot(p.astype(BF16), v_ref[:n_keys, :], preferred_element_type=F32) / den
            o_ref[:, g * ATT_HD:(g + 1) * ATT_HD] = o.astype(o_ref.dtype)

    @pl.when(qi == 0)
    def _context_queries():
        attend(None, None, n_ctx)

    @pl.when(qi > 0)
    def _latent_queries():
        off = pl.multiple_of((qi - 1) * tq, tq)
        attend(cos_ref[pl.ds(off, tq), :], sin_ref[pl.ds(off, tq), :], rows)


def _attn_call(p, q_g, k_g, cos_t, sin_t, layer, n_ctx):
    b, r, _ = p.shape
    tq = n_ctx
    t = r - n_ctx
    qw = ATT_GROUP * ATT_HD
    return pl.pallas_call(
        functools.partial(_attn_kernel, n_ctx=n_ctx, tq=tq),
        grid=(b, ATT_KV_HEADS, r // tq),
        in_specs=[
            pl.BlockSpec((None, tq, qw), lambda bi, kv, qi: (bi, qi, COL_AQ // qw + kv)),
            pl.BlockSpec((None, r, ATT_HD), lambda bi, kv, qi: (bi, 0, COL_AK // ATT_HD + kv)),
            pl.BlockSpec((None, r, ATT_HD), lambda bi, kv, qi: (bi, 0, COL_AV // ATT_HD + kv)),
            pl.BlockSpec((None, 1, ATT_HD), lambda bi, kv, qi: (layer, 0, 0)),
            pl.BlockSpec((None, 1, ATT_HD), lambda bi, kv, qi: (layer, 0, 0)),
            pl.BlockSpec((t, ATT_HD), lambda bi, kv, qi: (0, 0)),
            pl.BlockSpec((t, ATT_HD), lambda bi, kv, qi: (0, 0)),
        ],
        out_specs=pl.BlockSpec((None, tq, qw), lambda bi, kv, qi: (bi, qi, kv)),
        out_shape=jax.ShapeDtypeStruct((b, r, ATT_Q), BF16),
        scratch_shapes=[pltpu.VMEM((ATT_HD, r), BF16)],
        compiler_params=_cparams(3),
        name="gqa",
    )(p, p, p, q_g, k_g, cos_t, sin_t)


def _gla_direction(q_ref, k_ref, v_ref, dec_ref, wd_ref, bd_ref, st_s, reverse, n_chunks):
    c_len = GLA_CHUNK
    tb = n_chunks * c_len

    ri = lax.broadcasted_iota(jnp.int32, (tb, tb), 0)
    ci = lax.broadcasted_iota(jnp.int32, (tb, tb), 1)
    log2_c = c_len.bit_length() - 1
    same = jnp.right_shift(ri, log2_c) == jnp.right_shift(ci, log2_c)
    tri = same & ((ci >= ri) if reverse else (ri >= ci))
    tri_b = jnp.where(tri, 1.0, 0.0).astype(BF16)
    nt = (((1,), (1,)), ((), ()))
    tn = (((0,), (0,)), ((), ()))

    x = jnp.dot(dec_ref[...].astype(BF16), wd_ref[...].astype(BF16), preferred_element_type=F32) + bd_ref[...]
    g = (jnp.minimum(x, 0.0) - jnp.log1p(jnp.exp(-jnp.abs(x)))) * (1.0 / GLA_GATE_NORM)
    g_hi = g.astype(BF16)
    g_lo = (g - g_hi.astype(F32)).astype(BF16)
    bcum = jnp.dot(tri_b, g_hi, preferred_element_type=F32) + jnp.dot(tri_b, g_lo, preferred_element_type=F32)
    q = q_ref[...].astype(F32) * (GLA_HDK ** -0.5)
    k = k_ref[...].astype(F32)
    v = v_ref[...]
    q_dec = (q * jnp.exp(bcum)).astype(BF16)
    k_inv = (k * jnp.exp(-bcum)).astype(BF16)
    att = lax.dot_general(q_dec, k_inv, nt, preferred_element_type=F32)
    att = jnp.where(tri, att, 0.0).astype(BF16)
    o_intra = jnp.dot(att, v, preferred_element_type=F32)
    b_last, k_dec = [], []
    for c in range(n_chunks):
        sl = slice(c * c_len, (c + 1) * c_len)
        end = c * c_len if reverse else (c + 1) * c_len - 1
        b_last.append(bcum[end:end + 1, :])
        k_dec.append((k[sl, :] * jnp.exp(b_last[c] - bcum[sl, :])).astype(BF16))

    outs = [None] * n_chunks
    chunk_order = range(n_chunks - 1, -1, -1) if reverse else range(n_chunks)
    for c in chunk_order:
        sl = slice(c * c_len, (c + 1) * c_len)
        st = st_s[...]
        outs[c] = o_intra[sl, :] + lax.dot_general(q_dec[sl, :], st.astype(BF16), nt, preferred_element_type=F32)
        st_s[...] = (st * jnp.exp(b_last[c])
                     + lax.dot_general(v[sl, :], k_dec[c], tn, preferred_element_type=F32))
    return jnp.concatenate(outs, axis=0)


def _gla_kernel(qf_ref, kf_ref, vf_ref, decf_ref, qb_ref, kb_ref, vb_ref, decb_ref,
                wdf_ref, bdf_ref, wdb_ref, bdb_ref, r_ref, ng_ref, o_ref,
                stf_s, stb_s, of_s, ob_s, *, n_chunks, nblk):
    t = pl.program_id(2)
    tb = n_chunks * GLA_CHUNK

    @pl.when(t == 0)
    def _zero_states():
        stf_s[...] = jnp.zeros_like(stf_s)
        stb_s[...] = jnp.zeros_like(stb_s)

    o_f = _gla_direction(qf_ref, kf_ref, vf_ref, decf_ref, wdf_ref, bdf_ref, stf_s, False, n_chunks)
    o_b = _gla_direction(qb_ref, kb_ref, vb_ref, decb_ref, wdb_ref, bdb_ref, stb_s, True, n_chunks)
    blk_b = jnp.where(t == 0, 0, nblk - t)
    of_s[pl.ds(pl.multiple_of(t * tb, tb), tb), :] = o_f
    ob_s[pl.ds(pl.multiple_of(blk_b * tb, tb), tb), :] = o_b

    @pl.when(t == nblk - 1)
    def _combine_norm_gate():
        for c in range(nblk):
            sl = slice(c * tb, (c + 1) * tb)
            o = of_s[sl, :] + ob_s[sl, :]
            var = jnp.mean(o * o, axis=-1, keepdims=True)
            y = o * lax.rsqrt(var + NORM_EPS) * ng_ref[...]
            r = r_ref[sl, :].astype(F32)
            o_ref[sl, :] = (y * (r * _sigmoid(r))).astype(o_ref.dtype)


def _gla_call(p, dec, wd_pad, bd, norm_g, layer, n_ctx):
    b, r, _ = p.shape
    tb = n_ctx
    nblk = r // tb

    def blk(t, direction):
        if direction == 0:
            return t
        return jnp.where(t == 0, 0, nblk - t)

    def stream_specs(direction):
        return [
            pl.BlockSpec((None, tb, GLA_HDK), lambda bi, h, t: (bi, blk(t, direction), COL_GQ // GLA_HDK + h)),
            pl.BlockSpec((None, tb, GLA_HDK), lambda bi, h, t: (bi, blk(t, direction), COL_GK // GLA_HDK + h)),
            pl.BlockSpec((None, tb, GLA_HDV), lambda bi, h, t: (bi, blk(t, direction), COL_GV // GLA_HDV + h)),
            pl.BlockSpec((None, tb, DEC_PAD), lambda bi, h, t: (bi, blk(t, direction), 0)),
        ]

    def decay_specs(direction):
        return [
            pl.BlockSpec((None, None, DEC_PAD, GLA_HDK), lambda bi, h, t: (layer, direction, 0, h)),
            pl.BlockSpec((None, None, 1, GLA_HDK), lambda bi, h, t: (layer, direction, 0, h)),
        ]

    return pl.pallas_call(
        functools.partial(_gla_kernel, n_chunks=tb // GLA_CHUNK, nblk=nblk),
        grid=(b, GLA_HEADS, nblk),
        in_specs=stream_specs(0) + stream_specs(1) + decay_specs(0) + decay_specs(1) + [
            pl.BlockSpec((None, r, GLA_HDV), lambda bi, h, t: (bi, 0, COL_GR // GLA_HDV + h)),
            pl.BlockSpec((None, 1, GLA_HDV), lambda bi, h, t: (layer, 0, 0)),
        ],
        out_specs=pl.BlockSpec((None, r, GLA_HDV), lambda bi, h, t: (bi, 0, h)),
        out_shape=jax.ShapeDtypeStruct((b, r, GLA_DV), BF16),
        scratch_shapes=[pltpu.VMEM((GLA_HDV, GLA_HDK), F32)] * 2 + [pltpu.VMEM((r, GLA_HDV), F32)] * 2,
        compiler_params=_cparams(3),
        name="gla",
    )(p, p, p, dec, p, p, p, dec, wd_pad, bd, wd_pad, bd, p, norm_g)


LRU_PAD = 8


def _lru_kernel(x_ref, y_ref, cw_ref, cb_ref, wa_ref, ba_ref, wi_ref, bi_ref, lam_ref, o_ref,
                xp_s, af_s, bf_s, ab_s, bb_s, *, n_ctx, tc, tr):
    rows = x_ref.shape[0]
    n_lat = rows - n_ctx
    sub = 8
    ctx0 = LRU_PAD
    lat0 = 2 * LRU_PAD + n_ctx
    zpad = jnp.zeros((LRU_PAD, tc), F32)
    xp_s[0:LRU_PAD, :] = zpad
    xp_s[ctx0 + n_ctx:lat0, :] = zpad
    xp_s[lat0 + n_lat:lat0 + n_lat + LRU_PAD, :] = zpad
    xp_s[ctx0:ctx0 + n_ctx, :] = x_ref[0:n_ctx, :].astype(F32)
    xp_s[lat0:lat0 + n_lat, :] = x_ref[n_ctx:rows, :].astype(F32)

    cw = cw_ref[...]
    cb = cb_ref[...]
    lam = lam_ref[...]
    neg_sp = -(jnp.maximum(-lam, 0.0) + jnp.log1p(jnp.exp(-jnp.abs(lam))))
    log_a_half = (0.5 * LRU_C) * neg_sp
    row_id = lax.broadcasted_iota(jnp.int32, (tr, 1), 0)
    n_kb = tc // LRU_BS
    wa_h = [[(0.5 * wa_ref[d, kb]).astype(BF16) for kb in range(n_kb)] for d in range(2)]
    wi_h = [[(0.5 * wi_ref[d, kb]).astype(BF16) for kb in range(n_kb)] for d in range(2)]
    ba_h = 0.5 * ba_ref[...]
    bi_h = 0.5 * bi_ref[...]

    for c in range(rows // tr):
        r0 = c * tr
        p0 = (ctx0 if r0 < n_ctx else lat0 - n_ctx) + r0
        n_win = tr + 2 * sub
        win = xp_s[p0 - sub:p0 + tr + sub, :]
        xc = cb + win[sub:sub + tr, :] * cw[CONV_LEFT:CONV_LEFT + 1, :]
        for j in range(CONV_W):
            off = j - CONV_LEFT
            if off != 0:
                xc = xc + pltpu.roll(win, (-off) % n_win, 0)[sub:sub + tr, :] * cw[j:j + 1, :]
        xh = 0.5 * xc
        xb = [xc[:, kb * LRU_BS:(kb + 1) * LRU_BS].astype(BF16) for kb in range(n_kb)]
        for d, (a_s, b_s) in enumerate(((af_s, bf_s), (ab_s, bb_s))):
            za = jnp.concatenate([jnp.dot(xb[kb], wa_h[d][kb], preferred_element_type=F32)
                                  for kb in range(n_kb)], axis=1)
            zi = jnp.concatenate([jnp.dot(xb[kb], wi_h[d][kb], preferred_element_type=F32)
                                  for kb in range(n_kb)], axis=1)
            ta = jnp.tanh(za + ba_h[d:d + 1, :])
            ti = jnp.tanh(zi + bi_h[d:d + 1, :])
            la_h = log_a_half[d:d + 1, :]
            a = jnp.exp(la_h + la_h * ta)
            q = 1.0 - a * a
            mult = jnp.where(q > 0.0, q * lax.rsqrt(q), 0.0)
            first = 0 if d == 0 else n_ctx - 1
            if r0 <= first < r0 + tr:
                mult = jnp.where(row_id == first - r0, 1.0, mult)
            a_s[r0:r0 + tr, :] = a
            b_s[r0:r0 + tr, :] = (mult * xh) * (1.0 + ti)

    row8 = lax.broadcasted_iota(jnp.int32, (sub, tc), 0)

    def tile_maps(a, b, reverse):
        for dlt in (1, 2, 4):
            shift = sub - dlt if reverse else dlt
            valid = (row8 < sub - dlt) if reverse else (row8 >= dlt)
            a_sh = jnp.where(valid, pltpu.roll(a, shift, 0), 1.0)
            b_sh = jnp.where(valid, pltpu.roll(b, shift, 0), 0.0)
            b = b + a * b_sh
            a = a * a_sh
        return a, b

    def scan_tile(tile_f, tile_b, carry):
        hf, hb = carry
        rf = pl.ds(pl.multiple_of(tile_f * sub, sub), sub)
        rb = pl.ds(pl.multiple_of(tile_b * sub, sub), sub)
        a_f, b_f = tile_maps(af_s[rf, :], bf_s[rf, :], False)
        a_b, b_b = tile_maps(ab_s[rb, :], bb_s[rb, :], True)
        h_f = b_f + a_f * hf
        h_b = b_b + a_b * hb
        bf_s[rf, :] = h_f
        bb_s[rb, :] = h_b
        return h_f[sub - 1:sub, :], h_b[0:1, :]

    zero = jnp.zeros((1, tc), F32)
    ctx_tiles = n_ctx // sub
    all_tiles = rows // sub
    carry = lax.fori_loop(0, ctx_tiles, lambda i, cr: scan_tile(i, ctx_tiles - 1 - i, cr), (zero, zero),
                          unroll=2)
    lax.fori_loop(ctx_tiles, all_tiles, lambda i, cr: scan_tile(i, all_tiles - 1 + ctx_tiles - i, cr), carry,
                  unroll=2)

    k0 = 0.7978845608028654
    for c in range(rows // tr):
        sl = slice(c * tr, (c + 1) * tr)
        y = y_ref[sl, :].astype(F32)
        gelu = 0.5 * y * (1.0 + jnp.tanh(k0 * (y + 0.044715 * (y * y * y))))
        o_ref[sl, :] = ((bf_s[sl, :] + bb_s[sl, :]) * gelu).astype(o_ref.dtype)


def _lru_call(p, conv_w, conv_b, w_a, b_a, w_i, b_i, lam, layer, n_ctx):
    b, r, _ = p.shape
    tc = 256
    nb = tc // LRU_BS
    chan = lambda bi, j: (layer, 0, j)
    return pl.pallas_call(
        functools.partial(_lru_kernel, n_ctx=n_ctx, tc=tc, tr=n_ctx),
        grid=(b, LRU_W // tc),
        in_specs=[
            pl.BlockSpec((None, r, tc), lambda bi, j: (bi, 0, COL_LX // tc + j)),
            pl.BlockSpec((None, r, tc), lambda bi, j: (bi, 0, COL_LY // tc + j)),
            pl.BlockSpec((None, CONV_W, tc), chan),
            pl.BlockSpec((None, 1, tc), chan),
            pl.BlockSpec((None, 2, nb, LRU_BS, LRU_BS), lambda bi, j: (layer, 0, j, 0, 0)),
            pl.BlockSpec((None, 2, tc), chan),
            pl.BlockSpec((None, 2, nb, LRU_BS, LRU_BS), lambda bi, j: (layer, 0, j, 0, 0)),
            pl.BlockSpec((None, 2, tc), chan),
            pl.BlockSpec((None, 2, tc), chan),
        ],
        out_specs=pl.BlockSpec((None, r, tc), lambda bi, j: (bi, 0, j)),
        out_shape=jax.ShapeDtypeStruct((b, r, LRU_W), BF16),
        scratch_shapes=[pltpu.VMEM((r + 3 * LRU_PAD, tc), F32)] + [pltpu.VMEM((r, tc), F32)] * 4,
        compiler_params=_cparams(2),
        name="rglru",
    )(p, p, conv_w, conv_b, w_a, b_a, w_i, b_i, lam)


def _merge_kernel(xa_ref, xb_ref, xc_ref, wa_ref, wb_ref, wc_ref, ga_ref, gb_ref, gc_ref, bm_ref, o_ref,
                  wa_s=None, wb_s=None, wc_s=None):
    acc = None
    for n, (x_ref, w_ref, w_s, g_ref) in enumerate(((xa_ref, wa_ref, wa_s, ga_ref), (xb_ref, wb_ref, wb_s, gb_ref),
                                                    (xc_ref, wc_ref, wc_s, gc_ref))):
        proj = jnp.dot(x_ref[...], _weight_tile(w_ref, w_s), preferred_element_type=F32)
        term = _sigmoid(g_ref[...].astype(F32) + bm_ref[n:n + 1, :]) * proj
        acc = term if acc is None else acc + term
    o_ref[...] = acc.astype(o_ref.dtype)


def _merge_call(o_gla, o_att, o_lru, wbr, p, b_merge, layer, tm, tn):
    b, r, d = o_gla.shape
    ncol = d // tn
    x_spec = pl.BlockSpec((None, tm, d), lambda j, bi, i: (bi, i, 0))

    def w_spec(n):
        return pl.BlockSpec((None, None, d, tn), lambda j, bi, i: (layer, n, 0, j))

    def g_spec(n):
        return pl.BlockSpec((None, tm, tn), lambda j, bi, i: (bi, i, COL_GT // tn + n * ncol + j))

    return pl.pallas_call(
        _merge_kernel,
        grid=(ncol, b, r // tm),
        in_specs=[x_spec, x_spec, x_spec, w_spec(0), w_spec(1), w_spec(2), g_spec(0), g_spec(1), g_spec(2),
                  pl.BlockSpec((None, N_BRANCH, tn), lambda j, bi, i: (layer, 0, j))],
        out_specs=pl.BlockSpec((None, tm, tn), lambda j, bi, i: (bi, i, j)),
        out_shape=jax.ShapeDtypeStruct((b, r, d), BF16),
        scratch_shapes=_weight_scratch(wbr, d, tn, N_BRANCH),
        compiler_params=_cparams(3),
        name="merge",
    )(o_gla, o_att, o_lru, wbr, wbr, wbr, p, p, p, b_merge)


def _final_norm_kernel(h_ref, g_ref, o_ref):
    x = h_ref[...]
    var = jnp.mean(x * x, axis=-1, keepdims=True)
    o_ref[...] = x * lax.rsqrt(var + NORM_EPS) * g_ref[...]


def _final_norm_call(h, g, n_ctx):
    b, r, d = h.shape
    tr = n_ctx
    return pl.pallas_call(
        _final_norm_kernel,
        grid=(b, (r - n_ctx) // tr),
        in_specs=[
            pl.BlockSpec((None, tr, d), lambda bi, i: (bi, i + 1, 0)),
            pl.BlockSpec((1, d), lambda bi, i: (0, 0)),
        ],
        out_specs=pl.BlockSpec((None, tr, d), lambda bi, i: (bi, i, 0)),
        out_shape=jax.ShapeDtypeStruct((b, r - n_ctx, d), F32),
        compiler_params=_cparams(2),
        name="final_norm",
    )(h, g)


def _rope_tables(n_tokens):
    rows = n_tokens // GRID_W
    row = jnp.repeat(jnp.arange(rows, dtype=F32), GRID_W)
    col = jnp.tile(jnp.arange(GRID_W, dtype=F32), rows)
    inv = ROPE_THETA ** (-jnp.arange(ROPE_AXIS_PAIRS, dtype=F32) / ROPE_AXIS_PAIRS)
    ang = jnp.concatenate([row[:, None] * inv, col[:, None] * inv], axis=-1)
    cos, sin = jnp.cos(ang), jnp.sin(ang)
    cos_t = jnp.repeat(cos, 2, axis=-1)
    sin_t = jnp.stack([-sin, sin], axis=-1).reshape(n_tokens, ATT_HD)
    return cos_t, sin_t


def kernel(x, c, ctx, c_ctx, w_mod, b_mod, norm_mix_g, norm_ffn_g, w_in, gla_w_decay, gla_b_decay, gla_norm_g, q_norm_g, k_norm_g, conv_w, conv_b, lru_w_a, lru_b_a, lru_w_i, lru_b_i, lru_lambda, b_merge, w_branch, w_out, w_ffn_in, w_ffn_out, final_norm_g):
    depth = w_in.shape[0]
    batch, t_lat, d = x.shape
    n_ctx = ctx.shape[1]
    rows = n_ctx + t_lat
    ctx_row = batch
    assert batch < MOD_ROWS and rows % n_ctx == 0 and n_ctx % GLA_CHUNK == 0
    tm = 768 if rows % 768 == 0 else n_ctx
    tm_small = 384 if rows % 384 == 0 else n_ctx
    tn_in, tn_merge, tn_out, tn_ffn_in, tn_ffn_out = 1024, 512, d, 512, 512

    w_in_t = jnp.swapaxes(w_in, 1, 2)
    w_main = _win_prep_call(w_in_t)
    w_out_b = w_out.astype(BF16)
    wd_pad = jnp.zeros((depth, 2, DEC_PAD, GLA_DK), F32)
    wd_pad = wd_pad.at[:, 0, :GLA_RANK].set(gla_w_decay[:, 0])
    wd_pad = wd_pad.at[:, 1, GLA_RANK:2 * GLA_RANK].set(gla_w_decay[:, 1])
    bd = gla_b_decay.reshape(depth, 2, 1, GLA_DK)
    cos_t, sin_t = _rope_tables(t_lat)

    c16 = jnp.zeros((MOD_ROWS, d), F32).at[:batch].set(c).at[ctx_row].set(c_ctx)
    mods4 = _mods_call(c16, w_mod, b_mod).reshape(depth, MOD_ROWS, 1, N_MOD * d)

    g_mix = norm_mix_g.reshape(depth, 1, d)
    g_ffn = norm_ffn_g.reshape(depth, 1, d)
    g_gla = gla_norm_g.reshape(depth, 1, GLA_HDV)
    g_q = q_norm_g.reshape(depth, 1, ATT_HD)
    g_k = k_norm_g.reshape(depth, 1, ATT_HD)
    cb3 = conv_b.reshape(depth, 1, LRU_W)

    h = jnp.concatenate([ctx, x], axis=1)
    for l in range(depth):
        u = _normmod_call(h, g_mix, mods4, l, 0, 1, n_ctx, ctx_row)
        p = _linear_call(u, w_main, l, BF16, tm, tn_in, "in_proj")
        dec = _decay_proj_call(u, w_in_t, l, tm)
        o_att = _attn_call(p, g_q, g_k, cos_t, sin_t, l, n_ctx)
        o_gla = _gla_call(p, dec, wd_pad, bd, g_gla, l, n_ctx)
        o_lru = _lru_call(p, conv_w, cb3, lru_w_a, lru_b_a, lru_w_i, lru_b_i, lru_lambda, l, n_ctx)
        merged = _merge_call(o_gla, o_att, o_lru, w_branch, p, b_merge, l, tm_small, tn_merge)
        h, u2 = _linear_res_call(merged, w_out_b, l, h, mods4, 2, n_ctx, ctx_row, tm_small, tn_out, "out_proj",
                                 norm=(g_ffn, l, 3, 4))
        a = _swiglu_call(u2, w_ffn_in, l, tm, tn_ffn_in)
        h = _linear_res_call(a, w_ffn_out, l, h, mods4, 5, n_ctx, ctx_row, tm_small, tn_ffn_out, "ffn_out")
    return _final_norm_call(h, final_norm_g.reshape(1, d), n_ctx)
```

```python
import functools

import jax
import jax.numpy as jnp
import numpy as np
from jax import lax
from jax.experimental import pallas as pl
from jax.experimental.pallas import tpu as pltpu

F32 = jnp.float32
BF16 = jnp.bfloat16

D_MODEL = 2048
GRID_W = 64
NORM_EPS = 1e-6
N_MOD = 6

GLA_HEADS = 4
GLA_DK = D_MODEL // 2
GLA_DV = D_MODEL
GLA_HDK = GLA_DK // GLA_HEADS
GLA_HDV = GLA_DV // GLA_HEADS
GLA_RANK = 16
GLA_GATE_NORM = 16.0
GLA_CHUNK = 64

ATT_HEADS = 16
ATT_KV_HEADS = 4
ATT_HD = 128
ATT_GROUP = ATT_HEADS // ATT_KV_HEADS
ATT_Q = ATT_HEADS * ATT_HD
ATT_KV = ATT_KV_HEADS * ATT_HD
ROPE_THETA = 10000.0
ROPE_AXIS_PAIRS = ATT_HD // 4

LRU_W = D_MODEL
LRU_BLOCKS = 16
LRU_BS = LRU_W // LRU_BLOCKS
LRU_C = 8.0
CONV_W = 4
CONV_LEFT = 2

N_BRANCH = 3
FFN_HIDDEN = -(-8 * D_MODEL // (3 * 256)) * 256

COL_GQ = 0
COL_GK = COL_GQ + GLA_DK
COL_GV = COL_GK + GLA_DK
COL_GR = COL_GV + GLA_DV
COL_AQ = COL_GR + GLA_DV
COL_AK = COL_AQ + ATT_Q
COL_AV = COL_AK + ATT_KV
COL_LX = COL_AV + ATT_KV
COL_LY = COL_LX + LRU_W
COL_GT = COL_LY + LRU_W
N_MAIN = COL_GT + N_BRANCH * D_MODEL
DEC_ORIG = 2 * GLA_DK + 2 * GLA_DV
DEC_PAD = 128

V7X_VMEM_LIMIT_BYTES = 56 * 1024 * 1024
MOD_ROWS = 16


def _cparams(n_axes, vmem=V7X_VMEM_LIMIT_BYTES):
    return pltpu.CompilerParams(dimension_semantics=("arbitrary",) * n_axes, vmem_limit_bytes=vmem)


def _sigmoid(x):
    return 0.5 * jnp.tanh(0.5 * x) + 0.5


def _mods_kernel(c_ref, w_ref, b_ref, o_ref):
    c = c_ref[...]
    sc = (c * _sigmoid(c)).astype(BF16)
    o_ref[...] = jnp.dot(sc, w_ref[...].astype(BF16), preferred_element_type=F32) + b_ref[...]


def _mods_call(c16, w_mod, b_mod):
    depth, d, n = w_mod.shape
    tn = 1024
    return pl.pallas_call(
        _mods_kernel,
        grid=(depth, n // tn),
        in_specs=[
            pl.BlockSpec((MOD_ROWS, d), lambda l, j: (0, 0)),
            pl.BlockSpec((None, d, tn), lambda l, j: (l, 0, j)),
            pl.BlockSpec((None, 1, tn), lambda l, j: (l, 0, j)),
        ],
        out_specs=pl.BlockSpec((None, MOD_ROWS, tn), lambda l, j: (l, 0, j)),
        out_shape=jax.ShapeDtypeStruct((depth, MOD_ROWS, n), F32),
        compiler_params=_cparams(2),
        name="mods",
    )(c16, w_mod, b_mod.reshape(depth, 1, n))


def _win_prep_kernel(a_ref, b_ref, o_ref, *, n_plain, shift):
    j = pl.program_id(1)

    @pl.when(j < n_plain)
    def _before_decay_columns():
        o_ref[...] = a_ref[...].T.astype(o_ref.dtype)

    @pl.when(j >= n_plain)
    def _after_decay_columns():
        cols = jnp.concatenate([a_ref[shift:, :], b_ref[...]], axis=0)
        o_ref[...] = cols.T.astype(o_ref.dtype)


def _win_prep_call(w_in_t):
    depth, _, d = w_in_t.shape
    tn, shift = 512, 2 * GLA_RANK
    assert DEC_ORIG % tn == 0 and N_MAIN % tn == 0 and tn % shift == 0
    return pl.pallas_call(
        functools.partial(_win_prep_kernel, n_plain=DEC_ORIG // tn, shift=shift),
        grid=(depth, N_MAIN // tn),
        in_specs=[
            pl.BlockSpec((None, tn, d), lambda l, j: (l, j, 0)),
            pl.BlockSpec((None, shift, d), lambda l, j: (l, (j + 1) * (tn // shift), 0)),
        ],
        out_specs=pl.BlockSpec((None, d, tn), lambda l, j: (l, 0, j)),
        out_shape=jax.ShapeDtypeStruct((depth, d, N_MAIN), BF16),
        compiler_params=_cparams(2),
        name="w_in_prep",
    )(w_in_t, w_in_t)


def _normmod_kernel(h_ref, g_ref, sh_ref, sc_ref, o_ref):
    x = h_ref[...]
    var = jnp.mean(x * x, axis=-1, keepdims=True)
    y = x * lax.rsqrt(var + NORM_EPS) * g_ref[...]
    o_ref[...] = (y * (1.0 + sc_ref[...]) + sh_ref[...]).astype(o_ref.dtype)


def _normmod_call(h, g, mods4, layer, k_shift, k_scale, n_ctx, ctx_row):
    b, r, d = h.shape
    tr = n_ctx

    def mod_map(k):
        return lambda bi, i: (layer, jnp.where(i == 0, ctx_row, bi), 0, k)

    return pl.pallas_call(
        _normmod_kernel,
        grid=(b, r // tr),
        in_specs=[
            pl.BlockSpec((None, tr, d), lambda bi, i: (bi, i, 0)),
            pl.BlockSpec((None, 1, d), lambda bi, i: (layer, 0, 0)),
            pl.BlockSpec((None, None, 1, d), mod_map(k_shift)),
            pl.BlockSpec((None, None, 1, d), mod_map(k_scale)),
        ],
        out_specs=pl.BlockSpec((None, tr, d), lambda bi, i: (bi, i, 0)),
        out_shape=jax.ShapeDtypeStruct((b, r, d), BF16),
        compiler_params=_cparams(2),
        name="normmod",
    )(h, g, mods4, mods4)


def _linear_kernel(x_ref, w_ref, o_ref):
    o_ref[...] = jnp.dot(x_ref[...], w_ref[...], preferred_element_type=F32).astype(o_ref.dtype)


def _linear_call(x, w3, layer, out_dtype, tm, tn, name):
    b, r, k = x.shape
    n = w3.shape[-1]
    return pl.pallas_call(
        _linear_kernel,
        grid=(n // tn, b, r // tm),
        in_specs=[
            pl.BlockSpec((None, tm, k), lambda j, bi, i: (bi, i, 0)),
            pl.BlockSpec((None, k, tn), lambda j, bi, i: (layer, 0, j)),
        ],
        out_specs=pl.BlockSpec((None, tm, tn), lambda j, bi, i: (bi, i, j)),
        out_shape=jax.ShapeDtypeStruct((b, r, n), out_dtype),
        compiler_params=_cparams(3),
        name=name,
    )(x, w3)


def _decay_proj_kernel(x_ref, wt_ref, o_ref):
    o_ref[...] = lax.dot_general(x_ref[...], wt_ref[...].astype(BF16), (((1,), (1,)), ((), ())),
                                 preferred_element_type=F32)


def _decay_proj_call(x, w_in_t, layer, tm):
    b, r, k = x.shape
    return pl.pallas_call(
        _decay_proj_kernel,
        grid=(b, r // tm),
        in_specs=[
            pl.BlockSpec((None, tm, k), lambda bi, i: (bi, i, 0)),
            pl.BlockSpec((None, DEC_PAD, k), lambda bi, i: (layer, DEC_ORIG // DEC_PAD, 0)),
        ],
        out_specs=pl.BlockSpec((None, tm, DEC_PAD), lambda bi, i: (bi, i, 0)),
        out_shape=jax.ShapeDtypeStruct((b, r, DEC_PAD), F32),
        compiler_params=_cparams(2),
        name="dec_proj",
    )(x, w_in_t)


def _linear_res_kernel(*refs, tm, n_ctx, with_norm):
    x_ref, w_ref, h_ref, gc_ref, gl_ref = refs[:5]
    if with_norm:
        ng_ref, shc_ref, shl_ref, scc_ref, scl_ref, o_ref, u_ref = refs[5:]
    else:
        (o_ref,) = refs[5:]
    i = pl.program_id(2)
    y = jnp.dot(x_ref[...], w_ref[...], preferred_element_type=F32)
    is_ctx = (i * tm + lax.broadcasted_iota(jnp.int32, (tm, 1), 0)) < n_ctx
    h_new = h_ref[...] + jnp.where(is_ctx, gc_ref[...], gl_ref[...]) * y
    o_ref[...] = h_new
    if with_norm:
        var = jnp.mean(h_new * h_new, axis=-1, keepdims=True)
        yn = h_new * lax.rsqrt(var + NORM_EPS) * ng_ref[...]
        shift = jnp.where(is_ctx, shc_ref[...], shl_ref[...])
        scale = jnp.where(is_ctx, scc_ref[...], scl_ref[...])
        u_ref[...] = (yn * (1.0 + scale) + shift).astype(u_ref.dtype)


def _linear_res_call(x, w3, layer, h, mods4, k_gate, n_ctx, ctx_row, tm, tn, name, norm=None):
    b, r, k = x.shape
    n = w3.shape[-1]
    ngate = n // tn

    def mod_spec(lyr, row_of, kk):
        return pl.BlockSpec((None, None, 1, tn), lambda j, bi, i: (lyr, row_of(bi), 0, kk * ngate + j))

    ctx_of = lambda bi: ctx_row
    lat_of = lambda bi: bi
    in_specs = [
        pl.BlockSpec((None, tm, k), lambda j, bi, i: (bi, i, 0)),
        pl.BlockSpec((None, k, tn), lambda j, bi, i: (layer, 0, j)),
        pl.BlockSpec((None, tm, tn), lambda j, bi, i: (bi, i, j)),
        mod_spec(layer, ctx_of, k_gate),
        mod_spec(layer, lat_of, k_gate),
    ]
    args = [x, w3, h, mods4, mods4]
    out_specs = [pl.BlockSpec((None, tm, tn), lambda j, bi, i: (bi, i, j))]
    out_shape = [jax.ShapeDtypeStruct((b, r, n), F32)]
    if norm is not None:
        gains, n_layer, k_shift, k_scale = norm
        assert tn == n
        in_specs += [
            pl.BlockSpec((None, 1, n), lambda j, bi, i: (n_layer, 0, 0)),
            mod_spec(n_layer, ctx_of, k_shift), mod_spec(n_layer, lat_of, k_shift),
            mod_spec(n_layer, ctx_of, k_scale), mod_spec(n_layer, lat_of, k_scale),
        ]
        args += [gains, mods4, mods4, mods4, mods4]
        out_specs.append(pl.BlockSpec((None, tm, tn), lambda j, bi, i: (bi, i, j)))
        out_shape.append(jax.ShapeDtypeStruct((b, r, n), BF16))
    res = pl.pallas_call(
        functools.partial(_linear_res_kernel, tm=tm, n_ctx=n_ctx, with_norm=norm is not None),
        grid=(n // tn, b, r // tm),
        in_specs=in_specs,
        out_specs=out_specs,
        out_shape=out_shape,
        compiler_params=_cparams(3),
        name=name,
    )(*args)
    return res if norm is not None else res[0]


def _swiglu_kernel(x_ref, wg_ref, wu_ref, o_ref):
    x = x_ref[...]
    g = jnp.dot(x, wg_ref[...], preferred_element_type=F32)
    up = jnp.dot(x, wu_ref[...], preferred_element_type=F32)
    o_ref[...] = (g * _sigmoid(g) * up).astype(o_ref.dtype)


def _swiglu_call(x, w3, layer, tm, tn):
    b, r, k = x.shape
    hid = w3.shape[-1] // 2
    nb = hid // tn
    return pl.pallas_call(
        _swiglu_kernel,
        grid=(nb, b, r // tm),
        in_specs=[
            pl.BlockSpec((None, tm, k), lambda j, bi, i: (bi, i, 0)),
            pl.BlockSpec((None, k, tn), lambda j, bi, i: (layer, 0, j)),
            pl.BlockSpec((None, k, tn), lambda j, bi, i: (layer, 0, nb + j)),
        ],
        out_specs=pl.BlockSpec((None, tm, tn), lambda j, bi, i: (bi, i, j)),
        out_shape=jax.ShapeDtypeStruct((b, r, hid), BF16),
        compiler_params=_cparams(3),
        name="ffn_in",
    )(x, w3, w3)


def _norm_rope(x, gain, cs, sn):
    var = jnp.mean(x * x, axis=-1, keepdims=True)
    y = x * lax.rsqrt(var + NORM_EPS) * gain
    if cs is None:
        return y
    lane = lax.broadcasted_iota(jnp.int32, y.shape, 1)
    nxt = pltpu.roll(y, ATT_HD - 1, 1)
    prv = pltpu.roll(y, 1, 1)
    partner = jnp.where(lane % 2 == 0, nxt, prv)
    return y * cs + partner * sn


def _attn_kernel(q_ref, k_ref, v_ref, qg_ref, kg_ref, cos_ref, sin_ref, o_ref, kt_s, *, n_ctx, tq):
    qi = pl.program_id(2)
    rows = k_ref.shape[0]
    scale = ATT_HD ** -0.5

    @pl.when(qi == 0)
    def _prepare_k():
        kg = kg_ref[...]
        for c in range(rows // tq):
            kc = k_ref[c * tq:(c + 1) * tq, :].astype(F32)
            if c == 0:
                kn = _norm_rope(kc, kg, None, None)
            else:
                kn = _norm_rope(kc, kg, cos_ref[(c - 1) * tq:c * tq, :], sin_ref[(c - 1) * tq:c * tq, :])
            kt_s[:, c * tq:(c + 1) * tq] = kn.T.astype(BF16)

    def attend(cs, sn, n_keys):
        qg = qg_ref[...]
        scores = []
        for g in range(ATT_GROUP):
            q = _norm_rope(q_ref[:, g * ATT_HD:(g + 1) * ATT_HD].astype(F32), qg, cs, sn) * scale
            scores.append(jnp.dot(q.astype(BF16), kt_s[:, :n_keys], preferred_element_type=F32))
        for g in range(ATT_GROUP):
            s = scores[g]
            m = jnp.max(s, axis=-1, keepdims=True)
            p = jnp.exp(s - m)
            den = jnp.sum(p, axis=-1, keepdims=True)
            o = jnp.dot(p.astype(BF16), v_ref[:n_keys, :], preferred_element_type=F32) / den
            o_ref[:, g * ATT_HD:(g + 1) * ATT_HD] = o.astype(o_ref.dtype)

    @pl.when(qi == 0)
    def _context_queries():
        attend(None, None, n_ctx)

    @pl.when(qi > 0)
    def _latent_queries():
        off = pl.multiple_of((qi - 1) * tq, tq)
        attend(cos_ref[pl.ds(off, tq), :], sin_ref[pl.ds(off, tq), :], rows)


def _attn_call(p, q_g, k_g, cos_t, sin_t, layer, n_ctx):
    b, r, _ = p.shape
    tq = n_ctx
    t = r - n_ctx
    qw = ATT_GROUP * ATT_HD
    return pl.pallas_call(
        functools.partial(_attn_kernel, n_ctx=n_ctx, tq=tq),
        grid=(b, ATT_KV_HEADS, r // tq),
        in_specs=[
            pl.BlockSpec((None, tq, qw), lambda bi, kv, qi: (bi, qi, COL_AQ // qw + kv)),
            pl.BlockSpec((None, r, ATT_HD), lambda bi, kv, qi: (bi, 0, COL_AK // ATT_HD + kv)),
            pl.BlockSpec((None, r, ATT_HD), lambda bi, kv, qi: (bi, 0, COL_AV // ATT_HD + kv)),
            pl.BlockSpec((None, 1, ATT_HD), lambda bi, kv, qi: (layer, 0, 0)),
            pl.BlockSpec((None, 1, ATT_HD), lambda bi, kv, qi: (layer, 0, 0)),
            pl.BlockSpec((t, ATT_HD), lambda bi, kv, qi: (0, 0)),
            pl.BlockSpec((t, ATT_HD), lambda bi, kv, qi: (0, 0)),
        ],
        out_specs=pl.BlockSpec((None, tq, qw), lambda bi, kv, qi: (bi, qi, kv)),
        out_shape=jax.ShapeDtypeStruct((b, r, ATT_Q), BF16),
        scratch_shapes=[pltpu.VMEM((ATT_HD, r), BF16)],
        compiler_params=_cparams(3),
        name="gqa",
    )(p, p, p, q_g, k_g, cos_t, sin_t)


def _gla_direction(q_ref, k_ref, v_ref, dec_ref, wd_ref, bd_ref, vis_ref, st_s, reverse, n_chunks):
    c_len = GLA_CHUNK

    vis = vis_ref[...]
    tri = vis > 0.5
    tri_b = vis.astype(BF16)
    nt = (((1,), (1,)), ((), ()))
    tn = (((0,), (0,)), ((), ()))

    x = jnp.dot(dec_ref[...].astype(BF16), wd_ref[...].astype(BF16), preferred_element_type=F32) + bd_ref[...]
    g = (jnp.minimum(x, 0.0) - jnp.log(1.0 + jnp.exp(-jnp.abs(x)))) * (1.0 / GLA_GATE_NORM)
    g_hi = g.astype(BF16)
    g_lo = (g - g_hi.astype(F32)).astype(BF16)
    bcum = jnp.dot(tri_b, g_hi, preferred_element_type=F32) + jnp.dot(tri_b, g_lo, preferred_element_type=F32)
    q = q_ref[...].astype(F32) * (GLA_HDK ** -0.5)
    k = k_ref[...].astype(F32)
    v = v_ref[...]
    q_dec = (q * jnp.exp(bcum)).astype(BF16)
    k_inv = (k * jnp.exp(-bcum)).astype(BF16)
    att = lax.dot_general(q_dec, k_inv, nt, preferred_element_type=F32)
    att = jnp.where(tri, att, 0.0).astype(BF16)
    o_intra = jnp.dot(att, v, preferred_element_type=F32)
    b_last, k_dec = [], []
    for c in range(n_chunks):
        sl = slice(c * c_len, (c + 1) * c_len)
        end = c * c_len if reverse else (c + 1) * c_len - 1
        b_last.append(bcum[end:end + 1, :])
        k_dec.append((k[sl, :] * jnp.exp(b_last[c] - bcum[sl, :])).astype(BF16))

    outs = [None] * n_chunks
    chunk_order = range(n_chunks - 1, -1, -1) if reverse else range(n_chunks)
    st = st_s[...]
    for c in chunk_order:
        sl = slice(c * c_len, (c + 1) * c_len)
        outs[c] = o_intra[sl, :] + lax.dot_general(q_dec[sl, :], st.astype(BF16), nt, preferred_element_type=F32)
        st = st * jnp.exp(b_last[c]) + lax.dot_general(v[sl, :], k_dec[c], tn, preferred_element_type=F32)
    st_s[...] = st
    return jnp.concatenate(outs, axis=0)


def _gla_kernel(qf_ref, kf_ref, vf_ref, decf_ref, qb_ref, kb_ref, vb_ref, decb_ref,
                wdf_ref, bdf_ref, wdb_ref, bdb_ref, visf_ref, visb_ref, r_ref, ng_ref, o_ref,
                stf_s, stb_s, of_s, ob_s, *, n_chunks, nblk):
    t = pl.program_id(2)
    tb = n_chunks * GLA_CHUNK

    @pl.when(t == 0)
    def _zero_states():
        stf_s[...] = jnp.zeros_like(stf_s)
        stb_s[...] = jnp.zeros_like(stb_s)

    o_f = _gla_direction(qf_ref, kf_ref, vf_ref, decf_ref, wdf_ref, bdf_ref, visf_ref, stf_s, False, n_chunks)
    o_b = _gla_direction(qb_ref, kb_ref, vb_ref, decb_ref, wdb_ref, bdb_ref, visb_ref, stb_s, True, n_chunks)
    blk_b = jnp.where(t == 0, 0, nblk - t)
    of_s[pl.ds(pl.multiple_of(t * tb, tb), tb), :] = o_f
    ob_s[pl.ds(pl.multiple_of(blk_b * tb, tb), tb), :] = o_b

    @pl.when(t == nblk - 1)
    def _combine_norm_gate():
        for c in range(nblk):
            sl = slice(c * tb, (c + 1) * tb)
            o = of_s[sl, :] + ob_s[sl, :]
            var = jnp.mean(o * o, axis=-1, keepdims=True)
            y = o * lax.rsqrt(var + NORM_EPS) * ng_ref[...]
            r = r_ref[sl, :].astype(F32)
            o_ref[sl, :] = (y * (r * _sigmoid(r))).astype(o_ref.dtype)


def _gla_call(p, dec, wd_pad, bd, norm_g, layer, n_ctx):
    b, r, _ = p.shape
    tb = n_ctx
    nblk = r // tb

    def blk(t, direction):
        if direction == 0:
            return t
        return jnp.where(t == 0, 0, nblk - t)

    def stream_specs(direction):
        return [
            pl.BlockSpec((None, tb, GLA_HDK), lambda bi, h, t: (bi, blk(t, direction), COL_GQ // GLA_HDK + h)),
            pl.BlockSpec((None, tb, GLA_HDK), lambda bi, h, t: (bi, blk(t, direction), COL_GK // GLA_HDK + h)),
            pl.BlockSpec((None, tb, GLA_HDV), lambda bi, h, t: (bi, blk(t, direction), COL_GV // GLA_HDV + h)),
            pl.BlockSpec((None, tb, DEC_PAD), lambda bi, h, t: (bi, blk(t, direction), 0)),
        ]

    def decay_specs(direction):
        return [
            pl.BlockSpec((None, None, DEC_PAD, GLA_HDK), lambda bi, h, t: (layer, direction, 0, h)),
            pl.BlockSpec((None, None, 1, GLA_HDK), lambda bi, h, t: (layer, direction, 0, h)),
        ]

    idx = np.arange(tb)
    same_chunk = (idx[:, None] // GLA_CHUNK) == (idx[None, :] // GLA_CHUNK)
    vis = jnp.asarray(np.stack([same_chunk & (idx[:, None] >= idx[None, :]),
                                same_chunk & (idx[None, :] >= idx[:, None])]).astype(np.float32))

    return pl.pallas_call(
        functools.partial(_gla_kernel, n_chunks=tb // GLA_CHUNK, nblk=nblk),
        grid=(b, GLA_HEADS, nblk),
        in_specs=stream_specs(0) + stream_specs(1) + decay_specs(0) + decay_specs(1) + [
            pl.BlockSpec((None, tb, tb), lambda bi, h, t: (0, 0, 0)),
            pl.BlockSpec((None, tb, tb), lambda bi, h, t: (1, 0, 0)),
            pl.BlockSpec((None, r, GLA_HDV), lambda bi, h, t: (bi, 0, COL_GR // GLA_HDV + h)),
            pl.BlockSpec((None, 1, GLA_HDV), lambda bi, h, t: (layer, 0, 0)),
        ],
        out_specs=pl.BlockSpec((None, r, GLA_HDV), lambda bi, h, t: (bi, 0, h)),
        out_shape=jax.ShapeDtypeStruct((b, r, GLA_DV), BF16),
        scratch_shapes=[pltpu.VMEM((GLA_HDV, GLA_HDK), F32)] * 2 + [pltpu.VMEM((r, GLA_HDV), F32)] * 2,
        compiler_params=_cparams(3),
        name="gla",
    )(p, p, p, dec, p, p, p, dec, wd_pad, bd, wd_pad, bd, vis, vis, p, norm_g)


LRU_PAD = 8


def _lru_kernel(x_ref, y_ref, cw_ref, cb_ref, wa_ref, ba_ref, wi_ref, bi_ref, lam_ref, o_ref,
                xp_s, af_s, bf_s, ab_s, bb_s, *, n_ctx, tc, tr):
    rows = x_ref.shape[0]
    n_lat = rows - n_ctx
    sub = 8
    ctx0 = LRU_PAD
    lat0 = 2 * LRU_PAD + n_ctx
    zpad = jnp.zeros((LRU_PAD, tc), F32)
    xp_s[0:LRU_PAD, :] = zpad
    xp_s[ctx0 + n_ctx:lat0, :] = zpad
    xp_s[lat0 + n_lat:lat0 + n_lat + LRU_PAD, :] = zpad
    xp_s[ctx0:ctx0 + n_ctx, :] = x_ref[0:n_ctx, :].astype(F32)
    xp_s[lat0:lat0 + n_lat, :] = x_ref[n_ctx:rows, :].astype(F32)

    cw = cw_ref[...]
    cb = cb_ref[...]
    lam = lam_ref[...]
    neg_sp = -(jnp.maximum(-lam, 0.0) + jnp.log1p(jnp.exp(-jnp.abs(lam))))
    log_a_half = (0.5 * LRU_C) * neg_sp
    row_id = lax.broadcasted_iota(jnp.int32, (tr, 1), 0)
    n_kb = tc // LRU_BS
    wa_h = [[(0.5 * wa_ref[d, kb]).astype(BF16) for kb in range(n_kb)] for d in range(2)]
    wi_h = [[(0.5 * wi_ref[d, kb]).astype(BF16) for kb in range(n_kb)] for d in range(2)]
    ba_h = 0.5 * ba_ref[...]
    bi_h = 0.5 * bi_ref[...]

    for c in range(rows // tr):
        r0 = c * tr
        p0 = (ctx0 if r0 < n_ctx else lat0 - n_ctx) + r0
        n_win = tr + 2 * sub
        win = xp_s[p0 - sub:p0 + tr + sub, :]
        xc = cb + win[sub:sub + tr, :] * cw[CONV_LEFT:CONV_LEFT + 1, :]
        for j in range(CONV_W):
            off = j - CONV_LEFT
            if off != 0:
                xc = xc + pltpu.roll(win, (-off) % n_win, 0)[sub:sub + tr, :] * cw[j:j + 1, :]
        xh = 0.5 * xc
        xb = [xc[:, kb * LRU_BS:(kb + 1) * LRU_BS].astype(BF16) for kb in range(n_kb)]
        for d, (a_s, b_s) in enumerate(((af_s, bf_s), (ab_s, bb_s))):
            za = jnp.concatenate([jnp.dot(xb[kb], wa_h[d][kb], preferred_element_type=F32)
                                  for kb in range(n_kb)], axis=1)
            zi = jnp.concatenate([jnp.dot(xb[kb], wi_h[d][kb], preferred_element_type=F32)
                                  for kb in range(n_kb)], axis=1)
            ta = jnp.tanh(za + ba_h[d:d + 1, :])
            ti = jnp.tanh(zi + bi_h[d:d + 1, :])
            la_h = log_a_half[d:d + 1, :]
            a = jnp.exp(la_h + la_h * ta)
            q = 1.0 - a * a
            mult = jnp.where(q > 0.0, q * lax.rsqrt(q), 0.0)
            first = 0 if d == 0 else n_ctx - 1
            if r0 <= first < r0 + tr:
                mult = jnp.where(row_id == first - r0, 1.0, mult)
            a_s[r0:r0 + tr, :] = a
            b_s[r0:r0 + tr, :] = (mult * xh) * (1.0 + ti)

    row8 = lax.broadcasted_iota(jnp.int32, (sub, tc), 0)

    def tile_maps(a, b, reverse):
        for dlt in (1, 2, 4):
            shift = sub - dlt if reverse else dlt
            valid = (row8 < sub - dlt) if reverse else (row8 >= dlt)
            a_sh = jnp.where(valid, pltpu.roll(a, shift, 0), 1.0)
            b_sh = jnp.where(valid, pltpu.roll(b, shift, 0), 0.0)
            b = b + a * b_sh
            a = a * a_sh
        return a, b

    def scan_tile(tile_f, tile_b, carry):
        hf, hb = carry
        rf = pl.ds(pl.multiple_of(tile_f * sub, sub), sub)
        rb = pl.ds(pl.multiple_of(tile_b * sub, sub), sub)
        a_f, b_f = tile_maps(af_s[rf, :], bf_s[rf, :], False)
        a_b, b_b = tile_maps(ab_s[rb, :], bb_s[rb, :], True)
        h_f = b_f + a_f * hf
        h_b = b_b + a_b * hb
        bf_s[rf, :] = h_f
        bb_s[rb, :] = h_b
        return h_f[sub - 1:sub, :], h_b[0:1, :]

    zero = jnp.zeros((1, tc), F32)
    ctx_tiles = n_ctx // sub
    all_tiles = rows // sub
    carry = lax.fori_loop(0, ctx_tiles, lambda i, cr: scan_tile(i, ctx_tiles - 1 - i, cr), (zero, zero),
                          unroll=2)
    lax.fori_loop(ctx_tiles, all_tiles, lambda i, cr: scan_tile(i, all_tiles - 1 + ctx_tiles - i, cr), carry,
                  unroll=2)

    k0 = 0.7978845608028654
    for c in range(rows // tr):
        sl = slice(c * tr, (c + 1) * tr)
        y = y_ref[sl, :].astype(F32)
        gelu = 0.5 * y * (1.0 + jnp.tanh(k0 * (y + 0.044715 * (y * y * y))))
        o_ref[sl, :] = ((bf_s[sl, :] + bb_s[sl, :]) * gelu).astype(o_ref.dtype)


def _lru_call(p, conv_w, conv_b, w_a, b_a, w_i, b_i, lam, layer, n_ctx):
    b, r, _ = p.shape
    tc = 256
    nb = tc // LRU_BS
    chan = lambda bi, j: (layer, 0, j)
    return pl.pallas_call(
        functools.partial(_lru_kernel, n_ctx=n_ctx, tc=tc, tr=n_ctx),
        grid=(b, LRU_W // tc),
        in_specs=[
            pl.BlockSpec((None, r, tc), lambda bi, j: (bi, 0, COL_LX // tc + j)),
            pl.BlockSpec((None, r, tc), lambda bi, j: (bi, 0, COL_LY // tc + j)),
            pl.BlockSpec((None, CONV_W, tc), chan),
            pl.BlockSpec((None, 1, tc), chan),
            pl.BlockSpec((None, 2, nb, LRU_BS, LRU_BS), lambda bi, j: (layer, 0, j, 0, 0)),
            pl.BlockSpec((None, 2, tc), chan),
            pl.BlockSpec((None, 2, nb, LRU_BS, LRU_BS), lambda bi, j: (layer, 0, j, 0, 0)),
            pl.BlockSpec((None, 2, tc), chan),
            pl.BlockSpec((None, 2, tc), chan),
        ],
        out_specs=pl.BlockSpec((None, r, tc), lambda bi, j: (bi, 0, j)),
        out_shape=jax.ShapeDtypeStruct((b, r, LRU_W), BF16),
        scratch_shapes=[pltpu.VMEM((r + 3 * LRU_PAD, tc), F32)] + [pltpu.VMEM((r, tc), F32)] * 4,
        compiler_params=_cparams(2),
        name="rglru",
    )(p, p, conv_w, conv_b, w_a, b_a, w_i, b_i, lam)


def _merge_kernel(xa_ref, xb_ref, xc_ref, wa_ref, wb_ref, wc_ref, ga_ref, gb_ref, gc_ref, bm_ref, o_ref):
    acc = None
    for n, (x_ref, w_ref, g_ref) in enumerate(((xa_ref, wa_ref, ga_ref), (xb_ref, wb_ref, gb_ref),
                                               (xc_ref, wc_ref, gc_ref))):
        proj = jnp.dot(x_ref[...], w_ref[...], preferred_element_type=F32)
        term = _sigmoid(g_ref[...].astype(F32) + bm_ref[n:n + 1, :]) * proj
        acc = term if acc is None else acc + term
    o_ref[...] = acc.astype(o_ref.dtype)


def _merge_call(o_gla, o_att, o_lru, wbr, p, b_merge, layer, tm, tn):
    b, r, d = o_gla.shape
    ncol = d // tn
    x_spec = pl.BlockSpec((None, tm, d), lambda j, bi, i: (bi, i, 0))

    def w_spec(n):
        return pl.BlockSpec((None, None, d, tn), lambda j, bi, i: (layer, n, 0, j))

    def g_spec(n):
        return pl.BlockSpec((None, tm, tn), lambda j, bi, i: (bi, i, COL_GT // tn + n * ncol + j))

    return pl.pallas_call(
        _merge_kernel,
        grid=(ncol, b, r // tm),
        in_specs=[x_spec, x_spec, x_spec, w_spec(0), w_spec(1), w_spec(2), g_spec(0), g_spec(1), g_spec(2),
                  pl.BlockSpec((None, N_BRANCH, tn), lambda j, bi, i: (layer, 0, j))],
        out_specs=pl.BlockSpec((None, tm, tn), lambda j, bi, i: (bi, i, j)),
        out_shape=jax.ShapeDtypeStruct((b, r, d), BF16),
        compiler_params=_cparams(3),
        name="merge",
    )(o_gla, o_att, o_lru, wbr, wbr, wbr, p, p, p, b_merge)


def _final_norm_kernel(h_ref, g_ref, o_ref):
    x = h_ref[...]
    var = jnp.mean(x * x, axis=-1, keepdims=True)
    o_ref[...] = x * lax.rsqrt(var + NORM_EPS) * g_ref[...]


def _final_norm_call(h, g, n_ctx):
    b, r, d = h.shape
    tr = n_ctx
    return pl.pallas_call(
        _final_norm_kernel,
        grid=(b, (r - n_ctx) // tr),
        in_specs=[
            pl.BlockSpec((None, tr, d), lambda bi, i: (bi, i + 1, 0)),
            pl.BlockSpec((1, d), lambda bi, i: (0, 0)),
        ],
        out_specs=pl.BlockSpec((None, tr, d), lambda bi, i: (bi, i, 0)),
        out_shape=jax.ShapeDtypeStruct((b, r - n_ctx, d), F32),
        compiler_params=_cparams(2),
        name="final_norm",
    )(h, g)


def _rope_tables(n_tokens):
    rows = n_tokens // GRID_W
    row = jnp.repeat(jnp.arange(rows, dtype=F32), GRID_W)
    col = jnp.tile(jnp.arange(GRID_W, dtype=F32), rows)
    inv = ROPE_THETA ** (-jnp.arange(ROPE_AXIS_PAIRS, dtype=F32) / ROPE_AXIS_PAIRS)
    ang = jnp.concatenate([row[:, None] * inv, col[:, None] * inv], axis=-1)
    cos, sin = jnp.cos(ang), jnp.sin(ang)
    cos_t = jnp.repeat(cos, 2, axis=-1)
    sin_t = jnp.stack([-sin, sin], axis=-1).reshape(n_tokens, ATT_HD)
    return cos_t, sin_t


def kernel(x, c, ctx, c_ctx, w_mod, b_mod, norm_mix_g, norm_ffn_g, w_in, gla_w_decay, gla_b_decay, gla_norm_g, q_norm_g, k_norm_g, conv_w, conv_b, lru_w_a, lru_b_a, lru_w_i, lru_b_i, lru_lambda, b_merge, w_branch, w_out, w_ffn_in, w_ffn_out, final_norm_g):
    depth = w_in.shape[0]
    batch, t_lat, d = x.shape
    n_ctx = ctx.shape[1]
    rows = n_ctx + t_lat
    ctx_row = batch
    assert batch < MOD_ROWS and rows % n_ctx == 0 and n_ctx % GLA_CHUNK == 0
    tm = 768 if rows % 768 == 0 else n_ctx
    tm_small = 384 if rows % 384 == 0 else n_ctx
    tn_in, tn_merge, tn_out, tn_ffn_in, tn_ffn_out = 1024, 512, d, 512, 1024

    w_in_t = jnp.swapaxes(w_in, 1, 2)
    w_main = _win_prep_call(w_in_t)
    w_branch_b = w_branch.astype(BF16)
    w_out_b = w_out.astype(BF16)
    w_ffn_in_b = w_ffn_in.astype(BF16)
    w_ffn_out_b = w_ffn_out.astype(BF16)
    wd_pad = jnp.zeros((depth, 2, DEC_PAD, GLA_DK), F32)
    wd_pad = wd_pad.at[:, 0, :GLA_RANK].set(gla_w_decay[:, 0])
    wd_pad = wd_pad.at[:, 1, GLA_RANK:2 * GLA_RANK].set(gla_w_decay[:, 1])
    bd = gla_b_decay.reshape(depth, 2, 1, GLA_DK)
    cos_t, sin_t = _rope_tables(t_lat)

    c16 = jnp.zeros((MOD_ROWS, d), F32).at[:batch].set(c).at[ctx_row].set(c_ctx)
    mods4 = _mods_call(c16, w_mod, b_mod).reshape(depth, MOD_ROWS, 1, N_MOD * d)

    g_mix = norm_mix_g.reshape(depth, 1, d)
    g_ffn = norm_ffn_g.reshape(depth, 1, d)
    g_gla = gla_norm_g.reshape(depth, 1, GLA_HDV)
    g_q = q_norm_g.reshape(depth, 1, ATT_HD)
    g_k = k_norm_g.reshape(depth, 1, ATT_HD)
    cb3 = conv_b.reshape(depth, 1, LRU_W)

    h = jnp.concatenate([ctx, x], axis=1)
    for l in range(depth):
        u = _normmod_call(h, g_mix, mods4, l, 0, 1, n_ctx, ctx_row)
        p = _linear_call(u, w_main, l, BF16, tm, tn_in, "in_proj")
        dec = _decay_proj_call(u, w_in_t, l, tm)
        o_att = _attn_call(p, g_q, g_k, cos_t, sin_t, l, n_ctx)
        o_gla = _gla_call(p, dec, wd_pad, bd, g_gla, l, n_ctx)
        o_lru = _lru_call(p, conv_w, cb3, lru_w_a, lru_b_a, lru_w_i, lru_b_i, lru_lambda, l, n_ctx)
        merged = _merge_call(o_gla, o_att, o_lru, w_branch_b, p, b_merge, l, tm, tn_merge)
        h, u2 = _linear_res_call(merged, w_out_b, l, h, mods4, 2, n_ctx, ctx_row, tm_small, tn_out, "out_proj",
                                 norm=(g_ffn, l, 3, 4))
        a = _swiglu_call(u2, w_ffn_in_b, l, tm, tn_ffn_in)
        h = _linear_res_call(a, w_ffn_out_b, l, h, mods4, 5, n_ctx, ctx_row, tm_small, tn_ffn_out, "ffn_out")
    return _final_norm_call(h, final_norm_g.reshape(1, d), n_ctx)
```

```python
import functools

import jax
import jax.numpy as jnp
import numpy as np
from jax import lax
from jax.experimental import pallas as pl
from jax.experimental.pallas import tpu as pltpu

F32 = jnp.float32
BF16 = jnp.bfloat16

D_MODEL = 2048
GRID_W = 64
NORM_EPS = 1e-6
N_MOD = 6

GLA_HEADS = 4
GLA_DK = D_MODEL // 2
GLA_DV = D_MODEL
GLA_HDK = GLA_DK // GLA_HEADS
GLA_HDV = GLA_DV // GLA_HEADS
GLA_RANK = 16
GLA_GATE_NORM = 16.0
GLA_CHUNK = 64
GLA_HEADS_PER_STEP = 2

ATT_HEADS = 16
ATT_KV_HEADS = 4
ATT_HD = 128
ATT_GROUP = ATT_HEADS // ATT_KV_HEADS
ATT_Q = ATT_HEADS * ATT_HD
ATT_KV = ATT_KV_HEADS * ATT_HD
ROPE_THETA = 10000.0
ROPE_AXIS_PAIRS = ATT_HD // 4

LRU_W = D_MODEL
LRU_BLOCKS = 16
LRU_BS = LRU_W // LRU_BLOCKS
LRU_C = 8.0
CONV_W = 4
CONV_LEFT = 2

N_BRANCH = 3
FFN_HIDDEN = -(-8 * D_MODEL // (3 * 256)) * 256

COL_GQ = 0
COL_GK = COL_GQ + GLA_DK
COL_GV = COL_GK + GLA_DK
COL_GR = COL_GV + GLA_DV
COL_AQ = COL_GR + GLA_DV
COL_AK = COL_AQ + ATT_Q
COL_AV = COL_AK + ATT_KV
COL_LX = COL_AV + ATT_KV
COL_LY = COL_LX + LRU_W
COL_GT = COL_LY + LRU_W
N_MAIN = COL_GT + N_BRANCH * D_MODEL
DEC_ORIG = 2 * GLA_DK + 2 * GLA_DV
DEC_PAD = 128

V7X_VMEM_LIMIT_BYTES = 56 * 1024 * 1024
MOD_ROWS = 16


def _cparams(n_axes, vmem=V7X_VMEM_LIMIT_BYTES):
    return pltpu.CompilerParams(dimension_semantics=("arbitrary",) * n_axes, vmem_limit_bytes=vmem)


def _sigmoid(x):
    return 0.5 * jnp.tanh(0.5 * x) + 0.5


def _mods_kernel(c_ref, w_ref, b_ref, o_ref):
    c = c_ref[...]
    sc = (c * _sigmoid(c)).astype(BF16)
    o_ref[...] = jnp.dot(sc, w_ref[...].astype(BF16), preferred_element_type=F32) + b_ref[...]


def _mods_call(c16, w_mod, b_mod):
    depth, d, n = w_mod.shape
    tn = 1024
    return pl.pallas_call(
        _mods_kernel,
        grid=(depth, n // tn),
        in_specs=[
            pl.BlockSpec((MOD_ROWS, d), lambda l, j: (0, 0)),
            pl.BlockSpec((None, d, tn), lambda l, j: (l, 0, j)),
            pl.BlockSpec((None, 1, tn), lambda l, j: (l, 0, j)),
        ],
        out_specs=pl.BlockSpec((None, MOD_ROWS, tn), lambda l, j: (l, 0, j)),
        out_shape=jax.ShapeDtypeStruct((depth, MOD_ROWS, n), F32),
        compiler_params=_cparams(2),
        name="mods",
    )(c16, w_mod, b_mod.reshape(depth, 1, n))


def _win_prep_kernel(a_ref, b_ref, o_ref, *, n_plain, shift):
    j = pl.program_id(1)

    @pl.when(j < n_plain)
    def _before_decay_columns():
        o_ref[...] = a_ref[...].T.astype(o_ref.dtype)

    @pl.when(j >= n_plain)
    def _after_decay_columns():
        cols = jnp.concatenate([a_ref[shift:, :], b_ref[...]], axis=0)
        o_ref[...] = cols.T.astype(o_ref.dtype)


def _win_prep_call(w_in_t):
    depth, _, d = w_in_t.shape
    tn, shift = 512, 2 * GLA_RANK
    assert DEC_ORIG % tn == 0 and N_MAIN % tn == 0 and tn % shift == 0
    return pl.pallas_call(
        functools.partial(_win_prep_kernel, n_plain=DEC_ORIG // tn, shift=shift),
        grid=(depth, N_MAIN // tn),
        in_specs=[
            pl.BlockSpec((None, tn, d), lambda l, j: (l, j, 0)),
            pl.BlockSpec((None, shift, d), lambda l, j: (l, (j + 1) * (tn // shift), 0)),
        ],
        out_specs=pl.BlockSpec((None, d, tn), lambda l, j: (l, 0, j)),
        out_shape=jax.ShapeDtypeStruct((depth, d, N_MAIN), BF16),
        compiler_params=_cparams(2),
        name="w_in_prep",
    )(w_in_t, w_in_t)


def _normmod_kernel(h_ref, g_ref, sh_ref, sc_ref, wt_ref, o_ref, dec_ref):
    x = h_ref[...]
    var = jnp.mean(x * x, axis=-1, keepdims=True)
    y = x * lax.rsqrt(var + NORM_EPS) * g_ref[...]
    u = (y * (1.0 + sc_ref[...]) + sh_ref[...]).astype(o_ref.dtype)
    o_ref[...] = u
    dec_ref[...] = lax.dot_general(u, wt_ref[...].astype(BF16), (((1,), (1,)), ((), ())),
                                   preferred_element_type=F32)


def _normmod_call(h, g, mods4, w_in_t, layer, k_shift, k_scale, n_ctx, ctx_row):
    b, r, d = h.shape
    tr = n_ctx

    def mod_map(k):
        return lambda bi, i: (layer, jnp.where(i == 0, ctx_row, bi), 0, k)

    return pl.pallas_call(
        _normmod_kernel,
        grid=(b, r // tr),
        in_specs=[
            pl.BlockSpec((None, tr, d), lambda bi, i: (bi, i, 0)),
            pl.BlockSpec((None, 1, d), lambda bi, i: (layer, 0, 0)),
            pl.BlockSpec((None, None, 1, d), mod_map(k_shift)),
            pl.BlockSpec((None, None, 1, d), mod_map(k_scale)),
            pl.BlockSpec((None, DEC_PAD, d), lambda bi, i: (layer, DEC_ORIG // DEC_PAD, 0)),
        ],
        out_specs=[pl.BlockSpec((None, tr, d), lambda bi, i: (bi, i, 0)),
                   pl.BlockSpec((None, tr, DEC_PAD), lambda bi, i: (bi, i, 0))],
        out_shape=[jax.ShapeDtypeStruct((b, r, d), BF16), jax.ShapeDtypeStruct((b, r, DEC_PAD), F32)],
        compiler_params=_cparams(2),
        name="normmod",
    )(h, g, mods4, mods4, w_in_t)


def _linear_kernel(x_ref, w_ref, o_ref):
    o_ref[...] = jnp.dot(x_ref[...], w_ref[...], preferred_element_type=F32).astype(o_ref.dtype)


def _linear_call(x, w3, layer, out_dtype, tm, tn, name):
    b, r, k = x.shape
    n = w3.shape[-1]
    return pl.pallas_call(
        _linear_kernel,
        grid=(n // tn, b, r // tm),
        in_specs=[
            pl.BlockSpec((None, tm, k), lambda j, bi, i: (bi, i, 0)),
            pl.BlockSpec((None, k, tn), lambda j, bi, i: (layer, 0, j)),
        ],
        out_specs=pl.BlockSpec((None, tm, tn), lambda j, bi, i: (bi, i, j)),
        out_shape=jax.ShapeDtypeStruct((b, r, n), out_dtype),
        compiler_params=_cparams(3),
        name=name,
    )(x, w3)


def _linear_res_kernel(*refs, tm, n_ctx, with_norm):
    x_ref, w_ref, h_ref, gc_ref, gl_ref = refs[:5]
    if with_norm:
        ng_ref, shc_ref, shl_ref, scc_ref, scl_ref, o_ref, u_ref = refs[5:]
    else:
        (o_ref,) = refs[5:]
    i = pl.program_id(2)
    y = jnp.dot(x_ref[...], w_ref[...], preferred_element_type=F32)
    is_ctx = (i * tm + lax.broadcasted_iota(jnp.int32, (tm, 1), 0)) < n_ctx
    h_new = h_ref[...] + jnp.where(is_ctx, gc_ref[...], gl_ref[...]) * y
    o_ref[...] = h_new
    if with_norm:
        var = jnp.mean(h_new * h_new, axis=-1, keepdims=True)
        yn = h_new * lax.rsqrt(var + NORM_EPS) * ng_ref[...]
        shift = jnp.where(is_ctx, shc_ref[...], shl_ref[...])
        scale = jnp.where(is_ctx, scc_ref[...], scl_ref[...])
        u_ref[...] = (yn * (1.0 + scale) + shift).astype(u_ref.dtype)


def _linear_res_call(x, w3, layer, h, mods4, k_gate, n_ctx, ctx_row, tm, tn, name, norm=None):
    b, r, k = x.shape
    n = w3.shape[-1]
    ngate = n // tn

    def mod_spec(lyr, row_of, kk):
        return pl.BlockSpec((None, None, 1, tn), lambda j, bi, i: (lyr, row_of(bi), 0, kk * ngate + j))

    ctx_of = lambda bi: ctx_row
    lat_of = lambda bi: bi
    in_specs = [
        pl.BlockSpec((None, tm, k), lambda j, bi, i: (bi, i, 0)),
        pl.BlockSpec((None, k, tn), lambda j, bi, i: (layer, 0, j)),
        pl.BlockSpec((None, tm, tn), lambda j, bi, i: (bi, i, j)),
        mod_spec(layer, ctx_of, k_gate),
        mod_spec(layer, lat_of, k_gate),
    ]
    args = [x, w3, h, mods4, mods4]
    out_specs = [pl.BlockSpec((None, tm, tn), lambda j, bi, i: (bi, i, j))]
    out_shape = [jax.ShapeDtypeStruct((b, r, n), F32)]
    if norm is not None:
        gains, n_layer, k_shift, k_scale = norm
        assert tn == n
        in_specs += [
            pl.BlockSpec((None, 1, n), lambda j, bi, i: (n_layer, 0, 0)),
            mod_spec(n_layer, ctx_of, k_shift), mod_spec(n_layer, lat_of, k_shift),
            mod_spec(n_layer, ctx_of, k_scale), mod_spec(n_layer, lat_of, k_scale),
        ]
        args += [gains, mods4, mods4, mods4, mods4]
        out_specs.append(pl.BlockSpec((None, tm, tn), lambda j, bi, i: (bi, i, j)))
        out_shape.append(jax.ShapeDtypeStruct((b, r, n), BF16))
    res = pl.pallas_call(
        functools.partial(_linear_res_kernel, tm=tm, n_ctx=n_ctx, with_norm=norm is not None),
        grid=(n // tn, b, r // tm),
        in_specs=in_specs,
        out_specs=out_specs,
        out_shape=out_shape,
        compiler_params=_cparams(3),
        name=name,
    )(*args)
    return res if norm is not None else res[0]


def _swiglu_kernel(x_ref, wg_ref, wu_ref, o_ref):
    x = x_ref[...]
    g = jnp.dot(x, wg_ref[...], preferred_element_type=F32)
    up = jnp.dot(x, wu_ref[...], preferred_element_type=F32)
    o_ref[...] = (g * _sigmoid(g) * up).astype(o_ref.dtype)


def _swiglu_call(x, w3, layer, tm, tn):
    b, r, k = x.shape
    hid = w3.shape[-1] // 2
    nb = hid // tn
    return pl.pallas_call(
        _swiglu_kernel,
        grid=(nb, b, r // tm),
        in_specs=[
            pl.BlockSpec((None, tm, k), lambda j, bi, i: (bi, i, 0)),
            pl.BlockSpec((None, k, tn), lambda j, bi, i: (layer, 0, j)),
            pl.BlockSpec((None, k, tn), lambda j, bi, i: (layer, 0, nb + j)),
        ],
        out_specs=pl.BlockSpec((None, tm, tn), lambda j, bi, i: (bi, i, j)),
        out_shape=jax.ShapeDtypeStruct((b, r, hid), BF16),
        compiler_params=_cparams(3),
        name="ffn_in",
    )(x, w3, w3)


def _norm_rope(x, gain, cs, sn):
    var = jnp.mean(x * x, axis=-1, keepdims=True)
    y = x * lax.rsqrt(var + NORM_EPS) * gain
    if cs is None:
        return y
    lane = lax.broadcasted_iota(jnp.int32, y.shape, 1)
    nxt = pltpu.roll(y, ATT_HD - 1, 1)
    prv = pltpu.roll(y, 1, 1)
    partner = jnp.where(lane % 2 == 0, nxt, prv)
    return y * cs + partner * sn


def _attn_kernel(q_ref, k_ref, v_ref, qg_ref, kg_ref, cos_ref, sin_ref, o_ref, kt_s, vx_s, *, n_ctx, tq):
    qi = pl.program_id(2)
    rows = k_ref.shape[0]
    scale = ATT_HD ** -0.5 * float(np.log2(np.e))

    @pl.when(qi == 0)
    def _prepare_kv():
        kg = kg_ref[...]
        for c in range(rows // tq):
            kc = k_ref[c * tq:(c + 1) * tq, :].astype(F32)
            if c == 0:
                kn = _norm_rope(kc, kg, None, None)
            else:
                kn = _norm_rope(kc, kg, cos_ref[(c - 1) * tq:c * tq, :], sin_ref[(c - 1) * tq:c * tq, :])
            kt_s[:, c * tq:(c + 1) * tq] = kn.T.astype(BF16)
        lane = lax.broadcasted_iota(jnp.int32, (rows, ATT_HD), 1)
        vx_s[:, :ATT_HD] = v_ref[...]
        vx_s[:, ATT_HD:] = jnp.where(lane == 0, 1.0, 0.0).astype(BF16)

    def attend(cs, sn, n_keys):
        qg = qg_ref[...]
        scores = []
        for g in range(ATT_GROUP):
            q = _norm_rope(q_ref[:, g * ATT_HD:(g + 1) * ATT_HD].astype(F32), qg, cs, sn) * scale
            scores.append(jnp.dot(q.astype(BF16), kt_s[:, :n_keys], preferred_element_type=F32))
        for g in range(ATT_GROUP):
            s = scores[g]
            p = jnp.exp2(s - jnp.max(s, axis=-1, keepdims=True))
            ox = jnp.dot(p.astype(BF16), vx_s[:n_keys, :], preferred_element_type=F32)
            o = ox[:, :ATT_HD] / ox[:, ATT_HD:ATT_HD + 1]
            o_ref[:, g * ATT_HD:(g + 1) * ATT_HD] = o.astype(o_ref.dtype)

    @pl.when(qi == 0)
    def _context_queries():
        attend(None, None, n_ctx)

    @pl.when(qi > 0)
    def _latent_queries():
        off = pl.multiple_of((qi - 1) * tq, tq)
        attend(cos_ref[pl.ds(off, tq), :], sin_ref[pl.ds(off, tq), :], rows)


def _attn_call(p, q_g, k_g, cos_t, sin_t, layer, n_ctx):
    b, r, _ = p.shape
    tq = n_ctx
    t = r - n_ctx
    qw = ATT_GROUP * ATT_HD
    return pl.pallas_call(
        functools.partial(_attn_kernel, n_ctx=n_ctx, tq=tq),
        grid=(b, ATT_KV_HEADS, r // tq),
        in_specs=[
            pl.BlockSpec((None, tq, qw), lambda bi, kv, qi: (bi, qi, COL_AQ // qw + kv)),
            pl.BlockSpec((None, r, ATT_HD), lambda bi, kv, qi: (bi, 0, COL_AK // ATT_HD + kv)),
            pl.BlockSpec((None, r, ATT_HD), lambda bi, kv, qi: (bi, 0, COL_AV // ATT_HD + kv)),
            pl.BlockSpec((None, 1, ATT_HD), lambda bi, kv, qi: (layer, 0, 0)),
            pl.BlockSpec((None, 1, ATT_HD), lambda bi, kv, qi: (layer, 0, 0)),
            pl.BlockSpec((t, ATT_HD), lambda bi, kv, qi: (0, 0)),
            pl.BlockSpec((t, ATT_HD), lambda bi, kv, qi: (0, 0)),
        ],
        out_specs=pl.BlockSpec((None, tq, qw), lambda bi, kv, qi: (bi, qi, kv)),
        out_shape=jax.ShapeDtypeStruct((b, r, ATT_Q), BF16),
        scratch_shapes=[pltpu.VMEM((ATT_HD, r), BF16), pltpu.VMEM((r, 2 * ATT_HD), BF16)],
        compiler_params=_cparams(3),
        name="gqa",
    )(p, p, p, q_g, k_g, cos_t, sin_t)


def _gla_direction(q, k, v, dec, wd, bd, vis, st_s, reverse, n_chunks):
    c_len = GLA_CHUNK

    tri = vis > 0.5
    tri_b = vis.astype(BF16)
    nt = (((1,), (1,)), ((), ()))
    tn = (((0,), (0,)), ((), ()))

    x = jnp.dot(dec, wd, preferred_element_type=F32) + bd
    g = (jnp.minimum(x, 0.0) - jnp.log(1.0 + jnp.exp(-jnp.abs(x)))) * (1.0 / GLA_GATE_NORM)
    g_hi = g.astype(BF16)
    g_lo = (g - g_hi.astype(F32)).astype(BF16)
    bcum = jnp.dot(tri_b, g_hi, preferred_element_type=F32) + jnp.dot(tri_b, g_lo, preferred_element_type=F32)
    q = q.astype(F32) * (GLA_HDK ** -0.5)
    k = k.astype(F32)
    q_dec = (q * jnp.exp(bcum)).astype(BF16)
    k_inv = (k * jnp.exp(-bcum)).astype(BF16)
    att = lax.dot_general(q_dec, k_inv, nt, preferred_element_type=F32)
    att = jnp.where(tri, att, 0.0).astype(BF16)
    o_intra = jnp.dot(att, v, preferred_element_type=F32)
    b_last, k_dec = [], []
    for c in range(n_chunks):
        sl = slice(c * c_len, (c + 1) * c_len)
        end = c * c_len if reverse else (c + 1) * c_len - 1
        b_last.append(bcum[end:end + 1, :])
        k_dec.append((k[sl, :] * jnp.exp(b_last[c] - bcum[sl, :])).astype(BF16))

    outs = [None] * n_chunks
    chunk_order = range(n_chunks - 1, -1, -1) if reverse else range(n_chunks)
    st = st_s[...]
    for c in chunk_order:
        sl = slice(c * c_len, (c + 1) * c_len)
        outs[c] = o_intra[sl, :] + lax.dot_general(q_dec[sl, :], st.astype(BF16), nt, preferred_element_type=F32)
        st = st * jnp.exp(b_last[c]) + lax.dot_general(v[sl, :], k_dec[c], tn, preferred_element_type=F32)
    st_s[...] = st
    return jnp.concatenate(outs, axis=0)


def _gla_kernel(qf_ref, kf_ref, vf_ref, decf_ref, qb_ref, kb_ref, vb_ref, decb_ref,
                wdf_ref, bdf_ref, wdb_ref, bdb_ref, visf_ref, visb_ref, r_ref, ng_ref, o_ref,
                stf_s, stb_s, of_s, ob_s, *, n_chunks, nblk):
    t = pl.program_id(2)
    tb = n_chunks * GLA_CHUNK

    @pl.when(t == 0)
    def _zero_states():
        stf_s[...] = jnp.zeros_like(stf_s)
        stb_s[...] = jnp.zeros_like(stb_s)

    blk_b = jnp.where(t == 0, 0, nblk - t)
    rows_f = pl.ds(pl.multiple_of(t * tb, tb), tb)
    rows_b = pl.ds(pl.multiple_of(blk_b * tb, tb), tb)
    decf = decf_ref[...].astype(BF16)
    decb = decb_ref[...].astype(BF16)
    visf = visf_ref[...]
    visb = visb_ref[...]
    for hh in range(GLA_HEADS_PER_STEP):
        kcol = slice(hh * GLA_HDK, (hh + 1) * GLA_HDK)
        vcol = slice(hh * GLA_HDV, (hh + 1) * GLA_HDV)
        of_s[rows_f, vcol] = _gla_direction(
            qf_ref[:, kcol], kf_ref[:, kcol], vf_ref[:, vcol], decf, wdf_ref[:, kcol].astype(BF16),
            bdf_ref[:, kcol], visf, stf_s.at[hh], False, n_chunks)
        ob_s[rows_b, vcol] = _gla_direction(
            qb_ref[:, kcol], kb_ref[:, kcol], vb_ref[:, vcol], decb, wdb_ref[:, kcol].astype(BF16),
            bdb_ref[:, kcol], visb, stb_s.at[hh], True, n_chunks)

    @pl.when(t == nblk - 1)
    def _combine_norm_gate():
        for c in range(nblk):
            sl = slice(c * tb, (c + 1) * tb)
            for hh in range(GLA_HEADS_PER_STEP):
                vcol = slice(hh * GLA_HDV, (hh + 1) * GLA_HDV)
                o = of_s[sl, vcol] + ob_s[sl, vcol]
                var = jnp.mean(o * o, axis=-1, keepdims=True)
                y = o * lax.rsqrt(var + NORM_EPS) * ng_ref[...]
                r = r_ref[sl, vcol].astype(F32)
                o_ref[sl, vcol] = (y * (r * _sigmoid(r))).astype(o_ref.dtype)


def _gla_call(p, dec, wd_pad, bd, norm_g, layer, n_ctx):
    b, r, _ = p.shape
    tb = n_ctx
    nblk = r // tb

    def blk(t, direction):
        if direction == 0:
            return t
        return jnp.where(t == 0, 0, nblk - t)

    kw = GLA_HEADS_PER_STEP * GLA_HDK
    vw = GLA_HEADS_PER_STEP * GLA_HDV

    def stream_specs(direction):
        return [
            pl.BlockSpec((None, tb, kw), lambda bi, h, t: (bi, blk(t, direction), COL_GQ // kw + h)),
            pl.BlockSpec((None, tb, kw), lambda bi, h, t: (bi, blk(t, direction), COL_GK // kw + h)),
            pl.BlockSpec((None, tb, vw), lambda bi, h, t: (bi, blk(t, direction), COL_GV // vw + h)),
            pl.BlockSpec((None, tb, DEC_PAD), lambda bi, h, t: (bi, blk(t, direction), 0)),
        ]

    def decay_specs(direction):
        return [
            pl.BlockSpec((None, None, DEC_PAD, kw), lambda bi, h, t: (layer, direction, 0, h)),
            pl.BlockSpec((None, None, 1, kw), lambda bi, h, t: (layer, direction, 0, h)),
        ]

    idx = np.arange(tb)
    same_chunk = (idx[:, None] // GLA_CHUNK) == (idx[None, :] // GLA_CHUNK)
    vis = jnp.asarray(np.stack([same_chunk & (idx[:, None] >= idx[None, :]),
                                same_chunk & (idx[None, :] >= idx[:, None])]).astype(np.float32))

    return pl.pallas_call(
        functools.partial(_gla_kernel, n_chunks=tb // GLA_CHUNK, nblk=nblk),
        grid=(b, GLA_HEADS // GLA_HEADS_PER_STEP, nblk),
        in_specs=stream_specs(0) + stream_specs(1) + decay_specs(0) + decay_specs(1) + [
            pl.BlockSpec((None, tb, tb), lambda bi, h, t: (0, 0, 0)),
            pl.BlockSpec((None, tb, tb), lambda bi, h, t: (1, 0, 0)),
            pl.BlockSpec((None, r, vw), lambda bi, h, t: (bi, 0, COL_GR // vw + h)),
            pl.BlockSpec((None, 1, GLA_HDV), lambda bi, h, t: (layer, 0, 0)),
        ],
        out_specs=pl.BlockSpec((None, r, vw), lambda bi, h, t: (bi, 0, h)),
        out_shape=jax.ShapeDtypeStruct((b, r, GLA_DV), BF16),
        scratch_shapes=([pltpu.VMEM((GLA_HEADS_PER_STEP, GLA_HDV, GLA_HDK), F32)] * 2
                        + [pltpu.VMEM((r, vw), F32)] * 2),
        compiler_params=_cparams(3),
        name="gla",
    )(p, p, p, dec, p, p, p, dec, wd_pad, bd, wd_pad, bd, vis, vis, p, norm_g)


LRU_PAD = 8


def _lru_kernel(x_ref, y_ref, cw_ref, cb_ref, wa_ref, ba_ref, wi_ref, bi_ref, lam_ref, o_ref,
                xp_s, af_s, bf_s, ab_s, bb_s, *, n_ctx, tc, tr):
    rows = x_ref.shape[0]
    n_lat = rows - n_ctx
    sub = 8
    ctx0 = LRU_PAD
    lat0 = 2 * LRU_PAD + n_ctx
    zpad = jnp.zeros((LRU_PAD, tc), F32)
    xp_s[0:LRU_PAD, :] = zpad
    xp_s[ctx0 + n_ctx:lat0, :] = zpad
    xp_s[lat0 + n_lat:lat0 + n_lat + LRU_PAD, :] = zpad
    xp_s[ctx0:ctx0 + n_ctx, :] = x_ref[0:n_ctx, :].astype(F32)
    xp_s[lat0:lat0 + n_lat, :] = x_ref[n_ctx:rows, :].astype(F32)

    cw = cw_ref[...]
    cb = cb_ref[...]
    lam = lam_ref[...]
    neg_sp = -(jnp.maximum(-lam, 0.0) + jnp.log1p(jnp.exp(-jnp.abs(lam))))
    log_a_half = (0.5 * LRU_C) * neg_sp
    row_id = lax.broadcasted_iota(jnp.int32, (tr, 1), 0)
    n_kb = tc // LRU_BS
    wa_h = [[(0.5 * wa_ref[d, kb]).astype(BF16) for kb in range(n_kb)] for d in range(2)]
    wi_h = [[(0.5 * wi_ref[d, kb]).astype(BF16) for kb in range(n_kb)] for d in range(2)]
    ba_h = 0.5 * ba_ref[...]
    bi_h = 0.5 * bi_ref[...]

    for c in range(rows // tr):
        r0 = c * tr
        p0 = (ctx0 if r0 < n_ctx else lat0 - n_ctx) + r0
        n_win = tr + 2 * sub
        win = xp_s[p0 - sub:p0 + tr + sub, :]
        xc = cb + win[sub:sub + tr, :] * cw[CONV_LEFT:CONV_LEFT + 1, :]
        for j in range(CONV_W):
            off = j - CONV_LEFT
            if off != 0:
                xc = xc + pltpu.roll(win, (-off) % n_win, 0)[sub:sub + tr, :] * cw[j:j + 1, :]
        xh = 0.5 * xc
        xb = [xc[:, kb * LRU_BS:(kb + 1) * LRU_BS].astype(BF16) for kb in range(n_kb)]
        for d, (a_s, b_s) in enumerate(((af_s, bf_s), (ab_s, bb_s))):
            za = jnp.concatenate([jnp.dot(xb[kb], wa_h[d][kb], preferred_element_type=F32)
                                  for kb in range(n_kb)], axis=1)
            zi = jnp.concatenate([jnp.dot(xb[kb], wi_h[d][kb], preferred_element_type=F32)
                                  for kb in range(n_kb)], axis=1)
            ta = jnp.tanh(za + ba_h[d:d + 1, :])
            ti = jnp.tanh(zi + bi_h[d:d + 1, :])
            la_h = log_a_half[d:d + 1, :]
            a = jnp.exp(la_h + la_h * ta)
            q = 1.0 - a * a
            mult = jnp.where(q > 0.0, q * lax.rsqrt(q), 0.0)
            first = 0 if d == 0 else n_ctx - 1
            if r0 <= first < r0 + tr:
                mult = jnp.where(row_id == first - r0, 1.0, mult)
            a_s[r0:r0 + tr, :] = a
            b_s[r0:r0 + tr, :] = (mult * xh) * (1.0 + ti)

    row8 = lax.broadcasted_iota(jnp.int32, (sub, tc), 0)

    def tile_maps(a, b, reverse):
        for dlt in (1, 2, 4):
            shift = sub - dlt if reverse else dlt
            valid = (row8 < sub - dlt) if reverse else (row8 >= dlt)
            a_sh = jnp.where(valid, pltpu.roll(a, shift, 0), 1.0)
            b_sh = jnp.where(valid, pltpu.roll(b, shift, 0), 0.0)
            b = b + a * b_sh
            a = a * a_sh
        return a, b

    def scan_tile(tile_f, tile_b, carry):
        hf, hb = carry
        rf = pl.ds(pl.multiple_of(tile_f * sub, sub), sub)
        rb = pl.ds(pl.multiple_of(tile_b * sub, sub), sub)
        a_f, b_f = tile_maps(af_s[rf, :], bf_s[rf, :], False)
        a_b, b_b = tile_maps(ab_s[rb, :], bb_s[rb, :], True)
        h_f = b_f + a_f * hf
        h_b = b_b + a_b * hb
        bf_s[rf, :] = h_f
        bb_s[rb, :] = h_b
        return h_f[sub - 1:sub, :], h_b[0:1, :]

    zero = jnp.zeros((1, tc), F32)
    ctx_tiles = n_ctx // sub
    all_tiles = rows // sub
    carry = lax.fori_loop(0, ctx_tiles, lambda i, cr: scan_tile(i, ctx_tiles - 1 - i, cr), (zero, zero),
                          unroll=2)
    lax.fori_loop(ctx_tiles, all_tiles, lambda i, cr: scan_tile(i, all_tiles - 1 + ctx_tiles - i, cr), carry,
                  unroll=2)

    k0 = 0.7978845608028654
    for c in range(rows // tr):
        sl = slice(c * tr, (c + 1) * tr)
        y = y_ref[sl, :].astype(F32)
        gelu = 0.5 * y * (1.0 + jnp.tanh(k0 * (y + 0.044715 * (y * y * y))))
        o_ref[sl, :] = ((bf_s[sl, :] + bb_s[sl, :]) * gelu).astype(o_ref.dtype)


def _lru_call(p, conv_w, conv_b, w_a, b_a, w_i, b_i, lam, layer, n_ctx):
    b, r, _ = p.shape
    tc = 256
    nb = tc // LRU_BS
    chan = lambda bi, j: (layer, 0, j)
    return pl.pallas_call(
        functools.partial(_lru_kernel, n_ctx=n_ctx, tc=tc, tr=n_ctx),
        grid=(b, LRU_W // tc),
        in_specs=[
            pl.BlockSpec((None, r, tc), lambda bi, j: (bi, 0, COL_LX // tc + j)),
            pl.BlockSpec((None, r, tc), lambda bi, j: (bi, 0, COL_LY // tc + j)),
            pl.BlockSpec((None, CONV_W, tc), chan),
            pl.BlockSpec((None, 1, tc), chan),
            pl.BlockSpec((None, 2, nb, LRU_BS, LRU_BS), lambda bi, j: (layer, 0, j, 0, 0)),
            pl.BlockSpec((None, 2, tc), chan),
            pl.BlockSpec((None, 2, nb, LRU_BS, LRU_BS), lambda bi, j: (layer, 0, j, 0, 0)),
            pl.BlockSpec((None, 2, tc), chan),
            pl.BlockSpec((None, 2, tc), chan),
        ],
        out_specs=pl.BlockSpec((None, r, tc), lambda bi, j: (bi, 0, j)),
        out_shape=jax.ShapeDtypeStruct((b, r, LRU_W), BF16),
        scratch_shapes=[pltpu.VMEM((r + 3 * LRU_PAD, tc), F32)] + [pltpu.VMEM((r, tc), F32)] * 4,
        compiler_params=_cparams(2),
        name="rglru",
    )(p, p, conv_w, conv_b, w_a, b_a, w_i, b_i, lam)


def _merge_kernel(xa_ref, xb_ref, xc_ref, wa_ref, wb_ref, wc_ref, ga_ref, gb_ref, gc_ref, bm_ref, o_ref):
    acc = None
    for n, (x_ref, w_ref, g_ref) in enumerate(((xa_ref, wa_ref, ga_ref), (xb_ref, wb_ref, gb_ref),
                                               (xc_ref, wc_ref, gc_ref))):
        proj = jnp.dot(x_ref[...], w_ref[...], preferred_element_type=F32)
        term = _sigmoid(g_ref[...].astype(F32) + bm_ref[n:n + 1, :]) * proj
        acc = term if acc is None else acc + term
    o_ref[...] = acc.astype(o_ref.dtype)


def _merge_call(o_gla, o_att, o_lru, wbr, p, b_merge, layer, tm, tn):
    b, r, d = o_gla.shape
    ncol = d // tn
    x_spec = pl.BlockSpec((None, tm, d), lambda j, bi, i: (bi, i, 0))

    def w_spec(n):
        return pl.BlockSpec((None, None, d, tn), lambda j, bi, i: (layer, n, 0, j))

    def g_spec(n):
        return pl.BlockSpec((None, tm, tn), lambda j, bi, i: (bi, i, COL_GT // tn + n * ncol + j))

    return pl.pallas_call(
        _merge_kernel,
        grid=(ncol, b, r // tm),
        in_specs=[x_spec, x_spec, x_spec, w_spec(0), w_spec(1), w_spec(2), g_spec(0), g_spec(1), g_spec(2),
                  pl.BlockSpec((None, N_BRANCH, tn), lambda j, bi, i: (layer, 0, j))],
        out_specs=pl.BlockSpec((None, tm, tn), lambda j, bi, i: (bi, i, j)),
        out_shape=jax.ShapeDtypeStruct((b, r, d), BF16),
        compiler_params=_cparams(3),
        name="merge",
    )(o_gla, o_att, o_lru, wbr, wbr, wbr, p, p, p, b_merge)


def _final_norm_kernel(h_ref, g_ref, o_ref):
    x = h_ref[...]
    var = jnp.mean(x * x, axis=-1, keepdims=True)
    o_ref[...] = x * lax.rsqrt(var + NORM_EPS) * g_ref[...]


def _final_norm_call(h, g, n_ctx):
    b, r, d = h.shape
    tr = n_ctx
    return pl.pallas_call(
        _final_norm_kernel,
        grid=(b, (r - n_ctx) // tr),
        in_specs=[
            pl.BlockSpec((None, tr, d), lambda bi, i: (bi, i + 1, 0)),
            pl.BlockSpec((1, d), lambda bi, i: (0, 0)),
        ],
        out_specs=pl.BlockSpec((None, tr, d), lambda bi, i: (bi, i, 0)),
        out_shape=jax.ShapeDtypeStruct((b, r - n_ctx, d), F32),
        compiler_params=_cparams(2),
        name="final_norm",
    )(h, g)


def _rope_tables(n_tokens):
    rows = n_tokens // GRID_W
    row = jnp.repeat(jnp.arange(rows, dtype=F32), GRID_W)
    col = jnp.tile(jnp.arange(GRID_W, dtype=F32), rows)
    inv = ROPE_THETA ** (-jnp.arange(ROPE_AXIS_PAIRS, dtype=F32) / ROPE_AXIS_PAIRS)
    ang = jnp.concatenate([row[:, None] * inv, col[:, None] * inv], axis=-1)
    cos, sin = jnp.cos(ang), jnp.sin(ang)
    cos_t = jnp.repeat(cos, 2, axis=-1)
    sin_t = jnp.stack([-sin, sin], axis=-1).reshape(n_tokens, ATT_HD)
    return cos_t, sin_t


def kernel(x, c, ctx, c_ctx, w_mod, b_mod, norm_mix_g, norm_ffn_g, w_in, gla_w_decay, gla_b_decay, gla_norm_g, q_norm_g, k_norm_g, conv_w, conv_b, lru_w_a, lru_b_a, lru_w_i, lru_b_i, lru_lambda, b_merge, w_branch, w_out, w_ffn_in, w_ffn_out, final_norm_g):
    depth = w_in.shape[0]
    batch, t_lat, d = x.shape
    n_ctx = ctx.shape[1]
    rows = n_ctx + t_lat
    ctx_row = batch
    assert batch < MOD_ROWS and rows % n_ctx == 0 and n_ctx % GLA_CHUNK == 0
    tm = 768 if rows % 768 == 0 else n_ctx
    tm_small = 384 if rows % 384 == 0 else n_ctx
    tn_in, tn_merge, tn_out, tn_ffn_in, tn_ffn_out = 1024, 512, d, 512, 1024

    w_in_t = jnp.swapaxes(w_in, 1, 2)
    w_main = _win_prep_call(w_in_t)
    w_branch_b = w_branch.astype(BF16)
    w_out_b = w_out.astype(BF16)
    w_ffn_in_b = w_ffn_in.astype(BF16)
    w_ffn_out_b = w_ffn_out.astype(BF16)
    wd_pad = jnp.zeros((depth, 2, DEC_PAD, GLA_DK), F32)
    wd_pad = wd_pad.at[:, 0, :GLA_RANK].set(gla_w_decay[:, 0])
    wd_pad = wd_pad.at[:, 1, GLA_RANK:2 * GLA_RANK].set(gla_w_decay[:, 1])
    bd = gla_b_decay.reshape(depth, 2, 1, GLA_DK)
    cos_t, sin_t = _rope_tables(t_lat)

    c16 = jnp.zeros((MOD_ROWS, d), F32).at[:batch].set(c).at[ctx_row].set(c_ctx)
    mods4 = _mods_call(c16, w_mod, b_mod).reshape(depth, MOD_ROWS, 1, N_MOD * d)

    g_mix = norm_mix_g.reshape(depth, 1, d)
    g_ffn = norm_ffn_g.reshape(depth, 1, d)
    g_gla = gla_norm_g.reshape(depth, 1, GLA_HDV)
    g_q = q_norm_g.reshape(depth, 1, ATT_HD)
    g_k = k_norm_g.reshape(depth, 1, ATT_HD)
    cb3 = conv_b.reshape(depth, 1, LRU_W)

    h = jnp.concatenate([ctx, x], axis=1)
    for l in range(depth):
        u, dec = _normmod_call(h, g_mix, mods4, w_in_t, l, 0, 1, n_ctx, ctx_row)
        p = _linear_call(u, w_main, l, BF16, tm, tn_in, "in_proj")
        o_att = _attn_call(p, g_q, g_k, cos_t, sin_t, l, n_ctx)
        o_gla = _gla_call(p, dec, wd_pad, bd, g_gla, l, n_ctx)
        o_lru = _lru_call(p, conv_w, cb3, lru_w_a, lru_b_a, lru_w_i, lru_b_i, lru_lambda, l, n_ctx)
        merged = _merge_call(o_gla, o_att, o_lru, w_branch_b, p, b_merge, l, tm, tn_merge)
        h, u2 = _linear_res_call(merged, w_out_b, l, h, mods4, 2, n_ctx, ctx_row, tm_small, tn_out, "out_proj",
                                 norm=(g_ffn, l, 3, 4))
        a = _swiglu_call(u2, w_ffn_in_b, l, tm, tn_ffn_in)
        h = _linear_res_call(a, w_ffn_out_b, l, h, mods4, 5, n_ctx, ctx_row, tm_small, tn_ffn_out, "ffn_out")
    return _final_norm_call(h, final_norm_g.reshape(1, d), n_ctx)
```

```python
import functools

import jax
import jax.numpy as jnp
import numpy as np
from jax import lax
from jax.experimental import pallas as pl
from jax.experimental.pallas import tpu as pltpu

F32 = jnp.float32
BF16 = jnp.bfloat16

D_MODEL = 2048
GRID_W = 64
NORM_EPS = 1e-6
N_MOD = 6

GLA_HEADS = 4
GLA_DK = D_MODEL // 2
GLA_DV = D_MODEL
GLA_HDK = GLA_DK // GLA_HEADS
GLA_HDV = GLA_DV // GLA_HEADS
GLA_RANK = 16
GLA_GATE_NORM = 16.0
GLA_CHUNK = 64
GLA_HEADS_PER_STEP = 2

ATT_HEADS = 16
ATT_KV_HEADS = 4
ATT_HD = 128
ATT_GROUP = ATT_HEADS // ATT_KV_HEADS
ATT_Q = ATT_HEADS * ATT_HD
ATT_KV = ATT_KV_HEADS * ATT_HD
ROPE_THETA = 10000.0
ROPE_AXIS_PAIRS = ATT_HD // 4

LRU_W = D_MODEL
LRU_BLOCKS = 16
LRU_BS = LRU_W // LRU_BLOCKS
LRU_C = 8.0
CONV_W = 4
CONV_LEFT = 2

N_BRANCH = 3
FFN_HIDDEN = -(-8 * D_MODEL // (3 * 256)) * 256

COL_GQ = 0
COL_GK = COL_GQ + GLA_DK
COL_GV = COL_GK + GLA_DK
COL_GR = COL_GV + GLA_DV
COL_AQ = COL_GR + GLA_DV
COL_AK = COL_AQ + ATT_Q
COL_AV = COL_AK + ATT_KV
COL_LX = COL_AV + ATT_KV
COL_LY = COL_LX + LRU_W
COL_GT = COL_LY + LRU_W
N_MAIN = COL_GT + N_BRANCH * D_MODEL
DEC_ORIG = 2 * GLA_DK + 2 * GLA_DV
DEC_PAD = 128

V7X_VMEM_LIMIT_BYTES = 56 * 1024 * 1024
MOD_ROWS = 16


def _cparams(n_axes, vmem=V7X_VMEM_LIMIT_BYTES):
    return pltpu.CompilerParams(dimension_semantics=("arbitrary",) * n_axes, vmem_limit_bytes=vmem)


def _sigmoid(x):
    return 0.5 * jnp.tanh(0.5 * x) + 0.5


def _mods_kernel(c_ref, w_ref, b_ref, o_ref):
    c = c_ref[...]
    sc = (c * _sigmoid(c)).astype(BF16)
    o_ref[...] = jnp.dot(sc, w_ref[...].astype(BF16), preferred_element_type=F32) + b_ref[...]


def _mods_call(c16, w_mod, b_mod):
    depth, d, n = w_mod.shape
    tn = 1024
    return pl.pallas_call(
        _mods_kernel,
        grid=(depth, n // tn),
        in_specs=[
            pl.BlockSpec((MOD_ROWS, d), lambda l, j: (0, 0)),
            pl.BlockSpec((None, d, tn), lambda l, j: (l, 0, j)),
            pl.BlockSpec((None, 1, tn), lambda l, j: (l, 0, j)),
        ],
        out_specs=pl.BlockSpec((None, MOD_ROWS, tn), lambda l, j: (l, 0, j)),
        out_shape=jax.ShapeDtypeStruct((depth, MOD_ROWS, n), F32),
        compiler_params=_cparams(2),
        name="mods",
    )(c16, w_mod, b_mod.reshape(depth, 1, n))


def _win_prep_kernel(a_ref, b_ref, o_ref, *, n_plain, shift):
    j = pl.program_id(1)

    @pl.when(j < n_plain)
    def _before_decay_columns():
        o_ref[...] = a_ref[...].T.astype(o_ref.dtype)

    @pl.when(j >= n_plain)
    def _after_decay_columns():
        cols = jnp.concatenate([a_ref[shift:, :], b_ref[...]], axis=0)
        o_ref[...] = cols.T.astype(o_ref.dtype)


def _win_prep_call(w_in_t):
    depth, _, d = w_in_t.shape
    tn, shift = 512, 2 * GLA_RANK
    assert DEC_ORIG % tn == 0 and N_MAIN % tn == 0 and tn % shift == 0
    return pl.pallas_call(
        functools.partial(_win_prep_kernel, n_plain=DEC_ORIG // tn, shift=shift),
        grid=(depth, N_MAIN // tn),
        in_specs=[
            pl.BlockSpec((None, tn, d), lambda l, j: (l, j, 0)),
            pl.BlockSpec((None, shift, d), lambda l, j: (l, (j + 1) * (tn // shift), 0)),
        ],
        out_specs=pl.BlockSpec((None, d, tn), lambda l, j: (l, 0, j)),
        out_shape=jax.ShapeDtypeStruct((depth, d, N_MAIN), BF16),
        compiler_params=_cparams(2),
        name="w_in_prep",
    )(w_in_t, w_in_t)


def _normmod_kernel(h_ref, g_ref, sh_ref, sc_ref, wt_ref, o_ref, dec_ref):
    x = h_ref[...]
    var = jnp.mean(x * x, axis=-1, keepdims=True)
    y = x * lax.rsqrt(var + NORM_EPS) * g_ref[...]
    u = (y * (1.0 + sc_ref[...]) + sh_ref[...]).astype(o_ref.dtype)
    o_ref[...] = u
    dec_ref[...] = lax.dot_general(u, wt_ref[...].astype(BF16), (((1,), (1,)), ((), ())),
                                   preferred_element_type=F32)


def _normmod_call(h, g, mods4, w_in_t, layer, k_shift, k_scale, n_ctx, ctx_row):
    b, r, d = h.shape
    tr = n_ctx

    def mod_map(k):
        return lambda bi, i: (layer, jnp.where(i == 0, ctx_row, bi), 0, k)

    return pl.pallas_call(
        _normmod_kernel,
        grid=(b, r // tr),
        in_specs=[
            pl.BlockSpec((None, tr, d), lambda bi, i: (bi, i, 0)),
            pl.BlockSpec((None, 1, d), lambda bi, i: (layer, 0, 0)),
            pl.BlockSpec((None, None, 1, d), mod_map(k_shift)),
            pl.BlockSpec((None, None, 1, d), mod_map(k_scale)),
            pl.BlockSpec((None, DEC_PAD, d), lambda bi, i: (layer, DEC_ORIG // DEC_PAD, 0)),
        ],
        out_specs=[pl.BlockSpec((None, tr, d), lambda bi, i: (bi, i, 0)),
                   pl.BlockSpec((None, tr, DEC_PAD), lambda bi, i: (bi, i, 0))],
        out_shape=[jax.ShapeDtypeStruct((b, r, d), BF16), jax.ShapeDtypeStruct((b, r, DEC_PAD), F32)],
        compiler_params=_cparams(2),
        name="normmod",
    )(h, g, mods4, mods4, w_in_t)


def _linear_kernel(x_ref, w_ref, o_ref):
    o_ref[...] = jnp.dot(x_ref[...], w_ref[...], preferred_element_type=F32).astype(o_ref.dtype)


def _linear_call(x, w3, layer, out_dtype, tm, tn, name):
    b, r, k = x.shape
    n = w3.shape[-1]
    return pl.pallas_call(
        _linear_kernel,
        grid=(n // tn, b, r // tm),
        in_specs=[
            pl.BlockSpec((None, tm, k), lambda j, bi, i: (bi, i, 0)),
            pl.BlockSpec((None, k, tn), lambda j, bi, i: (layer, 0, j)),
        ],
        out_specs=pl.BlockSpec((None, tm, tn), lambda j, bi, i: (bi, i, j)),
        out_shape=jax.ShapeDtypeStruct((b, r, n), out_dtype),
        compiler_params=_cparams(3),
        name=name,
    )(x, w3)


def _linear_res_kernel(*refs, tm, n_ctx, with_norm):
    x_ref, w_ref, h_ref, gc_ref, gl_ref = refs[:5]
    if with_norm:
        ng_ref, shc_ref, shl_ref, scc_ref, scl_ref, o_ref, u_ref = refs[5:]
    else:
        (o_ref,) = refs[5:]
    i = pl.program_id(2)
    y = jnp.dot(x_ref[...], w_ref[...], preferred_element_type=F32)
    is_ctx = (i * tm + lax.broadcasted_iota(jnp.int32, (tm, 1), 0)) < n_ctx
    h_new = h_ref[...] + jnp.where(is_ctx, gc_ref[...], gl_ref[...]) * y
    o_ref[...] = h_new
    if with_norm:
        var = jnp.mean(h_new * h_new, axis=-1, keepdims=True)
        yn = h_new * lax.rsqrt(var + NORM_EPS) * ng_ref[...]
        shift = jnp.where(is_ctx, shc_ref[...], shl_ref[...])
        scale = jnp.where(is_ctx, scc_ref[...], scl_ref[...])
        u_ref[...] = (yn * (1.0 + scale) + shift).astype(u_ref.dtype)


def _linear_res_call(x, w3, layer, h, mods4, k_gate, n_ctx, ctx_row, tm, tn, name, norm=None):
    b, r, k = x.shape
    n = w3.shape[-1]
    ngate = n // tn

    def mod_spec(lyr, row_of, kk):
        return pl.BlockSpec((None, None, 1, tn), lambda j, bi, i: (lyr, row_of(bi), 0, kk * ngate + j))

    ctx_of = lambda bi: ctx_row
    lat_of = lambda bi: bi
    w_mode = dict(pipeline_mode=pl.Buffered(1)) if tn == n else {}
    in_specs = [
        pl.BlockSpec((None, tm, k), lambda j, bi, i: (bi, i, 0)),
        pl.BlockSpec((None, k, tn), lambda j, bi, i: (layer, 0, j), **w_mode),
        pl.BlockSpec((None, tm, tn), lambda j, bi, i: (bi, i, j)),
        mod_spec(layer, ctx_of, k_gate),
        mod_spec(layer, lat_of, k_gate),
    ]
    args = [x, w3, h, mods4, mods4]
    out_specs = [pl.BlockSpec((None, tm, tn), lambda j, bi, i: (bi, i, j))]
    out_shape = [jax.ShapeDtypeStruct((b, r, n), F32)]
    if norm is not None:
        gains, n_layer, k_shift, k_scale = norm
        assert tn == n
        in_specs += [
            pl.BlockSpec((None, 1, n), lambda j, bi, i: (n_layer, 0, 0)),
            mod_spec(n_layer, ctx_of, k_shift), mod_spec(n_layer, lat_of, k_shift),
            mod_spec(n_layer, ctx_of, k_scale), mod_spec(n_layer, lat_of, k_scale),
        ]
        args += [gains, mods4, mods4, mods4, mods4]
        out_specs.append(pl.BlockSpec((None, tm, tn), lambda j, bi, i: (bi, i, j)))
        out_shape.append(jax.ShapeDtypeStruct((b, r, n), BF16))
    res = pl.pallas_call(
        functools.partial(_linear_res_kernel, tm=tm, n_ctx=n_ctx, with_norm=norm is not None),
        grid=(n // tn, b, r // tm),
        in_specs=in_specs,
        out_specs=out_specs,
        out_shape=out_shape,
        compiler_params=_cparams(3),
        name=name,
    )(*args)
    return res if norm is not None else res[0]


def _swiglu_kernel(x_ref, wg_ref, wu_ref, o_ref):
    x = x_ref[...]
    g = jnp.dot(x, wg_ref[...], preferred_element_type=F32)
    up = jnp.dot(x, wu_ref[...], preferred_element_type=F32)
    o_ref[...] = (g * _sigmoid(g) * up).astype(o_ref.dtype)


def _swiglu_call(x, w3, layer, tm, tn):
    b, r, k = x.shape
    hid = w3.shape[-1] // 2
    nb = hid // tn
    return pl.pallas_call(
        _swiglu_kernel,
        grid=(nb, b, r // tm),
        in_specs=[
            pl.BlockSpec((None, tm, k), lambda j, bi, i: (bi, i, 0)),
            pl.BlockSpec((None, k, tn), lambda j, bi, i: (layer, 0, j)),
            pl.BlockSpec((None, k, tn), lambda j, bi, i: (layer, 0, nb + j)),
        ],
        out_specs=pl.BlockSpec((None, tm, tn), lambda j, bi, i: (bi, i, j)),
        out_shape=jax.ShapeDtypeStruct((b, r, hid), BF16),
        compiler_params=_cparams(3),
        name="ffn_in",
    )(x, w3, w3)


def _norm_rope(x, gain, cs, sn):
    var = jnp.mean(x * x, axis=-1, keepdims=True)
    y = x * lax.rsqrt(var + NORM_EPS) * gain
    if cs is None:
        return y
    lane = lax.broadcasted_iota(jnp.int32, y.shape, 1)
    nxt = pltpu.roll(y, ATT_HD - 1, 1)
    prv = pltpu.roll(y, 1, 1)
    partner = jnp.where(lane % 2 == 0, nxt, prv)
    return y * cs + partner * sn


def _attn_kernel(q_ref, k_ref, v_ref, qg_ref, kg_ref, cos_ref, sin_ref, o_ref, kt_s, vx_s, *, n_ctx, tq):
    qi = pl.program_id(2)
    rows = k_ref.shape[0]
    scale = ATT_HD ** -0.5 * float(np.log2(np.e))

    @pl.when(qi == 0)
    def _prepare_kv():
        kg = kg_ref[...]
        for c in range(rows // tq):
            kc = k_ref[c * tq:(c + 1) * tq, :].astype(F32)
            if c == 0:
                kn = _norm_rope(kc, kg, None, None)
            else:
                kn = _norm_rope(kc, kg, cos_ref[(c - 1) * tq:c * tq, :], sin_ref[(c - 1) * tq:c * tq, :])
            kt_s[:, c * tq:(c + 1) * tq] = kn.T.astype(BF16)
        lane = lax.broadcasted_iota(jnp.int32, (rows, ATT_HD), 1)
        vx_s[:, :ATT_HD] = v_ref[...]
        vx_s[:, ATT_HD:] = jnp.where(lane == 0, 1.0, 0.0).astype(BF16)

    def attend(cs, sn, n_keys):
        qg = qg_ref[...]
        scores = []
        for g in range(ATT_GROUP):
            q = _norm_rope(q_ref[:, g * ATT_HD:(g + 1) * ATT_HD].astype(F32), qg, cs, sn) * scale
            scores.append(jnp.dot(q.astype(BF16), kt_s[:, :n_keys], preferred_element_type=F32))
        for g in range(ATT_GROUP):
            s = scores[g]
            p = jnp.exp2(s - jnp.max(s, axis=-1, keepdims=True))
            ox = jnp.dot(p.astype(BF16), vx_s[:n_keys, :], preferred_element_type=F32)
            o = ox[:, :ATT_HD] / ox[:, ATT_HD:ATT_HD + 1]
            o_ref[:, g * ATT_HD:(g + 1) * ATT_HD] = o.astype(o_ref.dtype)

    @pl.when(qi == 0)
    def _context_queries():
        attend(None, None, n_ctx)

    @pl.when(qi > 0)
    def _latent_queries():
        off = pl.multiple_of((qi - 1) * tq, tq)
        attend(cos_ref[pl.ds(off, tq), :], sin_ref[pl.ds(off, tq), :], rows)


def _attn_call(p, q_g, k_g, cos_t, sin_t, layer, n_ctx):
    b, r, _ = p.shape
    tq = n_ctx
    t = r - n_ctx
    qw = ATT_GROUP * ATT_HD
    return pl.pallas_call(
        functools.partial(_attn_kernel, n_ctx=n_ctx, tq=tq),
        grid=(b, ATT_KV_HEADS, r // tq),
        in_specs=[
            pl.BlockSpec((None, tq, qw), lambda bi, kv, qi: (bi, qi, COL_AQ // qw + kv)),
            pl.BlockSpec((None, r, ATT_HD), lambda bi, kv, qi: (bi, 0, COL_AK // ATT_HD + kv)),
            pl.BlockSpec((None, r, ATT_HD), lambda bi, kv, qi: (bi, 0, COL_AV // ATT_HD + kv)),
            pl.BlockSpec((None, 1, ATT_HD), lambda bi, kv, qi: (layer, 0, 0)),
            pl.BlockSpec((None, 1, ATT_HD), lambda bi, kv, qi: (layer, 0, 0)),
            pl.BlockSpec((t, ATT_HD), lambda bi, kv, qi: (0, 0)),
            pl.BlockSpec((t, ATT_HD), lambda bi, kv, qi: (0, 0)),
        ],
        out_specs=pl.BlockSpec((None, tq, qw), lambda bi, kv, qi: (bi, qi, kv)),
        out_shape=jax.ShapeDtypeStruct((b, r, ATT_Q), BF16),
        scratch_shapes=[pltpu.VMEM((ATT_HD, r), BF16), pltpu.VMEM((r, 2 * ATT_HD), BF16)],
        compiler_params=_cparams(3),
        name="gqa",
    )(p, p, p, q_g, k_g, cos_t, sin_t)


def _gla_direction(q, k, v, dec, wd, bd, vis, st_s, reverse, n_chunks):
    c_len = GLA_CHUNK

    tri = vis > 0.5
    tri_b = vis.astype(BF16)
    nt = (((1,), (1,)), ((), ()))
    tn = (((0,), (0,)), ((), ()))

    x = jnp.dot(dec, wd, preferred_element_type=F32) + bd
    g = (jnp.minimum(x, 0.0) - jnp.log(1.0 + jnp.exp(-jnp.abs(x)))) * (1.0 / GLA_GATE_NORM)
    g_hi = g.astype(BF16)
    g_lo = (g - g_hi.astype(F32)).astype(BF16)
    bcum = jnp.dot(tri_b, g_hi, preferred_element_type=F32) + jnp.dot(tri_b, g_lo, preferred_element_type=F32)
    q = q.astype(F32) * (GLA_HDK ** -0.5)
    k = k.astype(F32)
    q_dec = (q * jnp.exp(bcum)).astype(BF16)
    k_inv = (k * jnp.exp(-bcum)).astype(BF16)
    att = lax.dot_general(q_dec, k_inv, nt, preferred_element_type=F32)
    att = jnp.where(tri, att, 0.0).astype(BF16)
    o_intra = jnp.dot(att, v, preferred_element_type=F32)
    b_last, k_dec = [], []
    for c in range(n_chunks):
        sl = slice(c * c_len, (c + 1) * c_len)
        end = c * c_len if reverse else (c + 1) * c_len - 1
        b_last.append(bcum[end:end + 1, :])
        k_dec.append((k[sl, :] * jnp.exp(b_last[c] - bcum[sl, :])).astype(BF16))

    outs = [None] * n_chunks
    chunk_order = range(n_chunks - 1, -1, -1) if reverse else range(n_chunks)
    st = st_s[...]
    for c in chunk_order:
        sl = slice(c * c_len, (c + 1) * c_len)
        outs[c] = o_intra[sl, :] + lax.dot_general(q_dec[sl, :], st.astype(BF16), nt, preferred_element_type=F32)
        st = st * jnp.exp(b_last[c]) + lax.dot_general(v[sl, :], k_dec[c], tn, preferred_element_type=F32)
    st_s[...] = st
    return jnp.concatenate(outs, axis=0)


def _gla_kernel(qf_ref, kf_ref, vf_ref, decf_ref, qb_ref, kb_ref, vb_ref, decb_ref,
                wdf_ref, bdf_ref, wdb_ref, bdb_ref, visf_ref, visb_ref, r_ref, ng_ref, o_ref,
                stf_s, stb_s, of_s, ob_s, *, n_chunks, nblk):
    t = pl.program_id(2)
    tb = n_chunks * GLA_CHUNK

    @pl.when(t == 0)
    def _zero_states():
        stf_s[...] = jnp.zeros_like(stf_s)
        stb_s[...] = jnp.zeros_like(stb_s)

    blk_b = jnp.where(t == 0, 0, nblk - t)
    rows_f = pl.ds(pl.multiple_of(t * tb, tb), tb)
    rows_b = pl.ds(pl.multiple_of(blk_b * tb, tb), tb)
    decf = decf_ref[...].astype(BF16)
    decb = decb_ref[...].astype(BF16)
    visf = visf_ref[...]
    visb = visb_ref[...]
    for hh in range(GLA_HEADS_PER_STEP):
        kcol = slice(hh * GLA_HDK, (hh + 1) * GLA_HDK)
        vcol = slice(hh * GLA_HDV, (hh + 1) * GLA_HDV)
        of_s[rows_f, vcol] = _gla_direction(
            qf_ref[:, kcol], kf_ref[:, kcol], vf_ref[:, vcol], decf, wdf_ref[:, kcol].astype(BF16),
            bdf_ref[:, kcol], visf, stf_s.at[hh], False, n_chunks)
        ob_s[rows_b, vcol] = _gla_direction(
            qb_ref[:, kcol], kb_ref[:, kcol], vb_ref[:, vcol], decb, wdb_ref[:, kcol].astype(BF16),
            bdb_ref[:, kcol], visb, stb_s.at[hh], True, n_chunks)

    @pl.when(t == nblk - 1)
    def _combine_norm_gate():
        for c in range(nblk):
            sl = slice(c * tb, (c + 1) * tb)
            for hh in range(GLA_HEADS_PER_STEP):
                vcol = slice(hh * GLA_HDV, (hh + 1) * GLA_HDV)
                o = of_s[sl, vcol] + ob_s[sl, vcol]
                var = jnp.mean(o * o, axis=-1, keepdims=True)
                y = o * lax.rsqrt(var + NORM_EPS) * ng_ref[...]
                r = r_ref[sl, vcol].astype(F32)
                o_ref[sl, vcol] = (y * (r * _sigmoid(r))).astype(o_ref.dtype)


def _gla_call(p, dec, wd_pad, bd, norm_g, layer, n_ctx):
    b, r, _ = p.shape
    tb = n_ctx
    nblk = r // tb

    def blk(t, direction):
        if direction == 0:
            return t
        return jnp.where(t == 0, 0, nblk - t)

    kw = GLA_HEADS_PER_STEP * GLA_HDK
    vw = GLA_HEADS_PER_STEP * GLA_HDV

    def stream_specs(direction):
        return [
            pl.BlockSpec((None, tb, kw), lambda bi, h, t: (bi, blk(t, direction), COL_GQ // kw + h)),
            pl.BlockSpec((None, tb, kw), lambda bi, h, t: (bi, blk(t, direction), COL_GK // kw + h)),
            pl.BlockSpec((None, tb, vw), lambda bi, h, t: (bi, blk(t, direction), COL_GV // vw + h)),
            pl.BlockSpec((None, tb, DEC_PAD), lambda bi, h, t: (bi, blk(t, direction), 0)),
        ]

    def decay_specs(direction):
        return [
            pl.BlockSpec((None, None, DEC_PAD, kw), lambda bi, h, t: (layer, direction, 0, h)),
            pl.BlockSpec((None, None, 1, kw), lambda bi, h, t: (layer, direction, 0, h)),
        ]

    idx = np.arange(tb)
    same_chunk = (idx[:, None] // GLA_CHUNK) == (idx[None, :] // GLA_CHUNK)
    vis = jnp.asarray(np.stack([same_chunk & (idx[:, None] >= idx[None, :]),
                                same_chunk & (idx[None, :] >= idx[:, None])]).astype(np.float32))

    return pl.pallas_call(
        functools.partial(_gla_kernel, n_chunks=tb // GLA_CHUNK, nblk=nblk),
        grid=(b, GLA_HEADS // GLA_HEADS_PER_STEP, nblk),
        in_specs=stream_specs(0) + stream_specs(1) + decay_specs(0) + decay_specs(1) + [
            pl.BlockSpec((None, tb, tb), lambda bi, h, t: (0, 0, 0)),
            pl.BlockSpec((None, tb, tb), lambda bi, h, t: (1, 0, 0)),
            pl.BlockSpec((None, r, vw), lambda bi, h, t: (bi, 0, COL_GR // vw + h)),
            pl.BlockSpec((None, 1, GLA_HDV), lambda bi, h, t: (layer, 0, 0)),
        ],
        out_specs=pl.BlockSpec((None, r, vw), lambda bi, h, t: (bi, 0, h)),
        out_shape=jax.ShapeDtypeStruct((b, r, GLA_DV), BF16),
        scratch_shapes=([pltpu.VMEM((GLA_HEADS_PER_STEP, GLA_HDV, GLA_HDK), F32)] * 2
                        + [pltpu.VMEM((r, vw), F32)] * 2),
        compiler_params=_cparams(3),
        name="gla",
    )(p, p, p, dec, p, p, p, dec, wd_pad, bd, wd_pad, bd, vis, vis, p, norm_g)


LRU_PAD = 8


def _lru_kernel(x_ref, y_ref, cw_ref, cb_ref, wa_ref, ba_ref, wi_ref, bi_ref, lam_ref, o_ref,
                xp_s, af_s, bf_s, ab_s, bb_s, *, n_ctx, tc, tr):
    rows = x_ref.shape[0]
    n_lat = rows - n_ctx
    sub = 8
    ctx0 = LRU_PAD
    lat0 = 2 * LRU_PAD + n_ctx
    zpad = jnp.zeros((LRU_PAD, tc), F32)
    xp_s[0:LRU_PAD, :] = zpad
    xp_s[ctx0 + n_ctx:lat0, :] = zpad
    xp_s[lat0 + n_lat:lat0 + n_lat + LRU_PAD, :] = zpad
    xp_s[ctx0:ctx0 + n_ctx, :] = x_ref[0:n_ctx, :].astype(F32)
    xp_s[lat0:lat0 + n_lat, :] = x_ref[n_ctx:rows, :].astype(F32)

    cw = cw_ref[...]
    cb = cb_ref[...]
    lam = lam_ref[...]
    neg_sp = -(jnp.maximum(-lam, 0.0) + jnp.log1p(jnp.exp(-jnp.abs(lam))))
    log2_a_half = (0.5 * LRU_C * float(np.log2(np.e))) * neg_sp
    row_id = lax.broadcasted_iota(jnp.int32, (tr, 1), 0)
    n_kb = tc // LRU_BS
    wa_h = [[(0.5 * wa_ref[d, kb]).astype(BF16) for kb in range(n_kb)] for d in range(2)]
    wi_h = [[(0.5 * wi_ref[d, kb]).astype(BF16) for kb in range(n_kb)] for d in range(2)]
    ba_h = 0.5 * ba_ref[...]
    bi_h = 0.5 * bi_ref[...]

    for c in range(rows // tr):
        r0 = c * tr
        p0 = (ctx0 if r0 < n_ctx else lat0 - n_ctx) + r0
        n_win = tr + 2 * sub
        win = xp_s[p0 - sub:p0 + tr + sub, :]
        xc = cb + win[sub:sub + tr, :] * cw[CONV_LEFT:CONV_LEFT + 1, :]
        for j in range(CONV_W):
            off = j - CONV_LEFT
            if off != 0:
                xc = xc + pltpu.roll(win, (-off) % n_win, 0)[sub:sub + tr, :] * cw[j:j + 1, :]
        xh = 0.5 * xc
        xb = [xc[:, kb * LRU_BS:(kb + 1) * LRU_BS].astype(BF16) for kb in range(n_kb)]
        for d, (a_s, b_s) in enumerate(((af_s, bf_s), (ab_s, bb_s))):
            za = jnp.concatenate([jnp.dot(xb[kb], wa_h[d][kb], preferred_element_type=F32)
                                  for kb in range(n_kb)], axis=1)
            zi = jnp.concatenate([jnp.dot(xb[kb], wi_h[d][kb], preferred_element_type=F32)
                                  for kb in range(n_kb)], axis=1)
            ta = jnp.tanh(za + ba_h[d:d + 1, :])
            ti = jnp.tanh(zi + bi_h[d:d + 1, :])
            la_h = log2_a_half[d:d + 1, :]
            a = jnp.exp2(la_h + la_h * ta)
            q = 1.0 - a * a
            mult = jnp.where(q > 0.0, q * lax.rsqrt(q), 0.0)
            first = 0 if d == 0 else n_ctx - 1
            if r0 <= first < r0 + tr:
                mult = jnp.where(row_id == first - r0, 1.0, mult)
            a_s[r0:r0 + tr, :] = a
            b_s[r0:r0 + tr, :] = (mult * xh) * (1.0 + ti)

    row8 = lax.broadcasted_iota(jnp.int32, (sub, tc), 0)

    def tile_maps(a, b, reverse):
        for dlt in (1, 2, 4):
            shift = sub - dlt if reverse else dlt
            valid = (row8 < sub - dlt) if reverse else (row8 >= dlt)
            a_sh = jnp.where(valid, pltpu.roll(a, shift, 0), 1.0)
            b_sh = jnp.where(valid, pltpu.roll(b, shift, 0), 0.0)
            b = b + a * b_sh
            a = a * a_sh
        return a, b

    def scan_tile(tile_f, tile_b, carry):
        hf, hb = carry
        rf = pl.ds(pl.multiple_of(tile_f * sub, sub), sub)
        rb = pl.ds(pl.multiple_of(tile_b * sub, sub), sub)
        a_f, b_f = tile_maps(af_s[rf, :], bf_s[rf, :], False)
        a_b, b_b = tile_maps(ab_s[rb, :], bb_s[rb, :], True)
        h_f = b_f + a_f * hf
        h_b = b_b + a_b * hb
        bf_s[rf, :] = h_f
        bb_s[rb, :] = h_b
        return h_f[sub - 1:sub, :], h_b[0:1, :]

    zero = jnp.zeros((1, tc), F32)
    ctx_tiles = n_ctx // sub
    all_tiles = rows // sub
    carry = lax.fori_loop(0, ctx_tiles, lambda i, cr: scan_tile(i, ctx_tiles - 1 - i, cr), (zero, zero),
                          unroll=2)
    lax.fori_loop(ctx_tiles, all_tiles, lambda i, cr: scan_tile(i, all_tiles - 1 + ctx_tiles - i, cr), carry,
                  unroll=2)

    k0 = 0.7978845608028654
    for c in range(rows // tr):
        sl = slice(c * tr, (c + 1) * tr)
        y = y_ref[sl, :].astype(F32)
        gelu = 0.5 * y * (1.0 + jnp.tanh(k0 * (y + 0.044715 * (y * y * y))))
        o_ref[sl, :] = ((bf_s[sl, :] + bb_s[sl, :]) * gelu).astype(o_ref.dtype)


def _lru_call(p, conv_w, conv_b, w_a, b_a, w_i, b_i, lam, layer, n_ctx):
    b, r, _ = p.shape
    tc = 256
    nb = tc // LRU_BS
    chan = lambda bi, j: (layer, 0, j)
    return pl.pallas_call(
        functools.partial(_lru_kernel, n_ctx=n_ctx, tc=tc, tr=n_ctx),
        grid=(b, LRU_W // tc),
        in_specs=[
            pl.BlockSpec((None, r, tc), lambda bi, j: (bi, 0, COL_LX // tc + j)),
            pl.BlockSpec((None, r, tc), lambda bi, j: (bi, 0, COL_LY // tc + j)),
            pl.BlockSpec((None, CONV_W, tc), chan),
            pl.BlockSpec((None, 1, tc), chan),
            pl.BlockSpec((None, 2, nb, LRU_BS, LRU_BS), lambda bi, j: (layer, 0, j, 0, 0)),
            pl.BlockSpec((None, 2, tc), chan),
            pl.BlockSpec((None, 2, nb, LRU_BS, LRU_BS), lambda bi, j: (layer, 0, j, 0, 0)),
            pl.BlockSpec((None, 2, tc), chan),
            pl.BlockSpec((None, 2, tc), chan),
        ],
        out_specs=pl.BlockSpec((None, r, tc), lambda bi, j: (bi, 0, j)),
        out_shape=jax.ShapeDtypeStruct((b, r, LRU_W), BF16),
        scratch_shapes=[pltpu.VMEM((r + 3 * LRU_PAD, tc), F32)] + [pltpu.VMEM((r, tc), F32)] * 4,
        compiler_params=_cparams(2),
        name="rglru",
    )(p, p, conv_w, conv_b, w_a, b_a, w_i, b_i, lam)


def _merge_kernel(xa_ref, xb_ref, xc_ref, wa_ref, wb_ref, wc_ref, ga_ref, gb_ref, gc_ref, bm_ref, o_ref):
    acc = None
    for n, (x_ref, w_ref, g_ref) in enumerate(((xa_ref, wa_ref, ga_ref), (xb_ref, wb_ref, gb_ref),
                                               (xc_ref, wc_ref, gc_ref))):
        proj = jnp.dot(x_ref[...], w_ref[...], preferred_element_type=F32)
        term = _sigmoid(g_ref[...].astype(F32) + bm_ref[n:n + 1, :]) * proj
        acc = term if acc is None else acc + term
    o_ref[...] = acc.astype(o_ref.dtype)


def _merge_call(o_gla, o_att, o_lru, wbr, p, b_merge, layer, tm, tn):
    b, r, d = o_gla.shape
    ncol = d // tn
    x_spec = pl.BlockSpec((None, tm, d), lambda j, bi, i: (bi, i, 0))

    def w_spec(n):
        return pl.BlockSpec((None, None, d, tn), lambda j, bi, i: (layer, n, 0, j))

    def g_spec(n):
        return pl.BlockSpec((None, tm, tn), lambda j, bi, i: (bi, i, COL_GT // tn + n * ncol + j))

    return pl.pallas_call(
        _merge_kernel,
        grid=(ncol, b, r // tm),
        in_specs=[x_spec, x_spec, x_spec, w_spec(0), w_spec(1), w_spec(2), g_spec(0), g_spec(1), g_spec(2),
                  pl.BlockSpec((None, N_BRANCH, tn), lambda j, bi, i: (layer, 0, j))],
        out_specs=pl.BlockSpec((None, tm, tn), lambda j, bi, i: (bi, i, j)),
        out_shape=jax.ShapeDtypeStruct((b, r, d), BF16),
        compiler_params=_cparams(3),
        name="merge",
    )(o_gla, o_att, o_lru, wbr, wbr, wbr, p, p, p, b_merge)


def _final_norm_kernel(h_ref, g_ref, o_ref):
    x = h_ref[...]
    var = jnp.mean(x * x, axis=-1, keepdims=True)
    o_ref[...] = x * lax.rsqrt(var + NORM_EPS) * g_ref[...]


def _final_norm_call(h, g, n_ctx):
    b, r, d = h.shape
    tr = n_ctx
    return pl.pallas_call(
        _final_norm_kernel,
        grid=(b, (r - n_ctx) // tr),
        in_specs=[
            pl.BlockSpec((None, tr, d), lambda bi, i: (bi, i + 1, 0)),
            pl.BlockSpec((1, d), lambda bi, i: (0, 0)),
        ],
        out_specs=pl.BlockSpec((None, tr, d), lambda bi, i: (bi, i, 0)),
        out_shape=jax.ShapeDtypeStruct((b, r - n_ctx, d), F32),
        compiler_params=_cparams(2),
        name="final_norm",
    )(h, g)


def _rope_tables(n_tokens):
    rows = n_tokens // GRID_W
    row = jnp.repeat(jnp.arange(rows, dtype=F32), GRID_W)
    col = jnp.tile(jnp.arange(GRID_W, dtype=F32), rows)
    inv = ROPE_THETA ** (-jnp.arange(ROPE_AXIS_PAIRS, dtype=F32) / ROPE_AXIS_PAIRS)
    ang = jnp.concatenate([row[:, None] * inv, col[:, None] * inv], axis=-1)
    cos, sin = jnp.cos(ang), jnp.sin(ang)
    cos_t = jnp.repeat(cos, 2, axis=-1)
    sin_t = jnp.stack([-sin, sin], axis=-1).reshape(n_tokens, ATT_HD)
    return cos_t, sin_t


def kernel(x, c, ctx, c_ctx, w_mod, b_mod, norm_mix_g, norm_ffn_g, w_in, gla_w_decay, gla_b_decay, gla_norm_g, q_norm_g, k_norm_g, conv_w, conv_b, lru_w_a, lru_b_a, lru_w_i, lru_b_i, lru_lambda, b_merge, w_branch, w_out, w_ffn_in, w_ffn_out, final_norm_g):
    depth = w_in.shape[0]
    batch, t_lat, d = x.shape
    n_ctx = ctx.shape[1]
    rows = n_ctx + t_lat
    ctx_row = batch
    assert batch < MOD_ROWS and rows % n_ctx == 0 and n_ctx % GLA_CHUNK == 0
    tm = 768 if rows % 768 == 0 else n_ctx
    tm_big = 1152 if rows % 1152 == 0 else tm
    tm_small = 384 if rows % 384 == 0 else n_ctx
    tn_in, tn_merge, tn_out, tn_ffn_in, tn_ffn_out = 1024, 512, d, 512, 1024

    w_in_t = jnp.swapaxes(w_in, 1, 2)
    w_main = _win_prep_call(w_in_t)
    w_branch_b = w_branch.astype(BF16)
    w_out_b = w_out.astype(BF16)
    w_ffn_in_b = w_ffn_in.astype(BF16)
    w_ffn_out_b = w_ffn_out.astype(BF16)
    wd_pad = jnp.zeros((depth, 2, DEC_PAD, GLA_DK), F32)
    wd_pad = wd_pad.at[:, 0, :GLA_RANK].set(gla_w_decay[:, 0])
    wd_pad = wd_pad.at[:, 1, GLA_RANK:2 * GLA_RANK].set(gla_w_decay[:, 1])
    bd = gla_b_decay.reshape(depth, 2, 1, GLA_DK)
    cos_t, sin_t = _rope_tables(t_lat)

    c16 = jnp.zeros((MOD_ROWS, d), F32).at[:batch].set(c).at[ctx_row].set(c_ctx)
    mods4 = _mods_call(c16, w_mod, b_mod).reshape(depth, MOD_ROWS, 1, N_MOD * d)

    g_mix = norm_mix_g.reshape(depth, 1, d)
    g_ffn = norm_ffn_g.reshape(depth, 1, d)
    g_gla = gla_norm_g.reshape(depth, 1, GLA_HDV)
    g_q = q_norm_g.reshape(depth, 1, ATT_HD)
    g_k = k_norm_g.reshape(depth, 1, ATT_HD)
    cb3 = conv_b.reshape(depth, 1, LRU_W)

    h = jnp.concatenate([ctx, x], axis=1)
    for l in range(depth):
        u, dec = _normmod_call(h, g_mix, mods4, w_in_t, l, 0, 1, n_ctx, ctx_row)
        p = _linear_call(u, w_main, l, BF16, tm_big, tn_in, "in_proj")
        o_att = _attn_call(p, g_q, g_k, cos_t, sin_t, l, n_ctx)
        o_gla = _gla_call(p, dec, wd_pad, bd, g_gla, l, n_ctx)
        o_lru = _lru_call(p, conv_w, cb3, lru_w_a, lru_b_a, lru_w_i, lru_b_i, lru_lambda, l, n_ctx)
        merged = _merge_call(o_gla, o_att, o_lru, w_branch_b, p, b_merge, l, tm, tn_merge)
        h, u2 = _linear_res_call(merged, w_out_b, l, h, mods4, 2, n_ctx, ctx_row, tm, tn_out, "out_proj",
                                 norm=(g_ffn, l, 3, 4))
        a = _swiglu_call(u2, w_ffn_in_b, l, tm_big, tn_ffn_in)
        h = _linear_res_call(a, w_ffn_out_b, l, h, mods4, 5, n_ctx, ctx_row, tm_small, tn_ffn_out, "ffn_out")
    return _final_norm_call(h, final_norm_g.reshape(1, d), n_ctx)
```

```python
import functools

import jax
import jax.numpy as jnp
import numpy as np
from jax import lax
from jax.experimental import pallas as pl
from jax.experimental.pallas import tpu as pltpu

F32 = jnp.float32
BF16 = jnp.bfloat16

D_MODEL = 2048
GRID_W = 64
NORM_EPS = 1e-6
N_MOD = 6

GLA_HEADS = 4
GLA_DK = D_MODEL // 2
GLA_DV = D_MODEL
GLA_HDK = GLA_DK // GLA_HEADS
GLA_HDV = GLA_DV // GLA_HEADS
GLA_RANK = 16
GLA_GATE_NORM = 16.0
GLA_CHUNK = 64
GLA_HEADS_PER_STEP = 2

ATT_HEADS = 16
ATT_KV_HEADS = 4
ATT_HD = 128
ATT_GROUP = ATT_HEADS // ATT_KV_HEADS
ATT_Q = ATT_HEADS * ATT_HD
ATT_KV = ATT_KV_HEADS * ATT_HD
ROPE_THETA = 10000.0
ROPE_AXIS_PAIRS = ATT_HD // 4

LRU_W = D_MODEL
LRU_BLOCKS = 16
LRU_BS = LRU_W // LRU_BLOCKS
LRU_C = 8.0
CONV_W = 4
CONV_LEFT = 2

N_BRANCH = 3
FFN_HIDDEN = -(-8 * D_MODEL // (3 * 256)) * 256

COL_GQ = 0
COL_GK = COL_GQ + GLA_DK
COL_GV = COL_GK + GLA_DK
COL_GR = COL_GV + GLA_DV
COL_AQ = COL_GR + GLA_DV
COL_AK = COL_AQ + ATT_Q
COL_AV = COL_AK + ATT_KV
COL_LX = COL_AV + ATT_KV
COL_LY = COL_LX + LRU_W
COL_GT = COL_LY + LRU_W
N_MAIN = COL_GT + N_BRANCH * D_MODEL
DEC_ORIG = 2 * GLA_DK + 2 * GLA_DV
DEC_PAD = 128

V7X_VMEM_LIMIT_BYTES = 56 * 1024 * 1024
MOD_ROWS = 16


def _cparams(n_axes, vmem=V7X_VMEM_LIMIT_BYTES):
    return pltpu.CompilerParams(dimension_semantics=("arbitrary",) * n_axes, vmem_limit_bytes=vmem)


def _sigmoid(x):
    return 0.5 * jnp.tanh(0.5 * x) + 0.5


def _mods_kernel(c_ref, w_ref, b_ref, o_ref):
    c = c_ref[...]
    sc = (c * _sigmoid(c)).astype(BF16)
    o_ref[...] = jnp.dot(sc, w_ref[...].astype(BF16), preferred_element_type=F32) + b_ref[...]


def _mods_call(c16, w_mod, b_mod):
    depth, d, n = w_mod.shape
    tn = 1024
    return pl.pallas_call(
        _mods_kernel,
        grid=(depth, n // tn),
        in_specs=[
            pl.BlockSpec((MOD_ROWS, d), lambda l, j: (0, 0)),
            pl.BlockSpec((None, d, tn), lambda l, j: (l, 0, j)),
            pl.BlockSpec((None, 1, tn), lambda l, j: (l, 0, j)),
        ],
        out_specs=pl.BlockSpec((None, MOD_ROWS, tn), lambda l, j: (l, 0, j)),
        out_shape=jax.ShapeDtypeStruct((depth, MOD_ROWS, n), F32),
        compiler_params=_cparams(2),
        name="mods",
    )(c16, w_mod, b_mod.reshape(depth, 1, n))


def _normmod_kernel(h_ref, g_ref, sh_ref, sc_ref, wt_ref, o_ref, dec_ref):
    x = h_ref[...]
    var = jnp.mean(x * x, axis=-1, keepdims=True)
    y = x * lax.rsqrt(var + NORM_EPS) * g_ref[...]
    u = (y * (1.0 + sc_ref[...]) + sh_ref[...]).astype(o_ref.dtype)
    o_ref[...] = u
    dec_ref[...] = lax.dot_general(u, wt_ref[...].astype(BF16), (((1,), (1,)), ((), ())),
                                   preferred_element_type=F32)


def _normmod_call(h, g, mods4, w_in_t, layer, k_shift, k_scale, n_ctx, ctx_row):
    b, r, d = h.shape
    tr = n_ctx

    def mod_map(k):
        return lambda bi, i: (layer, jnp.where(i == 0, ctx_row, bi), 0, k)

    return pl.pallas_call(
        _normmod_kernel,
        grid=(b, r // tr),
        in_specs=[
            pl.BlockSpec((None, tr, d), lambda bi, i: (bi, i, 0)),
            pl.BlockSpec((None, 1, d), lambda bi, i: (layer, 0, 0)),
            pl.BlockSpec((None, None, 1, d), mod_map(k_shift)),
            pl.BlockSpec((None, None, 1, d), mod_map(k_scale)),
            pl.BlockSpec((None, DEC_PAD, d), lambda bi, i: (layer, DEC_ORIG // DEC_PAD, 0)),
        ],
        out_specs=[pl.BlockSpec((None, tr, d), lambda bi, i: (bi, i, 0)),
                   pl.BlockSpec((None, tr, DEC_PAD), lambda bi, i: (bi, i, 0))],
        out_shape=[jax.ShapeDtypeStruct((b, r, d), BF16), jax.ShapeDtypeStruct((b, r, DEC_PAD), F32)],
        compiler_params=_cparams(2),
        name="normmod",
    )(h, g, mods4, mods4, w_in_t)


def _in_proj_kernel(x_ref, wa_ref, wb_ref, o_ref, *, n_plain, shift):
    j = pl.program_id(0)
    nt = (((1,), (1,)), ((), ()))

    @pl.when(j < n_plain)
    def _before_decay_columns():
        w = wa_ref[...].astype(BF16)
        o_ref[...] = lax.dot_general(x_ref[...], w, nt, preferred_element_type=F32).astype(o_ref.dtype)

    @pl.when(j >= n_plain)
    def _after_decay_columns():
        w = jnp.concatenate([wa_ref[shift:, :], wb_ref[...]], axis=0).astype(BF16)
        o_ref[...] = lax.dot_general(x_ref[...], w, nt, preferred_element_type=F32).astype(o_ref.dtype)


def _in_proj_call(x, w_in_t, layer, tm, tn):
    b, r, k = x.shape
    shift = 2 * GLA_RANK
    assert DEC_ORIG % tn == 0 and N_MAIN % tn == 0 and tn % shift == 0
    return pl.pallas_call(
        functools.partial(_in_proj_kernel, n_plain=DEC_ORIG // tn, shift=shift),
        grid=(N_MAIN // tn, b, r // tm),
        in_specs=[
            pl.BlockSpec((None, tm, k), lambda j, bi, i: (bi, i, 0)),
            pl.BlockSpec((None, tn, k), lambda j, bi, i: (layer, j, 0)),
            pl.BlockSpec((None, shift, k), lambda j, bi, i: (layer, (j + 1) * (tn // shift), 0)),
        ],
        out_specs=pl.BlockSpec((None, tm, tn), lambda j, bi, i: (bi, i, j)),
        out_shape=jax.ShapeDtypeStruct((b, r, N_MAIN), BF16),
        compiler_params=_cparams(3),
        name="in_proj",
    )(x, w_in_t, w_in_t)


def _linear_res_kernel(*refs, tm, n_ctx, with_norm):
    x_ref, w_ref, h_ref, gc_ref, gl_ref = refs[:5]
    if with_norm:
        ng_ref, shc_ref, shl_ref, scc_ref, scl_ref, o_ref, u_ref = refs[5:]
    else:
        (o_ref,) = refs[5:]
    i = pl.program_id(2)
    y = jnp.dot(x_ref[...], w_ref[...], preferred_element_type=F32)
    is_ctx = (i * tm + lax.broadcasted_iota(jnp.int32, (tm, 1), 0)) < n_ctx
    h_new = h_ref[...] + jnp.where(is_ctx, gc_ref[...], gl_ref[...]) * y
    o_ref[...] = h_new
    if with_norm:
        var = jnp.mean(h_new * h_new, axis=-1, keepdims=True)
        yn = h_new * lax.rsqrt(var + NORM_EPS) * ng_ref[...]
        shift = jnp.where(is_ctx, shc_ref[...], shl_ref[...])
        scale = jnp.where(is_ctx, scc_ref[...], scl_ref[...])
        u_ref[...] = (yn * (1.0 + scale) + shift).astype(u_ref.dtype)


def _linear_res_call(x, w3, layer, h, mods4, k_gate, n_ctx, ctx_row, tm, tn, name, norm=None):
    b, r, k = x.shape
    n = w3.shape[-1]
    ngate = n // tn

    def mod_spec(lyr, row_of, kk):
        return pl.BlockSpec((None, None, 1, tn), lambda j, bi, i: (lyr, row_of(bi), 0, kk * ngate + j))

    ctx_of = lambda bi: ctx_row
    lat_of = lambda bi: bi
    w_mode = dict(pipeline_mode=pl.Buffered(1)) if tn == n else {}
    in_specs = [
        pl.BlockSpec((None, tm, k), lambda j, bi, i: (bi, i, 0)),
        pl.BlockSpec((None, k, tn), lambda j, bi, i: (layer, 0, j), **w_mode),
        pl.BlockSpec((None, tm, tn), lambda j, bi, i: (bi, i, j)),
        mod_spec(layer, ctx_of, k_gate),
        mod_spec(layer, lat_of, k_gate),
    ]
    args = [x, w3, h, mods4, mods4]
    out_specs = [pl.BlockSpec((None, tm, tn), lambda j, bi, i: (bi, i, j))]
    out_shape = [jax.ShapeDtypeStruct((b, r, n), F32)]
    if norm is not None:
        gains, n_layer, k_shift, k_scale = norm
        assert tn == n
        in_specs += [
            pl.BlockSpec((None, 1, n), lambda j, bi, i: (n_layer, 0, 0)),
            mod_spec(n_layer, ctx_of, k_shift), mod_spec(n_layer, lat_of, k_shift),
            mod_spec(n_layer, ctx_of, k_scale), mod_spec(n_layer, lat_of, k_scale),
        ]
        args += [gains, mods4, mods4, mods4, mods4]
        out_specs.append(pl.BlockSpec((None, tm, tn), lambda j, bi, i: (bi, i, j)))
        out_shape.append(jax.ShapeDtypeStruct((b, r, n), BF16))
    res = pl.pallas_call(
        functools.partial(_linear_res_kernel, tm=tm, n_ctx=n_ctx, with_norm=norm is not None),
        grid=(n // tn, b, r // tm),
        in_specs=in_specs,
        out_specs=out_specs,
        out_shape=out_shape,
        compiler_params=_cparams(3),
        name=name,
    )(*args)
    return res if norm is not None else res[0]


def _swiglu_kernel(x_ref, wg_ref, wu_ref, o_ref):
    x = x_ref[...]
    g = jnp.dot(x, wg_ref[...].astype(BF16), preferred_element_type=F32)
    up = jnp.dot(x, wu_ref[...].astype(BF16), preferred_element_type=F32)
    o_ref[...] = (g * _sigmoid(g) * up).astype(o_ref.dtype)


def _swiglu_call(x, w3, layer, tm, tn):
    b, r, k = x.shape
    hid = w3.shape[-1] // 2
    nb = hid // tn
    return pl.pallas_call(
        _swiglu_kernel,
        grid=(nb, b, r // tm),
        in_specs=[
            pl.BlockSpec((None, tm, k), lambda j, bi, i: (bi, i, 0)),
            pl.BlockSpec((None, k, tn), lambda j, bi, i: (layer, 0, j)),
            pl.BlockSpec((None, k, tn), lambda j, bi, i: (layer, 0, nb + j)),
        ],
        out_specs=pl.BlockSpec((None, tm, tn), lambda j, bi, i: (bi, i, j)),
        out_shape=jax.ShapeDtypeStruct((b, r, hid), BF16),
        compiler_params=_cparams(3),
        name="ffn_in",
    )(x, w3, w3)


def _norm_rope(x, gain, cs, sn):
    var = jnp.mean(x * x, axis=-1, keepdims=True)
    y = x * lax.rsqrt(var + NORM_EPS) * gain
    if cs is None:
        return y
    lane = lax.broadcasted_iota(jnp.int32, y.shape, 1)
    nxt = pltpu.roll(y, ATT_HD - 1, 1)
    prv = pltpu.roll(y, 1, 1)
    partner = jnp.where(lane % 2 == 0, nxt, prv)
    return y * cs + partner * sn


def _attn_kernel(q_ref, k_ref, v_ref, qg_ref, kg_ref, cos_ref, sin_ref, o_ref, kt_s, vx_s, *, n_ctx, tq):
    qi = pl.program_id(2)
    rows = k_ref.shape[0]
    scale = ATT_HD ** -0.5 * float(np.log2(np.e))

    @pl.when(qi == 0)
    def _prepare_kv():
        kg = kg_ref[...]
        for c in range(rows // tq):
            kc = k_ref[c * tq:(c + 1) * tq, :].astype(F32)
            if c == 0:
                kn = _norm_rope(kc, kg, None, None)
            else:
                kn = _norm_rope(kc, kg, cos_ref[(c - 1) * tq:c * tq, :], sin_ref[(c - 1) * tq:c * tq, :])
            kt_s[:, c * tq:(c + 1) * tq] = kn.T.astype(BF16)
        lane = lax.broadcasted_iota(jnp.int32, (rows, ATT_HD), 1)
        vx_s[:, :ATT_HD] = v_ref[...]
        vx_s[:, ATT_HD:] = jnp.where(lane == 0, 1.0, 0.0).astype(BF16)

    def attend(cs, sn, n_keys):
        qg = qg_ref[...]
        scores = []
        for g in range(ATT_GROUP):
            q = _norm_rope(q_ref[:, g * ATT_HD:(g + 1) * ATT_HD].astype(F32), qg, cs, sn) * scale
            scores.append(jnp.dot(q.astype(BF16), kt_s[:, :n_keys], preferred_element_type=F32))
        for g in range(ATT_GROUP):
            s = scores[g]
            p = jnp.exp2(s - jnp.max(s, axis=-1, keepdims=True))
            ox = jnp.dot(p.astype(BF16), vx_s[:n_keys, :], preferred_element_type=F32)
            o = ox[:, :ATT_HD] / ox[:, ATT_HD:ATT_HD + 1]
            o_ref[:, g * ATT_HD:(g + 1) * ATT_HD] = o.astype(o_ref.dtype)

    @pl.when(qi == 0)
    def _context_queries():
        attend(None, None, n_ctx)

    @pl.when(qi > 0)
    def _latent_queries():
        off = pl.multiple_of((qi - 1) * tq, tq)
        attend(cos_ref[pl.ds(off, tq), :], sin_ref[pl.ds(off, tq), :], rows)


def _attn_call(p, q_g, k_g, cos_t, sin_t, layer, n_ctx):
    b, r, _ = p.shape
    tq = n_ctx
    t = r - n_ctx
    qw = ATT_GROUP * ATT_HD
    return pl.pallas_call(
        functools.partial(_attn_kernel, n_ctx=n_ctx, tq=tq),
        grid=(b, ATT_KV_HEADS, r // tq),
        in_specs=[
            pl.BlockSpec((None, tq, qw), lambda bi, kv, qi: (bi, qi, COL_AQ // qw + kv)),
            pl.BlockSpec((None, r, ATT_HD), lambda bi, kv, qi: (bi, 0, COL_AK // ATT_HD + kv)),
            pl.BlockSpec((None, r, ATT_HD), lambda bi, kv, qi: (bi, 0, COL_AV // ATT_HD + kv)),
            pl.BlockSpec((None, 1, ATT_HD), lambda bi, kv, qi: (layer, 0, 0)),
            pl.BlockSpec((None, 1, ATT_HD), lambda bi, kv, qi: (layer, 0, 0)),
            pl.BlockSpec((t, ATT_HD), lambda bi, kv, qi: (0, 0)),
            pl.BlockSpec((t, ATT_HD), lambda bi, kv, qi: (0, 0)),
        ],
        out_specs=pl.BlockSpec((None, tq, qw), lambda bi, kv, qi: (bi, qi, kv)),
        out_shape=jax.ShapeDtypeStruct((b, r, ATT_Q), BF16),
        scratch_shapes=[pltpu.VMEM((ATT_HD, r), BF16), pltpu.VMEM((r, 2 * ATT_HD), BF16)],
        compiler_params=_cparams(3),
        name="gqa",
    )(p, p, p, q_g, k_g, cos_t, sin_t)


def _gla_direction(q, k, v, dec, wd, bd, vis, st_s, reverse, n_chunks):
    c_len = GLA_CHUNK

    tri = vis > 0.5
    tri_b = vis.astype(BF16)
    nt = (((1,), (1,)), ((), ()))
    tn = (((0,), (0,)), ((), ()))

    x = jnp.dot(dec, wd, preferred_element_type=F32) + bd
    g = (jnp.minimum(x, 0.0) - jnp.log(1.0 + jnp.exp(-jnp.abs(x)))) * (1.0 / GLA_GATE_NORM)
    g_hi = g.astype(BF16)
    g_lo = (g - g_hi.astype(F32)).astype(BF16)
    bcum = jnp.dot(tri_b, g_hi, preferred_element_type=F32) + jnp.dot(tri_b, g_lo, preferred_element_type=F32)
    q = q.astype(F32) * (GLA_HDK ** -0.5)
    k = k.astype(F32)
    q_dec = (q * jnp.exp(bcum)).astype(BF16)
    k_inv = (k * jnp.exp(-bcum)).astype(BF16)
    att = lax.dot_general(q_dec, k_inv, nt, preferred_element_type=F32)
    att = jnp.where(tri, att, 0.0).astype(BF16)
    o_intra = jnp.dot(att, v, preferred_element_type=F32)
    b_last, k_dec = [], []
    for c in range(n_chunks):
        sl = slice(c * c_len, (c + 1) * c_len)
        end = c * c_len if reverse else (c + 1) * c_len - 1
        b_last.append(bcum[end:end + 1, :])
        k_dec.append((k[sl, :] * jnp.exp(b_last[c] - bcum[sl, :])).astype(BF16))

    outs = [None] * n_chunks
    chunk_order = range(n_chunks - 1, -1, -1) if reverse else range(n_chunks)
    st = st_s[...]
    for c in chunk_order:
        sl = slice(c * c_len, (c + 1) * c_len)
        outs[c] = o_intra[sl, :] + lax.dot_general(q_dec[sl, :], st.astype(BF16), nt, preferred_element_type=F32)
        st = st * jnp.exp(b_last[c]) + lax.dot_general(v[sl, :], k_dec[c], tn, preferred_element_type=F32)
    st_s[...] = st
    return jnp.concatenate(outs, axis=0)


def _gla_kernel(qf_ref, kf_ref, vf_ref, decf_ref, qb_ref, kb_ref, vb_ref, decb_ref,
                wdf_ref, bdf_ref, wdb_ref, bdb_ref, visf_ref, visb_ref, r_ref, ng_ref, o_ref,
                stf_s, stb_s, of_s, ob_s, *, n_chunks, nblk):
    t = pl.program_id(2)
    tb = n_chunks * GLA_CHUNK

    @pl.when(t == 0)
    def _zero_states():
        stf_s[...] = jnp.zeros_like(stf_s)
        stb_s[...] = jnp.zeros_like(stb_s)

    blk_b = jnp.where(t == 0, 0, nblk - t)
    rows_f = pl.ds(pl.multiple_of(t * tb, tb), tb)
    rows_b = pl.ds(pl.multiple_of(blk_b * tb, tb), tb)
    decf = decf_ref[...].astype(BF16)
    decb = decb_ref[...].astype(BF16)
    visf = visf_ref[...]
    visb = visb_ref[...]
    for hh in range(GLA_HEADS_PER_STEP):
        kcol = slice(hh * GLA_HDK, (hh + 1) * GLA_HDK)
        vcol = slice(hh * GLA_HDV, (hh + 1) * GLA_HDV)
        of_s[rows_f, vcol] = _gla_direction(
            qf_ref[:, kcol], kf_ref[:, kcol], vf_ref[:, vcol], decf, wdf_ref[:, kcol].astype(BF16),
            bdf_ref[:, kcol], visf, stf_s.at[hh], False, n_chunks)
        ob_s[rows_b, vcol] = _gla_direction(
            qb_ref[:, kcol], kb_ref[:, kcol], vb_ref[:, vcol], decb, wdb_ref[:, kcol].astype(BF16),
            bdb_ref[:, kcol], visb, stb_s.at[hh], True, n_chunks)

    @pl.when(t == nblk - 1)
    def _combine_norm_gate():
        for c in range(nblk):
            sl = slice(c * tb, (c + 1) * tb)
            for hh in range(GLA_HEADS_PER_STEP):
                vcol = slice(hh * GLA_HDV, (hh + 1) * GLA_HDV)
                o = of_s[sl, vcol] + ob_s[sl, vcol]
                var = jnp.mean(o * o, axis=-1, keepdims=True)
                y = o * lax.rsqrt(var + NORM_EPS) * ng_ref[...]
                r = r_ref[sl, vcol].astype(F32)
                o_ref[sl, vcol] = (y * (r * _sigmoid(r))).astype(o_ref.dtype)


def _gla_call(p, dec, wd_pad, bd, norm_g, layer, n_ctx):
    b, r, _ = p.shape
    tb = n_ctx
    nblk = r // tb

    def blk(t, direction):
        if direction == 0:
            return t
        return jnp.where(t == 0, 0, nblk - t)

    kw = GLA_HEADS_PER_STEP * GLA_HDK
    vw = GLA_HEADS_PER_STEP * GLA_HDV

    def stream_specs(direction):
        return [
            pl.BlockSpec((None, tb, kw), lambda bi, h, t: (bi, blk(t, direction), COL_GQ // kw + h)),
            pl.BlockSpec((None, tb, kw), lambda bi, h, t: (bi, blk(t, direction), COL_GK // kw + h)),
            pl.BlockSpec((None, tb, vw), lambda bi, h, t: (bi, blk(t, direction), COL_GV // vw + h)),
            pl.BlockSpec((None, tb, DEC_PAD), lambda bi, h, t: (bi, blk(t, direction), 0)),
        ]

    def decay_specs(direction):
        return [
            pl.BlockSpec((None, None, DEC_PAD, kw), lambda bi, h, t: (layer, direction, 0, h)),
            pl.BlockSpec((None, None, 1, kw), lambda bi, h, t: (layer, direction, 0, h)),
        ]

    idx = np.arange(tb)
    same_chunk = (idx[:, None] // GLA_CHUNK) == (idx[None, :] // GLA_CHUNK)
    vis = jnp.asarray(np.stack([same_chunk & (idx[:, None] >= idx[None, :]),
                                same_chunk & (idx[None, :] >= idx[:, None])]).astype(np.float32))

    return pl.pallas_call(
        functools.partial(_gla_kernel, n_chunks=tb // GLA_CHUNK, nblk=nblk),
        grid=(b, GLA_HEADS // GLA_HEADS_PER_STEP, nblk),
        in_specs=stream_specs(0) + stream_specs(1) + decay_specs(0) + decay_specs(1) + [
            pl.BlockSpec((None, tb, tb), lambda bi, h, t: (0, 0, 0)),
            pl.BlockSpec((None, tb, tb), lambda bi, h, t: (1, 0, 0)),
            pl.BlockSpec((None, r, vw), lambda bi, h, t: (bi, 0, COL_GR // vw + h)),
            pl.BlockSpec((None, 1, GLA_HDV), lambda bi, h, t: (layer, 0, 0)),
        ],
        out_specs=pl.BlockSpec((None, r, vw), lambda bi, h, t: (bi, 0, h)),
        out_shape=jax.ShapeDtypeStruct((b, r, GLA_DV), BF16),
        scratch_shapes=([pltpu.VMEM((GLA_HEADS_PER_STEP, GLA_HDV, GLA_HDK), F32)] * 2
                        + [pltpu.VMEM((r, vw), F32)] * 2),
        compiler_params=_cparams(3),
        name="gla",
    )(p, p, p, dec, p, p, p, dec, wd_pad, bd, wd_pad, bd, vis, vis, p, norm_g)


LRU_PAD = 8


def _lru_kernel(x_ref, y_ref, cw_ref, cb_ref, wa_ref, ba_ref, wi_ref, bi_ref, lam_ref, o_ref,
                xp_s, af_s, bf_s, ab_s, bb_s, *, n_ctx, tc, tr):
    rows = x_ref.shape[0]
    n_lat = rows - n_ctx
    sub = 8
    ctx0 = LRU_PAD
    lat0 = 2 * LRU_PAD + n_ctx
    zpad = jnp.zeros((LRU_PAD, tc), F32)
    xp_s[0:LRU_PAD, :] = zpad
    xp_s[ctx0 + n_ctx:lat0, :] = zpad
    xp_s[lat0 + n_lat:lat0 + n_lat + LRU_PAD, :] = zpad
    xp_s[ctx0:ctx0 + n_ctx, :] = x_ref[0:n_ctx, :].astype(F32)
    xp_s[lat0:lat0 + n_lat, :] = x_ref[n_ctx:rows, :].astype(F32)

    cw = cw_ref[...]
    cb = cb_ref[...]
    lam = lam_ref[...]
    neg_sp = -(jnp.maximum(-lam, 0.0) + jnp.log1p(jnp.exp(-jnp.abs(lam))))
    log2_a_half = (0.5 * LRU_C * float(np.log2(np.e))) * neg_sp
    row_id = lax.broadcasted_iota(jnp.int32, (tr, 1), 0)
    n_kb = tc // LRU_BS
    wa_h = [[(0.5 * wa_ref[d, kb]).astype(BF16) for kb in range(n_kb)] for d in range(2)]
    wi_h = [[(0.5 * wi_ref[d, kb]).astype(BF16) for kb in range(n_kb)] for d in range(2)]
    ba_h = 0.5 * ba_ref[...]
    bi_h = 0.5 * bi_ref[...]

    for c in range(rows // tr):
        r0 = c * tr
        p0 = (ctx0 if r0 < n_ctx else lat0 - n_ctx) + r0
        n_win = tr + 2 * sub
        win = xp_s[p0 - sub:p0 + tr + sub, :]
        xc = cb + win[sub:sub + tr, :] * cw[CONV_LEFT:CONV_LEFT + 1, :]
        for j in range(CONV_W):
            off = j - CONV_LEFT
            if off != 0:
                xc = xc + pltpu.roll(win, (-off) % n_win, 0)[sub:sub + tr, :] * cw[j:j + 1, :]
        xh = 0.5 * xc
        xb = [xc[:, kb * LRU_BS:(kb + 1) * LRU_BS].astype(BF16) for kb in range(n_kb)]
        for d, (a_s, b_s) in enumerate(((af_s, bf_s), (ab_s, bb_s))):
            za = jnp.concatenate([jnp.dot(xb[kb], wa_h[d][kb], preferred_element_type=F32)
                                  for kb in range(n_kb)], axis=1)
            zi = jnp.concatenate([jnp.dot(xb[kb], wi_h[d][kb], preferred_element_type=F32)
                                  for kb in range(n_kb)], axis=1)
            ta = jnp.tanh(za + ba_h[d:d + 1, :])
            ti = jnp.tanh(zi + bi_h[d:d + 1, :])
            la_h = log2_a_half[d:d + 1, :]
            a = jnp.exp2(la_h + la_h * ta)
            q = 1.0 - a * a
            mult = jnp.where(q > 0.0, q * lax.rsqrt(q), 0.0)
            first = 0 if d == 0 else n_ctx - 1
            if r0 <= first < r0 + tr:
                mult = jnp.where(row_id == first - r0, 1.0, mult)
            a_s[r0:r0 + tr, :] = a
            b_s[r0:r0 + tr, :] = (mult * xh) * (1.0 + ti)

    row8 = lax.broadcasted_iota(jnp.int32, (sub, tc), 0)

    def tile_maps(a, b, reverse):
        for dlt in (1, 2, 4):
            shift = sub - dlt if reverse else dlt
            valid = (row8 < sub - dlt) if reverse else (row8 >= dlt)
            a_sh = jnp.where(valid, pltpu.roll(a, shift, 0), 1.0)
            b_sh = jnp.where(valid, pltpu.roll(b, shift, 0), 0.0)
            b = b + a * b_sh
            a = a * a_sh
        return a, b

    def scan_tile(tile_f, tile_b, carry):
        hf, hb = carry
        rf = pl.ds(pl.multiple_of(tile_f * sub, sub), sub)
        rb = pl.ds(pl.multiple_of(tile_b * sub, sub), sub)
        a_f, b_f = tile_maps(af_s[rf, :], bf_s[rf, :], False)
        a_b, b_b = tile_maps(ab_s[rb, :], bb_s[rb, :], True)
        h_f = b_f + a_f * hf
        h_b = b_b + a_b * hb
        bf_s[rf, :] = h_f
        bb_s[rb, :] = h_b
        return h_f[sub - 1:sub, :], h_b[0:1, :]

    zero = jnp.zeros((1, tc), F32)
    ctx_tiles = n_ctx // sub
    all_tiles = rows // sub
    carry = lax.fori_loop(0, ctx_tiles, lambda i, cr: scan_tile(i, ctx_tiles - 1 - i, cr), (zero, zero),
                          unroll=2)
    lax.fori_loop(ctx_tiles, all_tiles, lambda i, cr: scan_tile(i, all_tiles - 1 + ctx_tiles - i, cr), carry,
                  unroll=2)

    k0 = 0.7978845608028654
    for c in range(rows // tr):
        sl = slice(c * tr, (c + 1) * tr)
        y = y_ref[sl, :].astype(F32)
        gelu = 0.5 * y * (1.0 + jnp.tanh(k0 * (y + 0.044715 * (y * y * y))))
        o_ref[sl, :] = ((bf_s[sl, :] + bb_s[sl, :]) * gelu).astype(o_ref.dtype)


def _lru_call(p, conv_w, conv_b, w_a, b_a, w_i, b_i, lam, layer, n_ctx):
    b, r, _ = p.shape
    tc = 256
    nb = tc // LRU_BS
    chan = lambda bi, j: (layer, 0, j)
    return pl.pallas_call(
        functools.partial(_lru_kernel, n_ctx=n_ctx, tc=tc, tr=n_ctx),
        grid=(b, LRU_W // tc),
        in_specs=[
            pl.BlockSpec((None, r, tc), lambda bi, j: (bi, 0, COL_LX // tc + j)),
            pl.BlockSpec((None, r, tc), lambda bi, j: (bi, 0, COL_LY // tc + j)),
            pl.BlockSpec((None, CONV_W, tc), chan),
            pl.BlockSpec((None, 1, tc), chan),
            pl.BlockSpec((None, 2, nb, LRU_BS, LRU_BS), lambda bi, j: (layer, 0, j, 0, 0)),
            pl.BlockSpec((None, 2, tc), chan),
            pl.BlockSpec((None, 2, nb, LRU_BS, LRU_BS), lambda bi, j: (layer, 0, j, 0, 0)),
            pl.BlockSpec((None, 2, tc), chan),
            pl.BlockSpec((None, 2, tc), chan),
        ],
        out_specs=pl.BlockSpec((None, r, tc), lambda bi, j: (bi, 0, j)),
        out_shape=jax.ShapeDtypeStruct((b, r, LRU_W), BF16),
        scratch_shapes=[pltpu.VMEM((r + 3 * LRU_PAD, tc), F32)] + [pltpu.VMEM((r, tc), F32)] * 4,
        compiler_params=_cparams(2),
        name="rglru",
    )(p, p, conv_w, conv_b, w_a, b_a, w_i, b_i, lam)


def _merge_kernel(xa_ref, xb_ref, xc_ref, wa_ref, wb_ref, wc_ref, ga_ref, gb_ref, gc_ref, bm_ref, o_ref):
    acc = None
    for n, (x_ref, w_ref, g_ref) in enumerate(((xa_ref, wa_ref, ga_ref), (xb_ref, wb_ref, gb_ref),
                                               (xc_ref, wc_ref, gc_ref))):
        proj = jnp.dot(x_ref[...], w_ref[...], preferred_element_type=F32)
        term = _sigmoid(g_ref[...].astype(F32) + bm_ref[n:n + 1, :]) * proj
        acc = term if acc is None else acc + term
    o_ref[...] = acc.astype(o_ref.dtype)


def _merge_call(o_gla, o_att, o_lru, wbr, p, b_merge, layer, tm, tn):
    b, r, d = o_gla.shape
    ncol = d // tn
    x_spec = pl.BlockSpec((None, tm, d), lambda j, bi, i: (bi, i, 0))

    def w_spec(n):
        return pl.BlockSpec((None, None, d, tn), lambda j, bi, i: (layer, n, 0, j))

    def g_spec(n):
        return pl.BlockSpec((None, tm, tn), lambda j, bi, i: (bi, i, COL_GT // tn + n * ncol + j))

    return pl.pallas_call(
        _merge_kernel,
        grid=(ncol, b, r // tm),
        in_specs=[x_spec, x_spec, x_spec, w_spec(0), w_spec(1), w_spec(2), g_spec(0), g_spec(1), g_spec(2),
                  pl.BlockSpec((None, N_BRANCH, tn), lambda j, bi, i: (layer, 0, j))],
        out_specs=pl.BlockSpec((None, tm, tn), lambda j, bi, i: (bi, i, j)),
        out_shape=jax.ShapeDtypeStruct((b, r, d), BF16),
        compiler_params=_cparams(3),
        name="merge",
    )(o_gla, o_att, o_lru, wbr, wbr, wbr, p, p, p, b_merge)


def _final_norm_kernel(h_ref, g_ref, o_ref):
    x = h_ref[...]
    var = jnp.mean(x * x, axis=-1, keepdims=True)
    o_ref[...] = x * lax.rsqrt(var + NORM_EPS) * g_ref[...]


def _final_norm_call(h, g, n_ctx):
    b, r, d = h.shape
    tr = n_ctx
    return pl.pallas_call(
        _final_norm_kernel,
        grid=(b, (r - n_ctx) // tr),
        in_specs=[
            pl.BlockSpec((None, tr, d), lambda bi, i: (bi, i + 1, 0)),
            pl.BlockSpec((1, d), lambda bi, i: (0, 0)),
        ],
        out_specs=pl.BlockSpec((None, tr, d), lambda bi, i: (bi, i, 0)),
        out_shape=jax.ShapeDtypeStruct((b, r - n_ctx, d), F32),
        compiler_params=_cparams(2),
        name="final_norm",
    )(h, g)


def _rope_tables(n_tokens):
    rows = n_tokens // GRID_W
    row = jnp.repeat(jnp.arange(rows, dtype=F32), GRID_W)
    col = jnp.tile(jnp.arange(GRID_W, dtype=F32), rows)
    inv = ROPE_THETA ** (-jnp.arange(ROPE_AXIS_PAIRS, dtype=F32) / ROPE_AXIS_PAIRS)
    ang = jnp.concatenate([row[:, None] * inv, col[:, None] * inv], axis=-1)
    cos, sin = jnp.cos(ang), jnp.sin(ang)
    cos_t = jnp.repeat(cos, 2, axis=-1)
    sin_t = jnp.stack([-sin, sin], axis=-1).reshape(n_tokens, ATT_HD)
    return cos_t, sin_t


def kernel(x, c, ctx, c_ctx, w_mod, b_mod, norm_mix_g, norm_ffn_g, w_in, gla_w_decay, gla_b_decay, gla_norm_g, q_norm_g, k_norm_g, conv_w, conv_b, lru_w_a, lru_b_a, lru_w_i, lru_b_i, lru_lambda, b_merge, w_branch, w_out, w_ffn_in, w_ffn_out, final_norm_g):
    depth = w_in.shape[0]
    batch, t_lat, d = x.shape
    n_ctx = ctx.shape[1]
    rows = n_ctx + t_lat
    ctx_row = batch
    assert batch < MOD_ROWS and rows % n_ctx == 0 and n_ctx % GLA_CHUNK == 0
    tm = 768 if rows % 768 == 0 else n_ctx
    tm_big = 1152 if rows % 1152 == 0 else tm
    tm_small = 384 if rows % 384 == 0 else n_ctx
    tn_in, tn_merge, tn_out, tn_ffn_in, tn_ffn_out = 1024, 512, d, 512, 1024

    w_in_t = jnp.swapaxes(w_in, 1, 2)
    w_branch_b = w_branch.astype(BF16)
    w_out_b = w_out.astype(BF16)
    w_ffn_out_b = w_ffn_out.astype(BF16)
    wd_pad = jnp.zeros((depth, 2, DEC_PAD, GLA_DK), F32)
    wd_pad = wd_pad.at[:, 0, :GLA_RANK].set(gla_w_decay[:, 0])
    wd_pad = wd_pad.at[:, 1, GLA_RANK:2 * GLA_RANK].set(gla_w_decay[:, 1])
    bd = gla_b_decay.reshape(depth, 2, 1, GLA_DK)
    cos_t, sin_t = _rope_tables(t_lat)

    c16 = jnp.zeros((MOD_ROWS, d), F32).at[:batch].set(c).at[ctx_row].set(c_ctx)
    mods4 = _mods_call(c16, w_mod, b_mod).reshape(depth, MOD_ROWS, 1, N_MOD * d)

    g_mix = norm_mix_g.reshape(depth, 1, d)
    g_ffn = norm_ffn_g.reshape(depth, 1, d)
    g_gla = gla_norm_g.reshape(depth, 1, GLA_HDV)
    g_q = q_norm_g.reshape(depth, 1, ATT_HD)
    g_k = k_norm_g.reshape(depth, 1, ATT_HD)
    cb3 = conv_b.reshape(depth, 1, LRU_W)

    h = jnp.concatenate([ctx, x], axis=1)
    for l in range(depth):
        u, dec = _normmod_call(h, g_mix, mods4, w_in_t, l, 0, 1, n_ctx, ctx_row)
        p = _in_proj_call(u, w_in_t, l, tm_big, tn_in)
        o_att = _attn_call(p, g_q, g_k, cos_t, sin_t, l, n_ctx)
        o_gla = _gla_call(p, dec, wd_pad, bd, g_gla, l, n_ctx)
        o_lru = _lru_call(p, conv_w, cb3, lru_w_a, lru_b_a, lru_w_i, lru_b_i, lru_lambda, l, n_ctx)
        merged = _merge_call(o_gla, o_att, o_lru, w_branch_b, p, b_merge, l, tm, tn_merge)
        h, u2 = _linear_res_call(merged, w_out_b, l, h, mods4, 2, n_ctx, ctx_row, tm, tn_out, "out_proj",
                                 norm=(g_ffn, l, 3, 4))
        a = _swiglu_call(u2, w_ffn_in, l, tm_big, tn_ffn_in)
        h = _linear_res_call(a, w_ffn_out_b, l, h, mods4, 5, n_ctx, ctx_row, tm_small, tn_ffn_out, "ffn_out")
    return _final_norm_call(h, final_norm_g.reshape(1, d), n_ctx)
```

```python
import functools

import jax
import jax.numpy as jnp
import numpy as np
from jax import lax
from jax.experimental import pallas as pl
from jax.experimental.pallas import tpu as pltpu

F32 = jnp.float32
BF16 = jnp.bfloat16

D_MODEL = 2048
GRID_W = 64
NORM_EPS = 1e-6
N_MOD = 6

GLA_HEADS = 4
GLA_DK = D_MODEL // 2
GLA_DV = D_MODEL
GLA_HDK = GLA_DK // GLA_HEADS
GLA_HDV = GLA_DV // GLA_HEADS
GLA_RANK = 16
GLA_GATE_NORM = 16.0
GLA_CHUNK = 64
GLA_HEADS_PER_STEP = 2

ATT_HEADS = 16
ATT_KV_HEADS = 4
ATT_HD = 128
ATT_GROUP = ATT_HEADS // ATT_KV_HEADS
ATT_Q = ATT_HEADS * ATT_HD
ATT_KV = ATT_KV_HEADS * ATT_HD
ROPE_THETA = 10000.0
ROPE_AXIS_PAIRS = ATT_HD // 4

LRU_W = D_MODEL
LRU_BLOCKS = 16
LRU_BS = LRU_W // LRU_BLOCKS
LRU_C = 8.0
CONV_W = 4
CONV_LEFT = 2

N_BRANCH = 3
FFN_HIDDEN = -(-8 * D_MODEL // (3 * 256)) * 256

COL_GQ = 0
COL_GK = COL_GQ + GLA_DK
COL_GV = COL_GK + GLA_DK
COL_GR = COL_GV + GLA_DV
COL_AQ = COL_GR + GLA_DV
COL_AK = COL_AQ + ATT_Q
COL_AV = COL_AK + ATT_KV
COL_LX = COL_AV + ATT_KV
COL_LY = COL_LX + LRU_W
COL_GT = COL_LY + LRU_W
N_MAIN = COL_GT + N_BRANCH * D_MODEL
DEC_ORIG = 2 * GLA_DK + 2 * GLA_DV
DEC_PAD = 128

V7X_VMEM_LIMIT_BYTES = 56 * 1024 * 1024
MOD_ROWS = 16


def _cparams(n_axes, vmem=V7X_VMEM_LIMIT_BYTES):
    return pltpu.CompilerParams(dimension_semantics=("arbitrary",) * n_axes, vmem_limit_bytes=vmem)


def _sigmoid(x):
    return 0.5 * jnp.tanh(0.5 * x) + 0.5


def _silu(x):
    h = 0.5 * x
    return h + h * jnp.tanh(h)


def _mods_kernel(c_ref, w_ref, b_ref, o_ref):
    c = c_ref[...]
    sc = _silu(c).astype(BF16)
    o_ref[...] = jnp.dot(sc, w_ref[...].astype(BF16), preferred_element_type=F32) + b_ref[...]


def _mods_call(c16, w_mod, b_mod):
    depth, d, n = w_mod.shape
    tn = 1024
    return pl.pallas_call(
        _mods_kernel,
        grid=(depth, n // tn),
        in_specs=[
            pl.BlockSpec((MOD_ROWS, d), lambda l, j: (0, 0)),
            pl.BlockSpec((None, d, tn), lambda l, j: (l, 0, j)),
            pl.BlockSpec((None, 1, tn), lambda l, j: (l, 0, j)),
        ],
        out_specs=pl.BlockSpec((None, MOD_ROWS, tn), lambda l, j: (l, 0, j)),
        out_shape=jax.ShapeDtypeStruct((depth, MOD_ROWS, n), F32),
        compiler_params=_cparams(2),
        name="mods",
    )(c16, w_mod, b_mod.reshape(depth, 1, n))


def _normmod_kernel(h_ref, g_ref, shc_ref, shl_ref, scc_ref, scl_ref, wt_ref, o_ref, dec_ref, *, tr, n_ctx):
    x = h_ref[...]
    var = jnp.mean(x * x, axis=-1, keepdims=True)
    y = x * lax.rsqrt(var + NORM_EPS) * g_ref[...]
    is_ctx = (pl.program_id(1) * tr + lax.broadcasted_iota(jnp.int32, (tr, 1), 0)) < n_ctx
    shift = jnp.where(is_ctx, shc_ref[...], shl_ref[...])
    scale = jnp.where(is_ctx, scc_ref[...], scl_ref[...])
    u = (y * (1.0 + scale) + shift).astype(o_ref.dtype)
    o_ref[...] = u
    dec_ref[...] = lax.dot_general(u, wt_ref[...].astype(BF16), (((1,), (1,)), ((), ())),
                                   preferred_element_type=F32)


def _normmod_call(h, g, mods4, w_in_t, layer, k_shift, k_scale, n_ctx, ctx_row, tr):
    b, r, d = h.shape

    def mod_spec(k, is_ctx):
        return pl.BlockSpec((None, None, 1, d), lambda bi, i: (layer, ctx_row if is_ctx else bi, 0, k))

    return pl.pallas_call(
        functools.partial(_normmod_kernel, tr=tr, n_ctx=n_ctx),
        grid=(b, r // tr),
        in_specs=[
            pl.BlockSpec((None, tr, d), lambda bi, i: (bi, i, 0)),
            pl.BlockSpec((None, 1, d), lambda bi, i: (layer, 0, 0)),
            mod_spec(k_shift, True), mod_spec(k_shift, False),
            mod_spec(k_scale, True), mod_spec(k_scale, False),
            pl.BlockSpec((None, DEC_PAD, d), lambda bi, i: (layer, DEC_ORIG // DEC_PAD, 0)),
        ],
        out_specs=[pl.BlockSpec((None, tr, d), lambda bi, i: (bi, i, 0)),
                   pl.BlockSpec((None, tr, DEC_PAD), lambda bi, i: (bi, i, 0))],
        out_shape=[jax.ShapeDtypeStruct((b, r, d), BF16), jax.ShapeDtypeStruct((b, r, DEC_PAD), F32)],
        compiler_params=_cparams(2),
        name="normmod",
    )(h, g, mods4, mods4, mods4, mods4, w_in_t)


def _in_proj_kernel(x_ref, wa_ref, wb_ref, o_ref, *, n_plain, shift):
    j = pl.program_id(0)
    nt = (((1,), (1,)), ((), ()))

    @pl.when(j < n_plain)
    def _before_decay_columns():
        w = wa_ref[...].astype(BF16)
        o_ref[...] = lax.dot_general(x_ref[...], w, nt, preferred_element_type=F32).astype(o_ref.dtype)

    @pl.when(j >= n_plain)
    def _after_decay_columns():
        w = jnp.concatenate([wa_ref[shift:, :], wb_ref[...]], axis=0).astype(BF16)
        o_ref[...] = lax.dot_general(x_ref[...], w, nt, preferred_element_type=F32).astype(o_ref.dtype)


def _in_proj_call(x, w_in_t, layer, tm, tn):
    b, r, k = x.shape
    shift = 2 * GLA_RANK
    assert DEC_ORIG % tn == 0 and N_MAIN % tn == 0 and tn % shift == 0
    return pl.pallas_call(
        functools.partial(_in_proj_kernel, n_plain=DEC_ORIG // tn, shift=shift),
        grid=(N_MAIN // tn, b, r // tm),
        in_specs=[
            pl.BlockSpec((None, tm, k), lambda j, bi, i: (bi, i, 0)),
            pl.BlockSpec((None, tn, k), lambda j, bi, i: (layer, j, 0)),
            pl.BlockSpec((None, shift, k), lambda j, bi, i: (layer, (j + 1) * (tn // shift), 0)),
        ],
        out_specs=pl.BlockSpec((None, tm, tn), lambda j, bi, i: (bi, i, j)),
        out_shape=jax.ShapeDtypeStruct((b, r, N_MAIN), BF16),
        compiler_params=_cparams(3),
        name="in_proj",
    )(x, w_in_t, w_in_t)


def _linear_res_kernel(*refs, tm, n_ctx, with_norm):
    x_ref, w_ref, h_ref, gc_ref, gl_ref = refs[:5]
    if with_norm:
        ng_ref, shc_ref, shl_ref, scc_ref, scl_ref, o_ref, u_ref = refs[5:]
    else:
        (o_ref,) = refs[5:]
    i = pl.program_id(2)
    y = jnp.dot(x_ref[...], w_ref[...], preferred_element_type=F32)
    is_ctx = (i * tm + lax.broadcasted_iota(jnp.int32, (tm, 1), 0)) < n_ctx
    h_new = h_ref[...] + jnp.where(is_ctx, gc_ref[...], gl_ref[...]) * y
    o_ref[...] = h_new
    if with_norm:
        var = jnp.mean(h_new * h_new, axis=-1, keepdims=True)
        yn = h_new * lax.rsqrt(var + NORM_EPS) * ng_ref[...]
        shift = jnp.where(is_ctx, shc_ref[...], shl_ref[...])
        scale = jnp.where(is_ctx, scc_ref[...], scl_ref[...])
        u_ref[...] = (yn * (1.0 + scale) + shift).astype(u_ref.dtype)


def _linear_res_call(x, w3, layer, h, mods4, k_gate, n_ctx, ctx_row, tm, tn, name, norm=None):
    b, r, k = x.shape
    n = w3.shape[-1]
    ngate = n // tn

    def mod_spec(lyr, row_of, kk):
        return pl.BlockSpec((None, None, 1, tn), lambda j, bi, i: (lyr, row_of(bi), 0, kk * ngate + j))

    ctx_of = lambda bi: ctx_row
    lat_of = lambda bi: bi
    w_mode = dict(pipeline_mode=pl.Buffered(1)) if tn == n else {}
    in_specs = [
        pl.BlockSpec((None, tm, k), lambda j, bi, i: (bi, i, 0)),
        pl.BlockSpec((None, k, tn), lambda j, bi, i: (layer, 0, j), **w_mode),
        pl.BlockSpec((None, tm, tn), lambda j, bi, i: (bi, i, j)),
        mod_spec(layer, ctx_of, k_gate),
        mod_spec(layer, lat_of, k_gate),
    ]
    args = [x, w3, h, mods4, mods4]
    out_specs = [pl.BlockSpec((None, tm, tn), lambda j, bi, i: (bi, i, j))]
    out_shape = [jax.ShapeDtypeStruct((b, r, n), F32)]
    if norm is not None:
        gains, n_layer, k_shift, k_scale = norm
        assert tn == n
        in_specs += [
            pl.BlockSpec((None, 1, n), lambda j, bi, i: (n_layer, 0, 0)),
            mod_spec(n_layer, ctx_of, k_shift), mod_spec(n_layer, lat_of, k_shift),
            mod_spec(n_layer, ctx_of, k_scale), mod_spec(n_layer, lat_of, k_scale),
        ]
        args += [gains, mods4, mods4, mods4, mods4]
        out_specs.append(pl.BlockSpec((None, tm, tn), lambda j, bi, i: (bi, i, j)))
        out_shape.append(jax.ShapeDtypeStruct((b, r, n), BF16))
    res = pl.pallas_call(
        functools.partial(_linear_res_kernel, tm=tm, n_ctx=n_ctx, with_norm=norm is not None),
        grid=(n // tn, b, r // tm),
        in_specs=in_specs,
        out_specs=out_specs,
        out_shape=out_shape,
        compiler_params=_cparams(3),
        name=name,
    )(*args)
    return res if norm is not None else res[0]


def _swiglu_kernel(x_ref, wg_ref, wu_ref, o_ref):
    x = x_ref[...]
    g = jnp.dot(x, wg_ref[...].astype(BF16), preferred_element_type=F32)
    up = jnp.dot(x, wu_ref[...].astype(BF16), preferred_element_type=F32)
    o_ref[...] = (g * _sigmoid(g) * up).astype(o_ref.dtype)


def _swiglu_call(x, w3, layer, tm, tn):
    b, r, k = x.shape
    hid = w3.shape[-1] // 2
    nb = hid // tn
    return pl.pallas_call(
        _swiglu_kernel,
        grid=(nb, b, r // tm),
        in_specs=[
            pl.BlockSpec((None, tm, k), lambda j, bi, i: (bi, i, 0)),
            pl.BlockSpec((None, k, tn), lambda j, bi, i: (layer, 0, j)),
            pl.BlockSpec((None, k, tn), lambda j, bi, i: (layer, 0, nb + j)),
        ],
        out_specs=pl.BlockSpec((None, tm, tn), lambda j, bi, i: (bi, i, j)),
        out_shape=jax.ShapeDtypeStruct((b, r, hid), BF16),
        compiler_params=_cparams(3),
        name="ffn_in",
    )(x, w3, w3)


def _norm_rope(x, gain, cs, sn):
    var = jnp.mean(x * x, axis=-1, keepdims=True)
    y = x * lax.rsqrt(var + NORM_EPS) * gain
    if cs is None:
        return y
    lane = lax.broadcasted_iota(jnp.int32, y.shape, 1)
    nxt = pltpu.roll(y, ATT_HD - 1, 1)
    prv = pltpu.roll(y, 1, 1)
    partner = jnp.where(lane % 2 == 0, nxt, prv)
    return y * cs + partner * sn


def _attn_kernel(q_ref, k_ref, v_ref, qg_ref, kg_ref, cos_ref, sin_ref, o_ref, kt_s, vx_s, *, n_ctx, tq):
    qi = pl.program_id(2)
    rows = k_ref.shape[0]
    scale = ATT_HD ** -0.5 * float(np.log2(np.e))

    @pl.when(qi == 0)
    def _prepare_kv():
        kg = kg_ref[...]
        for c in range(rows // tq):
            kc = k_ref[c * tq:(c + 1) * tq, :].astype(F32)
            if c == 0:
                kn = _norm_rope(kc, kg, None, None)
            else:
                kn = _norm_rope(kc, kg, cos_ref[(c - 1) * tq:c * tq, :], sin_ref[(c - 1) * tq:c * tq, :])
            kt_s[:, c * tq:(c + 1) * tq] = kn.T.astype(BF16)
        lane = lax.broadcasted_iota(jnp.int32, (rows, ATT_HD), 1)
        vx_s[:, :ATT_HD] = v_ref[...]
        vx_s[:, ATT_HD:] = jnp.where(lane == 0, 1.0, 0.0).astype(BF16)

    def attend(cs, sn, n_keys):
        qg = qg_ref[...]
        scores = []
        for g in range(ATT_GROUP):
            q = _norm_rope(q_ref[:, g * ATT_HD:(g + 1) * ATT_HD].astype(F32), qg, cs, sn) * scale
            scores.append(jnp.dot(q.astype(BF16), kt_s[:, :n_keys], preferred_element_type=F32))
        for g in range(ATT_GROUP):
            s = scores[g]
            p = jnp.exp2(s - jnp.max(s, axis=-1, keepdims=True))
            ox = jnp.dot(p.astype(BF16), vx_s[:n_keys, :], preferred_element_type=F32)
            o = ox[:, :ATT_HD] / ox[:, ATT_HD:ATT_HD + 1]
            o_ref[:, g * ATT_HD:(g + 1) * ATT_HD] = o.astype(o_ref.dtype)

    @pl.when(qi == 0)
    def _context_queries():
        attend(None, None, n_ctx)

    @pl.when(qi > 0)
    def _latent_queries():
        off = pl.multiple_of((qi - 1) * tq, tq)
        attend(cos_ref[pl.ds(off, tq), :], sin_ref[pl.ds(off, tq), :], rows)


def _attn_call(p, q_g, k_g, cos_t, sin_t, layer, n_ctx):
    b, r, _ = p.shape
    tq = n_ctx
    t = r - n_ctx
    qw = ATT_GROUP * ATT_HD
    return pl.pallas_call(
        functools.partial(_attn_kernel, n_ctx=n_ctx, tq=tq),
        grid=(b, ATT_KV_HEADS, r // tq),
        in_specs=[
            pl.BlockSpec((None, tq, qw), lambda bi, kv, qi: (bi, qi, COL_AQ // qw + kv)),
            pl.BlockSpec((None, r, ATT_HD), lambda bi, kv, qi: (bi, 0, COL_AK // ATT_HD + kv)),
            pl.BlockSpec((None, r, ATT_HD), lambda bi, kv, qi: (bi, 0, COL_AV // ATT_HD + kv)),
            pl.BlockSpec((None, 1, ATT_HD), lambda bi, kv, qi: (layer, 0, 0)),
            pl.BlockSpec((None, 1, ATT_HD), lambda bi, kv, qi: (layer, 0, 0)),
            pl.BlockSpec((t, ATT_HD), lambda bi, kv, qi: (0, 0)),
            pl.BlockSpec((t, ATT_HD), lambda bi, kv, qi: (0, 0)),
        ],
        out_specs=pl.BlockSpec((None, tq, qw), lambda bi, kv, qi: (bi, qi, kv)),
        out_shape=jax.ShapeDtypeStruct((b, r, ATT_Q), BF16),
        scratch_shapes=[pltpu.VMEM((ATT_HD, r), BF16), pltpu.VMEM((r, 2 * ATT_HD), BF16)],
        compiler_params=_cparams(3),
        name="gqa",
    )(p, p, p, q_g, k_g, cos_t, sin_t)


def _gla_direction(q, k, v, dec, wd, bd, vis, st_s, reverse, n_chunks):
    c_len = GLA_CHUNK

    tri = vis > 0.5
    tri_b = vis.astype(BF16)
    nt = (((1,), (1,)), ((), ()))
    tn = (((0,), (0,)), ((), ()))

    x = jnp.dot(dec, wd, preferred_element_type=F32) + bd
    g = (jnp.minimum(x, 0.0) - jnp.log(1.0 + jnp.exp(-jnp.abs(x)))) * (1.0 / GLA_GATE_NORM)
    g_hi = g.astype(BF16)
    g_lo = (g - g_hi.astype(F32)).astype(BF16)
    bcum = jnp.dot(tri_b, g_hi, preferred_element_type=F32) + jnp.dot(tri_b, g_lo, preferred_element_type=F32)
    q = q.astype(F32) * (GLA_HDK ** -0.5)
    k = k.astype(F32)
    q_dec = (q * jnp.exp(bcum)).astype(BF16)
    k_inv = (k * jnp.exp(-bcum)).astype(BF16)
    att = lax.dot_general(q_dec, k_inv, nt, preferred_element_type=F32)
    att = jnp.where(tri, att, 0.0).astype(BF16)
    o_intra = jnp.dot(att, v, preferred_element_type=F32)
    b_last, k_dec = [], []
    for c in range(n_chunks):
        sl = slice(c * c_len, (c + 1) * c_len)
        end = c * c_len if reverse else (c + 1) * c_len - 1
        b_last.append(bcum[end:end + 1, :])
        k_dec.append((k[sl, :] * jnp.exp(b_last[c] - bcum[sl, :])).astype(BF16))

    outs = [None] * n_chunks
    chunk_order = range(n_chunks - 1, -1, -1) if reverse else range(n_chunks)
    st = st_s[...]
    for c in chunk_order:
        sl = slice(c * c_len, (c + 1) * c_len)
        outs[c] = o_intra[sl, :] + lax.dot_general(q_dec[sl, :], st.astype(BF16), nt, preferred_element_type=F32)
        st = st * jnp.exp(b_last[c]) + lax.dot_general(v[sl, :], k_dec[c], tn, preferred_element_type=F32)
    st_s[...] = st
    return jnp.concatenate(outs, axis=0)


def _gla_kernel(qf_ref, kf_ref, vf_ref, decf_ref, qb_ref, kb_ref, vb_ref, decb_ref,
                wdf_ref, bdf_ref, wdb_ref, bdb_ref, visf_ref, visb_ref, r_ref, ng_ref, o_ref,
                stf_s, stb_s, of_s, ob_s, *, n_chunks, nblk):
    t = pl.program_id(2)
    tb = n_chunks * GLA_CHUNK

    @pl.when(t == 0)
    def _zero_states():
        stf_s[...] = jnp.zeros_like(stf_s)
        stb_s[...] = jnp.zeros_like(stb_s)

    blk_b = jnp.where(t == 0, 0, nblk - t)
    rows_f = pl.ds(pl.multiple_of(t * tb, tb), tb)
    rows_b = pl.ds(pl.multiple_of(blk_b * tb, tb), tb)
    decf = decf_ref[...].astype(BF16)
    decb = decb_ref[...].astype(BF16)
    visf = visf_ref[...]
    visb = visb_ref[...]
    for hh in range(GLA_HEADS_PER_STEP):
        kcol = slice(hh * GLA_HDK, (hh + 1) * GLA_HDK)
        vcol = slice(hh * GLA_HDV, (hh + 1) * GLA_HDV)
        of_s[rows_f, vcol] = _gla_direction(
            qf_ref[:, kcol], kf_ref[:, kcol], vf_ref[:, vcol], decf, wdf_ref[:, kcol].astype(BF16),
            bdf_ref[:, kcol], visf, stf_s.at[hh], False, n_chunks)
        ob_s[rows_b, vcol] = _gla_direction(
            qb_ref[:, kcol], kb_ref[:, kcol], vb_ref[:, vcol], decb, wdb_ref[:, kcol].astype(BF16),
            bdb_ref[:, kcol], visb, stb_s.at[hh], True, n_chunks)

    @pl.when(t == nblk - 1)
    def _combine_norm_gate():
        for c in range(nblk):
            sl = slice(c * tb, (c + 1) * tb)
            for hh in range(GLA_HEADS_PER_STEP):
                vcol = slice(hh * GLA_HDV, (hh + 1) * GLA_HDV)
                o = of_s[sl, vcol] + ob_s[sl, vcol]
                var = jnp.mean(o * o, axis=-1, keepdims=True)
                y = o * lax.rsqrt(var + NORM_EPS) * ng_ref[...]
                r = r_ref[sl, vcol].astype(F32)
                o_ref[sl, vcol] = (y * _silu(r)).astype(o_ref.dtype)


def _gla_call(p, dec, wd_pad, bd, norm_g, layer, n_ctx):
    b, r, _ = p.shape
    tb = n_ctx
    nblk = r // tb

    def blk(t, direction):
        if direction == 0:
            return t
        return jnp.where(t == 0, 0, nblk - t)

    kw = GLA_HEADS_PER_STEP * GLA_HDK
    vw = GLA_HEADS_PER_STEP * GLA_HDV

    def stream_specs(direction):
        return [
            pl.BlockSpec((None, tb, kw), lambda bi, h, t: (bi, blk(t, direction), COL_GQ // kw + h)),
            pl.BlockSpec((None, tb, kw), lambda bi, h, t: (bi, blk(t, direction), COL_GK // kw + h)),
            pl.BlockSpec((None, tb, vw), lambda bi, h, t: (bi, blk(t, direction), COL_GV // vw + h)),
            pl.BlockSpec((None, tb, DEC_PAD), lambda bi, h, t: (bi, blk(t, direction), 0)),
        ]

    def decay_specs(direction):
        return [
            pl.BlockSpec((None, None, DEC_PAD, kw), lambda bi, h, t: (layer, direction, 0, h)),
            pl.BlockSpec((None, None, 1, kw), lambda bi, h, t: (layer, direction, 0, h)),
        ]

    idx = np.arange(tb)
    same_chunk = (idx[:, None] // GLA_CHUNK) == (idx[None, :] // GLA_CHUNK)
    vis = jnp.asarray(np.stack([same_chunk & (idx[:, None] >= idx[None, :]),
                                same_chunk & (idx[None, :] >= idx[:, None])]).astype(np.float32))

    return pl.pallas_call(
        functools.partial(_gla_kernel, n_chunks=tb // GLA_CHUNK, nblk=nblk),
        grid=(b, GLA_HEADS // GLA_HEADS_PER_STEP, nblk),
        in_specs=stream_specs(0) + stream_specs(1) + decay_specs(0) + decay_specs(1) + [
            pl.BlockSpec((None, tb, tb), lambda bi, h, t: (0, 0, 0)),
            pl.BlockSpec((None, tb, tb), lambda bi, h, t: (1, 0, 0)),
            pl.BlockSpec((None, r, vw), lambda bi, h, t: (bi, 0, COL_GR // vw + h)),
            pl.BlockSpec((None, 1, GLA_HDV), lambda bi, h, t: (layer, 0, 0)),
        ],
        out_specs=pl.BlockSpec((None, r, vw), lambda bi, h, t: (bi, 0, h)),
        out_shape=jax.ShapeDtypeStruct((b, r, GLA_DV), BF16),
        scratch_shapes=([pltpu.VMEM((GLA_HEADS_PER_STEP, GLA_HDV, GLA_HDK), F32)] * 2
                        + [pltpu.VMEM((r, vw), F32)] * 2),
        compiler_params=_cparams(3),
        name="gla",
    )(p, p, p, dec, p, p, p, dec, wd_pad, bd, wd_pad, bd, vis, vis, p, norm_g)


LRU_PAD = 8


def _lru_kernel(x_ref, y_ref, cw_ref, cb_ref, wa_ref, ba_ref, wi_ref, bi_ref, lam_ref, o_ref,
                xp_s, af_s, bf_s, ab_s, bb_s, *, n_ctx, tc, tr):
    rows = x_ref.shape[0]
    n_lat = rows - n_ctx
    sub = 8
    ctx0 = LRU_PAD
    lat0 = 2 * LRU_PAD + n_ctx
    zpad = jnp.zeros((LRU_PAD, tc), F32)
    xp_s[0:LRU_PAD, :] = zpad
    xp_s[ctx0 + n_ctx:lat0, :] = zpad
    xp_s[lat0 + n_lat:lat0 + n_lat + LRU_PAD, :] = zpad
    xp_s[ctx0:ctx0 + n_ctx, :] = x_ref[0:n_ctx, :].astype(F32)
    xp_s[lat0:lat0 + n_lat, :] = x_ref[n_ctx:rows, :].astype(F32)

    cw = cw_ref[...]
    cb = cb_ref[...]
    lam = lam_ref[...]
    neg_sp = -(jnp.maximum(-lam, 0.0) + jnp.log1p(jnp.exp(-jnp.abs(lam))))
    log2_a_half = (0.5 * LRU_C * float(np.log2(np.e))) * neg_sp
    row_id = lax.broadcasted_iota(jnp.int32, (tr, 1), 0)
    n_kb = tc // LRU_BS
    wa_h = [[(0.5 * wa_ref[d, kb]).astype(BF16) for kb in range(n_kb)] for d in range(2)]
    wi_h = [[(0.5 * wi_ref[d, kb]).astype(BF16) for kb in range(n_kb)] for d in range(2)]
    ba_h = 0.5 * ba_ref[...]
    bi_h = 0.5 * bi_ref[...]

    for c in range(rows // tr):
        r0 = c * tr
        p0 = (ctx0 if r0 < n_ctx else lat0 - n_ctx) + r0
        n_win = tr + 2 * sub
        win = xp_s[p0 - sub:p0 + tr + sub, :]
        xc = cb + win[sub:sub + tr, :] * cw[CONV_LEFT:CONV_LEFT + 1, :]
        for j in range(CONV_W):
            off = j - CONV_LEFT
            if off != 0:
                xc = xc + pltpu.roll(win, (-off) % n_win, 0)[sub:sub + tr, :] * cw[j:j + 1, :]
        xh = 0.5 * xc
        xb = [xc[:, kb * LRU_BS:(kb + 1) * LRU_BS].astype(BF16) for kb in range(n_kb)]
        for d, (a_s, b_s) in enumerate(((af_s, bf_s), (ab_s, bb_s))):
            za = jnp.concatenate([jnp.dot(xb[kb], wa_h[d][kb], preferred_element_type=F32)
                                  for kb in range(n_kb)], axis=1)
            zi = jnp.concatenate([jnp.dot(xb[kb], wi_h[d][kb], preferred_element_type=F32)
                                  for kb in range(n_kb)], axis=1)
            ta = jnp.tanh(za + ba_h[d:d + 1, :])
            ti = jnp.tanh(zi + bi_h[d:d + 1, :])
            la_h = log2_a_half[d:d + 1, :]
            a = jnp.exp2(la_h + la_h * ta)
            q = 1.0 - a * a
            mult = jnp.where(q > 0.0, q * lax.rsqrt(q), 0.0)
            first = 0 if d == 0 else n_ctx - 1
            if r0 <= first < r0 + tr:
                mult = jnp.where(row_id == first - r0, 1.0, mult)
            a_s[r0:r0 + tr, :] = a
            b_s[r0:r0 + tr, :] = (mult * xh) * (1.0 + ti)

    row8 = lax.broadcasted_iota(jnp.int32, (sub, tc), 0)

    def tile_maps(a, b, reverse):
        for dlt in (1, 2, 4):
            shift = sub - dlt if reverse else dlt
            valid = (row8 < sub - dlt) if reverse else (row8 >= dlt)
            a_sh = jnp.where(valid, pltpu.roll(a, shift, 0), 1.0)
            b_sh = jnp.where(valid, pltpu.roll(b, shift, 0), 0.0)
            b = b + a * b_sh
            a = a * a_sh
        return a, b

    def scan_tile(tile_f, tile_b, carry):
        hf, hb = carry
        rf = pl.ds(pl.multiple_of(tile_f * sub, sub), sub)
        rb = pl.ds(pl.multiple_of(tile_b * sub, sub), sub)
        a_f, b_f = tile_maps(af_s[rf, :], bf_s[rf, :], False)
        a_b, b_b = tile_maps(ab_s[rb, :], bb_s[rb, :], True)
        h_f = b_f + a_f * hf
        h_b = b_b + a_b * hb
        bf_s[rf, :] = h_f
        bb_s[rb, :] = h_b
        return h_f[sub - 1:sub, :], h_b[0:1, :]

    zero = jnp.zeros((1, tc), F32)
    ctx_tiles = n_ctx // sub
    all_tiles = rows // sub
    carry = lax.fori_loop(0, ctx_tiles, lambda i, cr: scan_tile(i, ctx_tiles - 1 - i, cr), (zero, zero),
                          unroll=2)
    lax.fori_loop(ctx_tiles, all_tiles, lambda i, cr: scan_tile(i, all_tiles - 1 + ctx_tiles - i, cr), carry,
                  unroll=2)

    k0 = 0.7978845608028654
    for c in range(rows // tr):
        sl = slice(c * tr, (c + 1) * tr)
        y = y_ref[sl, :].astype(F32)
        gelu = 0.5 * y * (1.0 + jnp.tanh(k0 * (y + 0.044715 * (y * y * y))))
        o_ref[sl, :] = ((bf_s[sl, :] + bb_s[sl, :]) * gelu).astype(o_ref.dtype)


def _lru_call(p, conv_w, conv_b, w_a, b_a, w_i, b_i, lam, layer, n_ctx):
    b, r, _ = p.shape
    tc = 512
    nb = tc // LRU_BS
    chan = lambda bi, j: (layer, 0, j)
    return pl.pallas_call(
        functools.partial(_lru_kernel, n_ctx=n_ctx, tc=tc, tr=n_ctx),
        grid=(b, LRU_W // tc),
        in_specs=[
            pl.BlockSpec((None, r, tc), lambda bi, j: (bi, 0, COL_LX // tc + j)),
            pl.BlockSpec((None, r, tc), lambda bi, j: (bi, 0, COL_LY // tc + j)),
            pl.BlockSpec((None, CONV_W, tc), chan),
            pl.BlockSpec((None, 1, tc), chan),
            pl.BlockSpec((None, 2, nb, LRU_BS, LRU_BS), lambda bi, j: (layer, 0, j, 0, 0)),
            pl.BlockSpec((None, 2, tc), chan),
            pl.BlockSpec((None, 2, nb, LRU_BS, LRU_BS), lambda bi, j: (layer, 0, j, 0, 0)),
            pl.BlockSpec((None, 2, tc), chan),
            pl.BlockSpec((None, 2, tc), chan),
        ],
        out_specs=pl.BlockSpec((None, r, tc), lambda bi, j: (bi, 0, j)),
        out_shape=jax.ShapeDtypeStruct((b, r, LRU_W), BF16),
        scratch_shapes=[pltpu.VMEM((r + 3 * LRU_PAD, tc), F32)] + [pltpu.VMEM((r, tc), F32)] * 4,
        compiler_params=_cparams(2),
        name="rglru",
    )(p, p, conv_w, conv_b, w_a, b_a, w_i, b_i, lam)


def _merge_kernel(xa_ref, xb_ref, xc_ref, wa_ref, wb_ref, wc_ref, ga_ref, gb_ref, gc_ref, bm_ref, o_ref):
    acc = None
    for n, (x_ref, w_ref, g_ref) in enumerate(((xa_ref, wa_ref, ga_ref), (xb_ref, wb_ref, gb_ref),
                                               (xc_ref, wc_ref, gc_ref))):
        proj = jnp.dot(x_ref[...], w_ref[...], preferred_element_type=F32)
        term = _sigmoid(g_ref[...].astype(F32) + bm_ref[n:n + 1, :]) * proj
        acc = term if acc is None else acc + term
    o_ref[...] = acc.astype(o_ref.dtype)


def _merge_call(o_gla, o_att, o_lru, wbr, p, b_merge, layer, tm, tn):
    b, r, d = o_gla.shape
    ncol = d // tn
    x_spec = pl.BlockSpec((None, tm, d), lambda j, bi, i: (bi, i, 0))

    def w_spec(n):
        return pl.BlockSpec((None, None, d, tn), lambda j, bi, i: (layer, n, 0, j))

    def g_spec(n):
        return pl.BlockSpec((None, tm, tn), lambda j, bi, i: (bi, i, COL_GT // tn + n * ncol + j))

    return pl.pallas_call(
        _merge_kernel,
        grid=(ncol, b, r // tm),
        in_specs=[x_spec, x_spec, x_spec, w_spec(0), w_spec(1), w_spec(2), g_spec(0), g_spec(1), g_spec(2),
                  pl.BlockSpec((None, N_BRANCH, tn), lambda j, bi, i: (layer, 0, j))],
        out_specs=pl.BlockSpec((None, tm, tn), lambda j, bi, i: (bi, i, j)),
        out_shape=jax.ShapeDtypeStruct((b, r, d), BF16),
        compiler_params=_cparams(3),
        name="merge",
    )(o_gla, o_att, o_lru, wbr, wbr, wbr, p, p, p, b_merge)


def _final_norm_kernel(h_ref, g_ref, o_ref):
    x = h_ref[...]
    var = jnp.mean(x * x, axis=-1, keepdims=True)
    o_ref[...] = x * lax.rsqrt(var + NORM_EPS) * g_ref[...]


def _final_norm_call(h, g, n_ctx):
    b, r, d = h.shape
    tr = n_ctx
    return pl.pallas_call(
        _final_norm_kernel,
        grid=(b, (r - n_ctx) // tr),
        in_specs=[
            pl.BlockSpec((None, tr, d), lambda bi, i: (bi, i + 1, 0)),
            pl.BlockSpec((1, d), lambda bi, i: (0, 0)),
        ],
        out_specs=pl.BlockSpec((None, tr, d), lambda bi, i: (bi, i, 0)),
        out_shape=jax.ShapeDtypeStruct((b, r - n_ctx, d), F32),
        compiler_params=_cparams(2),
        name="final_norm",
    )(h, g)


def _rope_tables(n_tokens):
    rows = n_tokens // GRID_W
    row = jnp.repeat(jnp.arange(rows, dtype=F32), GRID_W)
    col = jnp.tile(jnp.arange(GRID_W, dtype=F32), rows)
    inv = ROPE_THETA ** (-jnp.arange(ROPE_AXIS_PAIRS, dtype=F32) / ROPE_AXIS_PAIRS)
    ang = jnp.concatenate([row[:, None] * inv, col[:, None] * inv], axis=-1)
    cos, sin = jnp.cos(ang), jnp.sin(ang)
    cos_t = jnp.repeat(cos, 2, axis=-1)
    sin_t = jnp.stack([-sin, sin], axis=-1).reshape(n_tokens, ATT_HD)
    return cos_t, sin_t


def kernel(x, c, ctx, c_ctx, w_mod, b_mod, norm_mix_g, norm_ffn_g, w_in, gla_w_decay, gla_b_decay, gla_norm_g, q_norm_g, k_norm_g, conv_w, conv_b, lru_w_a, lru_b_a, lru_w_i, lru_b_i, lru_lambda, b_merge, w_branch, w_out, w_ffn_in, w_ffn_out, final_norm_g):
    depth = w_in.shape[0]
    batch, t_lat, d = x.shape
    n_ctx = ctx.shape[1]
    rows = n_ctx + t_lat
    ctx_row = batch
    assert batch < MOD_ROWS and rows % n_ctx == 0 and n_ctx % GLA_CHUNK == 0
    tm = 768 if rows % 768 == 0 else n_ctx
    tm_big = 1152 if rows % 1152 == 0 else tm
    tm_small = 384 if rows % 384 == 0 else n_ctx
    tn_in, tn_merge, tn_out, tn_ffn_in, tn_ffn_out = 1024, 512, d, 512, 1024

    w_in_t = jnp.swapaxes(w_in, 1, 2)
    w_branch_b = w_branch.astype(BF16)
    w_out_b = w_out.astype(BF16)
    w_ffn_out_b = w_ffn_out.astype(BF16)
    wd_pad = jnp.zeros((depth, 2, DEC_PAD, GLA_DK), F32)
    wd_pad = wd_pad.at[:, 0, :GLA_RANK].set(gla_w_decay[:, 0])
    wd_pad = wd_pad.at[:, 1, GLA_RANK:2 * GLA_RANK].set(gla_w_decay[:, 1])
    bd = gla_b_decay.reshape(depth, 2, 1, GLA_DK)
    cos_t, sin_t = _rope_tables(t_lat)

    c16 = jnp.zeros((MOD_ROWS, d), F32).at[:batch].set(c).at[ctx_row].set(c_ctx)
    mods4 = _mods_call(c16, w_mod, b_mod).reshape(depth, MOD_ROWS, 1, N_MOD * d)

    g_mix = norm_mix_g.reshape(depth, 1, d)
    g_ffn = norm_ffn_g.reshape(depth, 1, d)
    g_gla = gla_norm_g.reshape(depth, 1, GLA_HDV)
    g_q = q_norm_g.reshape(depth, 1, ATT_HD)
    g_k = k_norm_g.reshape(depth, 1, ATT_HD)
    cb3 = conv_b.reshape(depth, 1, LRU_W)

    h = jnp.concatenate([ctx, x], axis=1)
    for l in range(depth):
        u, dec = _normmod_call(h, g_mix, mods4, w_in_t, l, 0, 1, n_ctx, ctx_row, tm)
        p = _in_proj_call(u, w_in_t, l, tm_big, tn_in)
        o_att = _attn_call(p, g_q, g_k, cos_t, sin_t, l, n_ctx)
        o_gla = _gla_call(p, dec, wd_pad, bd, g_gla, l, n_ctx)
        o_lru = _lru_call(p, conv_w, cb3, lru_w_a, lru_b_a, lru_w_i, lru_b_i, lru_lambda, l, n_ctx)
        merged = _merge_call(o_gla, o_att, o_lru, w_branch_b, p, b_merge, l, tm, tn_merge)
        h, u2 = _linear_res_call(merged, w_out_b, l, h, mods4, 2, n_ctx, ctx_row, tm, tn_out, "out_proj",
                                 norm=(g_ffn, l, 3, 4))
        a = _swiglu_call(u2, w_ffn_in, l, tm_big, tn_ffn_in)
        h = _linear_res_call(a, w_ffn_out_b, l, h, mods4, 5, n_ctx, ctx_row, tm_small, tn_ffn_out, "ffn_out")
    return _final_norm_call(h, final_norm_g.reshape(1, d), n_ctx)
```

```python
import functools

import jax
import jax.numpy as jnp
import numpy as np
from jax import lax
from jax.experimental import pallas as pl
from jax.experimental.pallas import tpu as pltpu

F32 = jnp.float32
BF16 = jnp.bfloat16

D_MODEL = 2048
GRID_W = 64
NORM_EPS = 1e-6
N_MOD = 6

GLA_HEADS = 4
GLA_DK = D_MODEL // 2
GLA_DV = D_MODEL
GLA_HDK = GLA_DK // GLA_HEADS
GLA_HDV = GLA_DV // GLA_HEADS
GLA_RANK = 16
GLA_GATE_NORM = 16.0
GLA_CHUNK = 64
GLA_HEADS_PER_STEP = 2

ATT_HEADS = 16
ATT_KV_HEADS = 4
ATT_HD = 128
ATT_GROUP = ATT_HEADS // ATT_KV_HEADS
ATT_Q = ATT_HEADS * ATT_HD
ATT_KV = ATT_KV_HEADS * ATT_HD
ROPE_THETA = 10000.0
ROPE_AXIS_PAIRS = ATT_HD // 4

LRU_W = D_MODEL
LRU_BLOCKS = 16
LRU_BS = LRU_W // LRU_BLOCKS
LRU_C = 8.0
CONV_W = 4
CONV_LEFT = 2

N_BRANCH = 3
FFN_HIDDEN = -(-8 * D_MODEL // (3 * 256)) * 256

COL_GQ = 0
COL_GK = COL_GQ + GLA_DK
COL_GV = COL_GK + GLA_DK
COL_GR = COL_GV + GLA_DV
COL_AQ = COL_GR + GLA_DV
COL_AK = COL_AQ + ATT_Q
COL_AV = COL_AK + ATT_KV
COL_LX = COL_AV + ATT_KV
COL_LY = COL_LX + LRU_W
COL_GT = COL_LY + LRU_W
N_MAIN = COL_GT + N_BRANCH * D_MODEL
DEC_ORIG = 2 * GLA_DK + 2 * GLA_DV
DEC_PAD = 128

V7X_VMEM_LIMIT_BYTES = 56 * 1024 * 1024
MOD_ROWS = 16


def _cparams(n_axes, vmem=V7X_VMEM_LIMIT_BYTES):
    return pltpu.CompilerParams(dimension_semantics=("arbitrary",) * n_axes, vmem_limit_bytes=vmem)


def _sigmoid(x):
    return 0.5 * jnp.tanh(0.5 * x) + 0.5


def _silu(x):
    h = 0.5 * x
    return h + h * jnp.tanh(h)


def _mods_kernel(c_ref, w_ref, b_ref, o_ref):
    c = c_ref[...]
    sc = _silu(c).astype(BF16)
    o_ref[...] = jnp.dot(sc, w_ref[...].astype(BF16), preferred_element_type=F32) + b_ref[...]


def _mods_call(c16, w_mod, b_mod):
    depth, d, n = w_mod.shape
    tn = 1024
    return pl.pallas_call(
        _mods_kernel,
        grid=(depth, n // tn),
        in_specs=[
            pl.BlockSpec((MOD_ROWS, d), lambda l, j: (0, 0)),
            pl.BlockSpec((None, d, tn), lambda l, j: (l, 0, j)),
            pl.BlockSpec((None, 1, tn), lambda l, j: (l, 0, j)),
        ],
        out_specs=pl.BlockSpec((None, MOD_ROWS, tn), lambda l, j: (l, 0, j)),
        out_shape=jax.ShapeDtypeStruct((depth, MOD_ROWS, n), F32),
        compiler_params=_cparams(2),
        name="mods",
    )(c16, w_mod, b_mod.reshape(depth, 1, n))


def _normmod_kernel(h_ref, g_ref, shc_ref, shl_ref, scc_ref, scl_ref, wt_ref, o_ref, dec_ref, *, tr, n_ctx):
    x = h_ref[...]
    var = jnp.mean(x * x, axis=-1, keepdims=True)
    y = x * lax.rsqrt(var + NORM_EPS) * g_ref[...]
    is_ctx = (pl.program_id(1) * tr + lax.broadcasted_iota(jnp.int32, (tr, 1), 0)) < n_ctx
    shift = jnp.where(is_ctx, shc_ref[...], shl_ref[...])
    scale = jnp.where(is_ctx, scc_ref[...], scl_ref[...])
    u = (y * (1.0 + scale) + shift).astype(o_ref.dtype)
    o_ref[...] = u
    dec_ref[...] = lax.dot_general(u, wt_ref[...].astype(BF16), (((1,), (1,)), ((), ())),
                                   preferred_element_type=F32)


def _normmod_call(h, g, mods4, w_in_t, layer, k_shift, k_scale, n_ctx, ctx_row, tr):
    b, r, d = h.shape

    def mod_spec(k, is_ctx):
        return pl.BlockSpec((None, None, 1, d), lambda bi, i: (layer, ctx_row if is_ctx else bi, 0, k))

    return pl.pallas_call(
        functools.partial(_normmod_kernel, tr=tr, n_ctx=n_ctx),
        grid=(b, r // tr),
        in_specs=[
            pl.BlockSpec((None, tr, d), lambda bi, i: (bi, i, 0)),
            pl.BlockSpec((None, 1, d), lambda bi, i: (layer, 0, 0)),
            mod_spec(k_shift, True), mod_spec(k_shift, False),
            mod_spec(k_scale, True), mod_spec(k_scale, False),
            pl.BlockSpec((None, DEC_PAD, d), lambda bi, i: (layer, DEC_ORIG // DEC_PAD, 0)),
        ],
        out_specs=[pl.BlockSpec((None, tr, d), lambda bi, i: (bi, i, 0)),
                   pl.BlockSpec((None, tr, DEC_PAD), lambda bi, i: (bi, i, 0))],
        out_shape=[jax.ShapeDtypeStruct((b, r, d), BF16), jax.ShapeDtypeStruct((b, r, DEC_PAD), F32)],
        compiler_params=_cparams(2),
        name="normmod",
    )(h, g, mods4, mods4, mods4, mods4, w_in_t)


def _in_proj_kernel(x_ref, wa_ref, wb_ref, o_ref, *, n_plain, shift):
    j = pl.program_id(0)
    nt = (((1,), (1,)), ((), ()))

    @pl.when(j < n_plain)
    def _before_decay_columns():
        w = wa_ref[...].astype(BF16)
        o_ref[...] = lax.dot_general(x_ref[...], w, nt, preferred_element_type=F32).astype(o_ref.dtype)

    @pl.when(j >= n_plain)
    def _after_decay_columns():
        w = jnp.concatenate([wa_ref[shift:, :], wb_ref[...]], axis=0).astype(BF16)
        o_ref[...] = lax.dot_general(x_ref[...], w, nt, preferred_element_type=F32).astype(o_ref.dtype)


def _in_proj_call(x, w_in_t, layer, tm, tn):
    b, r, k = x.shape
    shift = 2 * GLA_RANK
    assert DEC_ORIG % tn == 0 and N_MAIN % tn == 0 and tn % shift == 0
    return pl.pallas_call(
        functools.partial(_in_proj_kernel, n_plain=DEC_ORIG // tn, shift=shift),
        grid=(N_MAIN // tn, b, r // tm),
        in_specs=[
            pl.BlockSpec((None, tm, k), lambda j, bi, i: (bi, i, 0)),
            pl.BlockSpec((None, tn, k), lambda j, bi, i: (layer, j, 0)),
            pl.BlockSpec((None, shift, k), lambda j, bi, i: (layer, (j + 1) * (tn // shift), 0)),
        ],
        out_specs=pl.BlockSpec((None, tm, tn), lambda j, bi, i: (bi, i, j)),
        out_shape=jax.ShapeDtypeStruct((b, r, N_MAIN), BF16),
        compiler_params=_cparams(3),
        name="in_proj",
    )(x, w_in_t, w_in_t)


def _linear_res_kernel(*refs, tm, n_ctx, with_norm):
    x_ref, w_ref, h_ref, gc_ref, gl_ref = refs[:5]
    if with_norm:
        ng_ref, shc_ref, shl_ref, scc_ref, scl_ref, o_ref, u_ref = refs[5:]
    else:
        (o_ref,) = refs[5:]
    i = pl.program_id(2)
    y = jnp.dot(x_ref[...], w_ref[...], preferred_element_type=F32)
    is_ctx = (i * tm + lax.broadcasted_iota(jnp.int32, (tm, 1), 0)) < n_ctx
    h_new = h_ref[...] + jnp.where(is_ctx, gc_ref[...], gl_ref[...]) * y
    o_ref[...] = h_new
    if with_norm:
        var = jnp.mean(h_new * h_new, axis=-1, keepdims=True)
        yn = h_new * lax.rsqrt(var + NORM_EPS) * ng_ref[...]
        shift = jnp.where(is_ctx, shc_ref[...], shl_ref[...])
        scale = jnp.where(is_ctx, scc_ref[...], scl_ref[...])
        u_ref[...] = (yn * (1.0 + scale) + shift).astype(u_ref.dtype)


def _linear_res_call(x, w3, layer, h, mods4, k_gate, n_ctx, ctx_row, tm, tn, name, norm=None):
    b, r, k = x.shape
    n = w3.shape[-1]
    ngate = n // tn

    def mod_spec(lyr, row_of, kk):
        return pl.BlockSpec((None, None, 1, tn), lambda j, bi, i: (lyr, row_of(bi), 0, kk * ngate + j))

    ctx_of = lambda bi: ctx_row
    lat_of = lambda bi: bi
    w_mode = dict(pipeline_mode=pl.Buffered(1)) if tn == n else {}
    in_specs = [
        pl.BlockSpec((None, tm, k), lambda j, bi, i: (bi, i, 0)),
        pl.BlockSpec((None, k, tn), lambda j, bi, i: (layer, 0, j), **w_mode),
        pl.BlockSpec((None, tm, tn), lambda j, bi, i: (bi, i, j)),
        mod_spec(layer, ctx_of, k_gate),
        mod_spec(layer, lat_of, k_gate),
    ]
    args = [x, w3, h, mods4, mods4]
    out_specs = [pl.BlockSpec((None, tm, tn), lambda j, bi, i: (bi, i, j))]
    out_shape = [jax.ShapeDtypeStruct((b, r, n), F32)]
    if norm is not None:
        gains, n_layer, k_shift, k_scale = norm
        assert tn == n
        in_specs += [
            pl.BlockSpec((None, 1, n), lambda j, bi, i: (n_layer, 0, 0)),
            mod_spec(n_layer, ctx_of, k_shift), mod_spec(n_layer, lat_of, k_shift),
            mod_spec(n_layer, ctx_of, k_scale), mod_spec(n_layer, lat_of, k_scale),
        ]
        args += [gains, mods4, mods4, mods4, mods4]
        out_specs.append(pl.BlockSpec((None, tm, tn), lambda j, bi, i: (bi, i, j)))
        out_shape.append(jax.ShapeDtypeStruct((b, r, n), BF16))
    res = pl.pallas_call(
        functools.partial(_linear_res_kernel, tm=tm, n_ctx=n_ctx, with_norm=norm is not None),
        grid=(n // tn, b, r // tm),
        in_specs=in_specs,
        out_specs=out_specs,
        out_shape=out_shape,
        compiler_params=_cparams(3),
        name=name,
    )(*args)
    return res if norm is not None else res[0]


def _swiglu_kernel(x_ref, wg_ref, wu_ref, o_ref):
    x = x_ref[...]
    g = jnp.dot(x, wg_ref[...].astype(BF16), preferred_element_type=F32)
    up = jnp.dot(x, wu_ref[...].astype(BF16), preferred_element_type=F32)
    o_ref[...] = (g * _sigmoid(g) * up).astype(o_ref.dtype)


def _swiglu_call(x, w3, layer, tm, tn):
    b, r, k = x.shape
    hid = w3.shape[-1] // 2
    nb = hid // tn
    return pl.pallas_call(
        _swiglu_kernel,
        grid=(nb, b, r // tm),
        in_specs=[
            pl.BlockSpec((None, tm, k), lambda j, bi, i: (bi, i, 0)),
            pl.BlockSpec((None, k, tn), lambda j, bi, i: (layer, 0, j)),
            pl.BlockSpec((None, k, tn), lambda j, bi, i: (layer, 0, nb + j)),
        ],
        out_specs=pl.BlockSpec((None, tm, tn), lambda j, bi, i: (bi, i, j)),
        out_shape=jax.ShapeDtypeStruct((b, r, hid), BF16),
        compiler_params=_cparams(3),
        name="ffn_in",
    )(x, w3, w3)


def _norm_rope(x, gain, cs, sn):
    var = jnp.mean(x * x, axis=-1, keepdims=True)
    y = x * lax.rsqrt(var + NORM_EPS) * gain
    if cs is None:
        return y
    lane = lax.broadcasted_iota(jnp.int32, y.shape, 1)
    nxt = pltpu.roll(y, ATT_HD - 1, 1)
    prv = pltpu.roll(y, 1, 1)
    partner = jnp.where(lane % 2 == 0, nxt, prv)
    return y * cs + partner * sn


def _attn_kernel(q_ref, k_ref, v_ref, qg_ref, kg_ref, cos_ref, sin_ref, o_ref, kt_s, vx_s, *, n_ctx, tq):
    qi = pl.program_id(2)
    rows = k_ref.shape[0]
    scale = ATT_HD ** -0.5 * float(np.log2(np.e))

    @pl.when(qi == 0)
    def _prepare_kv():
        kg = kg_ref[...]
        for c in range(rows // tq):
            kc = k_ref[c * tq:(c + 1) * tq, :].astype(F32)
            if c == 0:
                kn = _norm_rope(kc, kg, None, None)
            else:
                kn = _norm_rope(kc, kg, cos_ref[(c - 1) * tq:c * tq, :], sin_ref[(c - 1) * tq:c * tq, :])
            kt_s[:, c * tq:(c + 1) * tq] = kn.T.astype(BF16)
        lane = lax.broadcasted_iota(jnp.int32, (rows, ATT_HD), 1)
        vx_s[:, :ATT_HD] = v_ref[...]
        vx_s[:, ATT_HD:] = jnp.where(lane == 0, 1.0, 0.0).astype(BF16)

    def attend(cs, sn, n_keys):
        qg = qg_ref[...]
        scores = []
        for g in range(ATT_GROUP):
            q = _norm_rope(q_ref[:, g * ATT_HD:(g + 1) * ATT_HD].astype(F32), qg, cs, sn) * scale
            scores.append(jnp.dot(q.astype(BF16), kt_s[:, :n_keys], preferred_element_type=F32))
        for g in range(ATT_GROUP):
            s = scores[g]
            p = jnp.exp2(s - jnp.max(s, axis=-1, keepdims=True))
            ox = jnp.dot(p.astype(BF16), vx_s[:n_keys, :], preferred_element_type=F32)
            o = ox[:, :ATT_HD] / ox[:, ATT_HD:ATT_HD + 1]
            o_ref[:, g * ATT_HD:(g + 1) * ATT_HD] = o.astype(o_ref.dtype)

    @pl.when(qi == 0)
    def _context_queries():
        attend(None, None, n_ctx)

    @pl.when(qi > 0)
    def _latent_queries():
        off = pl.multiple_of((qi - 1) * tq, tq)
        attend(cos_ref[pl.ds(off, tq), :], sin_ref[pl.ds(off, tq), :], rows)


def _attn_call(p, q_g, k_g, cos_t, sin_t, layer, n_ctx):
    b, r, _ = p.shape
    tq = n_ctx
    t = r - n_ctx
    qw = ATT_GROUP * ATT_HD
    return pl.pallas_call(
        functools.partial(_attn_kernel, n_ctx=n_ctx, tq=tq),
        grid=(b, ATT_KV_HEADS, r // tq),
        in_specs=[
            pl.BlockSpec((None, tq, qw), lambda bi, kv, qi: (bi, qi, COL_AQ // qw + kv)),
            pl.BlockSpec((None, r, ATT_HD), lambda bi, kv, qi: (bi, 0, COL_AK // ATT_HD + kv)),
            pl.BlockSpec((None, r, ATT_HD), lambda bi, kv, qi: (bi, 0, COL_AV // ATT_HD + kv)),
            pl.BlockSpec((None, 1, ATT_HD), lambda bi, kv, qi: (layer, 0, 0)),
            pl.BlockSpec((None, 1, ATT_HD), lambda bi, kv, qi: (layer, 0, 0)),
            pl.BlockSpec((t, ATT_HD), lambda bi, kv, qi: (0, 0)),
            pl.BlockSpec((t, ATT_HD), lambda bi, kv, qi: (0, 0)),
        ],
        out_specs=pl.BlockSpec((None, tq, qw), lambda bi, kv, qi: (bi, qi, kv)),
        out_shape=jax.ShapeDtypeStruct((b, r, ATT_Q), BF16),
        scratch_shapes=[pltpu.VMEM((ATT_HD, r), BF16), pltpu.VMEM((r, 2 * ATT_HD), BF16)],
        compiler_params=_cparams(3),
        name="gqa",
    )(p, p, p, q_g, k_g, cos_t, sin_t)


def _gla_direction(q, k, v, dec, wd, bd, vis, st_s, reverse, n_chunks):
    c_len = GLA_CHUNK

    tri = vis > 0.5
    tri_b = vis.astype(BF16)
    nt = (((1,), (1,)), ((), ()))
    tn = (((0,), (0,)), ((), ()))

    x = jnp.dot(dec, wd, preferred_element_type=F32) + bd
    g = (jnp.minimum(x, 0.0) - jnp.log(1.0 + jnp.exp(-jnp.abs(x)))) * (1.0 / GLA_GATE_NORM)
    g_hi = g.astype(BF16)
    g_lo = (g - g_hi.astype(F32)).astype(BF16)
    bcum = jnp.dot(tri_b, g_hi, preferred_element_type=F32) + jnp.dot(tri_b, g_lo, preferred_element_type=F32)
    q = q.astype(F32) * (GLA_HDK ** -0.5)
    k = k.astype(F32)
    q_dec = (q * jnp.exp(bcum)).astype(BF16)
    k_inv = (k * jnp.exp(-bcum)).astype(BF16)
    att = lax.dot_general(q_dec, k_inv, nt, preferred_element_type=F32)
    att = jnp.where(tri, att, 0.0).astype(BF16)
    o_intra = jnp.dot(att, v, preferred_element_type=F32)
    b_last, k_dec = [], []
    for c in range(n_chunks):
        sl = slice(c * c_len, (c + 1) * c_len)
        end = c * c_len if reverse else (c + 1) * c_len - 1
        b_last.append(bcum[end:end + 1, :])
        k_dec.append((k[sl, :] * jnp.exp(b_last[c] - bcum[sl, :])).astype(BF16))

    outs = [None] * n_chunks
    chunk_order = range(n_chunks - 1, -1, -1) if reverse else range(n_chunks)
    st = st_s[...]
    for c in chunk_order:
        sl = slice(c * c_len, (c + 1) * c_len)
        outs[c] = o_intra[sl, :] + lax.dot_general(q_dec[sl, :], st.astype(BF16), nt, preferred_element_type=F32)
        st = st * jnp.exp(b_last[c]) + lax.dot_general(v[sl, :], k_dec[c], tn, preferred_element_type=F32)
    st_s[...] = st
    return jnp.concatenate(outs, axis=0)


def _gla_kernel(qf_ref, kf_ref, vf_ref, decf_ref, qb_ref, kb_ref, vb_ref, decb_ref,
                wdf_ref, bdf_ref, wdb_ref, bdb_ref, visf_ref, visb_ref, r_ref, ng_ref, o_ref,
                stf_s, stb_s, of_s, ob_s, *, n_chunks, nblk):
    t = pl.program_id(2)
    tb = n_chunks * GLA_CHUNK

    @pl.when(t == 0)
    def _zero_states():
        stf_s[...] = jnp.zeros_like(stf_s)
        stb_s[...] = jnp.zeros_like(stb_s)

    blk_b = jnp.where(t == 0, 0, nblk - t)
    rows_f = pl.ds(pl.multiple_of(t * tb, tb), tb)
    rows_b = pl.ds(pl.multiple_of(blk_b * tb, tb), tb)
    decf = decf_ref[...].astype(BF16)
    decb = decb_ref[...].astype(BF16)
    visf = visf_ref[...]
    visb = visb_ref[...]
    for hh in range(GLA_HEADS_PER_STEP):
        kcol = slice(hh * GLA_HDK, (hh + 1) * GLA_HDK)
        vcol = slice(hh * GLA_HDV, (hh + 1) * GLA_HDV)
        of_s[rows_f, vcol] = _gla_direction(
            qf_ref[:, kcol], kf_ref[:, kcol], vf_ref[:, vcol], decf, wdf_ref[:, kcol].astype(BF16),
            bdf_ref[:, kcol], visf, stf_s.at[hh], False, n_chunks)
        ob_s[rows_b, vcol] = _gla_direction(
            qb_ref[:, kcol], kb_ref[:, kcol], vb_ref[:, vcol], decb, wdb_ref[:, kcol].astype(BF16),
            bdb_ref[:, kcol], visb, stb_s.at[hh], True, n_chunks)

    @pl.when(t == nblk - 1)
    def _combine_norm_gate():
        for c in range(nblk):
            sl = slice(c * tb, (c + 1) * tb)
            for hh in range(GLA_HEADS_PER_STEP):
                vcol = slice(hh * GLA_HDV, (hh + 1) * GLA_HDV)
                o = of_s[sl, vcol] + ob_s[sl, vcol]
                var = jnp.mean(o * o, axis=-1, keepdims=True)
                y = o * lax.rsqrt(var + NORM_EPS) * ng_ref[...]
                r = r_ref[sl, vcol].astype(F32)
                o_ref[sl, vcol] = (y * _silu(r)).astype(o_ref.dtype)


def _gla_call(p, dec, wd_pad, bd, norm_g, layer, n_ctx):
    b, r, _ = p.shape
    tb = n_ctx
    nblk = r // tb

    def blk(t, direction):
        if direction == 0:
            return t
        return jnp.where(t == 0, 0, nblk - t)

    kw = GLA_HEADS_PER_STEP * GLA_HDK
    vw = GLA_HEADS_PER_STEP * GLA_HDV

    def stream_specs(direction):
        return [
            pl.BlockSpec((None, tb, kw), lambda bi, h, t: (bi, blk(t, direction), COL_GQ // kw + h)),
            pl.BlockSpec((None, tb, kw), lambda bi, h, t: (bi, blk(t, direction), COL_GK // kw + h)),
            pl.BlockSpec((None, tb, vw), lambda bi, h, t: (bi, blk(t, direction), COL_GV // vw + h)),
            pl.BlockSpec((None, tb, DEC_PAD), lambda bi, h, t: (bi, blk(t, direction), 0)),
        ]

    def decay_specs(direction):
        return [
            pl.BlockSpec((None, None, DEC_PAD, kw), lambda bi, h, t: (layer, direction, 0, h)),
            pl.BlockSpec((None, None, 1, kw), lambda bi, h, t: (layer, direction, 0, h)),
        ]

    idx = np.arange(tb)
    same_chunk = (idx[:, None] // GLA_CHUNK) == (idx[None, :] // GLA_CHUNK)
    vis = jnp.asarray(np.stack([same_chunk & (idx[:, None] >= idx[None, :]),
                                same_chunk & (idx[None, :] >= idx[:, None])]).astype(np.float32))

    return pl.pallas_call(
        functools.partial(_gla_kernel, n_chunks=tb // GLA_CHUNK, nblk=nblk),
        grid=(b, GLA_HEADS // GLA_HEADS_PER_STEP, nblk),
        in_specs=stream_specs(0) + stream_specs(1) + decay_specs(0) + decay_specs(1) + [
            pl.BlockSpec((None, tb, tb), lambda bi, h, t: (0, 0, 0)),
            pl.BlockSpec((None, tb, tb), lambda bi, h, t: (1, 0, 0)),
            pl.BlockSpec((None, r, vw), lambda bi, h, t: (bi, 0, COL_GR // vw + h)),
            pl.BlockSpec((None, 1, GLA_HDV), lambda bi, h, t: (layer, 0, 0)),
        ],
        out_specs=pl.BlockSpec((None, r, vw), lambda bi, h, t: (bi, 0, h)),
        out_shape=jax.ShapeDtypeStruct((b, r, GLA_DV), BF16),
        scratch_shapes=([pltpu.VMEM((GLA_HEADS_PER_STEP, GLA_HDV, GLA_HDK), F32)] * 2
                        + [pltpu.VMEM((r, vw), F32)] * 2),
        compiler_params=_cparams(3),
        name="gla",
    )(p, p, p, dec, p, p, p, dec, wd_pad, bd, wd_pad, bd, vis, vis, p, norm_g)


LRU_PAD = 8


def _lru_kernel(x_ref, y_ref, cw_ref, cb_ref, wa_ref, ba_ref, wi_ref, bi_ref, lam_ref, o_ref,
                xp_s, af_s, bf_s, ab_s, bb_s, *, n_ctx, tc, tr):
    rows = x_ref.shape[0]
    n_lat = rows - n_ctx
    sub = 8
    ctx0 = LRU_PAD
    lat0 = 2 * LRU_PAD + n_ctx
    zpad = jnp.zeros((LRU_PAD, tc), F32)
    xp_s[0:LRU_PAD, :] = zpad
    xp_s[ctx0 + n_ctx:lat0, :] = zpad
    xp_s[lat0 + n_lat:lat0 + n_lat + LRU_PAD, :] = zpad
    xp_s[ctx0:ctx0 + n_ctx, :] = x_ref[0:n_ctx, :].astype(F32)
    xp_s[lat0:lat0 + n_lat, :] = x_ref[n_ctx:rows, :].astype(F32)

    cw = cw_ref[...]
    cb = cb_ref[...]
    lam = lam_ref[...]
    neg_sp = -(jnp.maximum(-lam, 0.0) + jnp.log1p(jnp.exp(-jnp.abs(lam))))
    log2_a_half = (0.5 * LRU_C * float(np.log2(np.e))) * neg_sp
    row_id = lax.broadcasted_iota(jnp.int32, (tr, 1), 0)
    n_kb = tc // LRU_BS
    wa_h = [[(0.5 * wa_ref[d, kb]).astype(BF16) for kb in range(n_kb)] for d in range(2)]
    wi_h = [[(0.5 * wi_ref[d, kb]).astype(BF16) for kb in range(n_kb)] for d in range(2)]
    ba_h = 0.5 * ba_ref[...]
    bi_h = 0.5 * bi_ref[...]

    for c in range(rows // tr):
        r0 = c * tr
        p0 = (ctx0 if r0 < n_ctx else lat0 - n_ctx) + r0
        n_win = tr + 2 * sub
        win = xp_s[p0 - sub:p0 + tr + sub, :]
        xc = cb + win[sub:sub + tr, :] * cw[CONV_LEFT:CONV_LEFT + 1, :]
        for j in range(CONV_W):
            off = j - CONV_LEFT
            if off != 0:
                xc = xc + pltpu.roll(win, (-off) % n_win, 0)[sub:sub + tr, :] * cw[j:j + 1, :]
        xh = 0.5 * xc
        xb = [xc[:, kb * LRU_BS:(kb + 1) * LRU_BS].astype(BF16) for kb in range(n_kb)]
        for d, (a_s, b_s) in enumerate(((af_s, bf_s), (ab_s, bb_s))):
            za = jnp.concatenate([jnp.dot(xb[kb], wa_h[d][kb], preferred_element_type=F32)
                                  for kb in range(n_kb)], axis=1)
            zi = jnp.concatenate([jnp.dot(xb[kb], wi_h[d][kb], preferred_element_type=F32)
                                  for kb in range(n_kb)], axis=1)
            ta = jnp.tanh(za + ba_h[d:d + 1, :])
            ti = jnp.tanh(zi + bi_h[d:d + 1, :])
            la_h = log2_a_half[d:d + 1, :]
            a = jnp.exp2(la_h + la_h * ta)
            q = 1.0 - a * a
            mult = jnp.where(q > 0.0, q * lax.rsqrt(q), 0.0)
            first = 0 if d == 0 else n_ctx - 1
            if r0 <= first < r0 + tr:
                mult = jnp.where(row_id == first - r0, 1.0, mult)
            a_s[r0:r0 + tr, :] = a
            b_s[r0:r0 + tr, :] = (mult * xh) * (1.0 + ti)

    row8 = lax.broadcasted_iota(jnp.int32, (sub, tc), 0)

    def tile_maps(a, b, reverse):
        for dlt in (1, 2, 4):
            shift = sub - dlt if reverse else dlt
            valid = (row8 < sub - dlt) if reverse else (row8 >= dlt)
            a_sh = jnp.where(valid, pltpu.roll(a, shift, 0), 1.0)
            b_sh = jnp.where(valid, pltpu.roll(b, shift, 0), 0.0)
            b = b + a * b_sh
            a = a * a_sh
        return a, b

    def scan_tile(tile_f, tile_b, carry):
        hf, hb = carry
        rf = pl.ds(pl.multiple_of(tile_f * sub, sub), sub)
        rb = pl.ds(pl.multiple_of(tile_b * sub, sub), sub)
        a_f, b_f = tile_maps(af_s[rf, :], bf_s[rf, :], False)
        a_b, b_b = tile_maps(ab_s[rb, :], bb_s[rb, :], True)
        h_f = b_f + a_f * hf
        h_b = b_b + a_b * hb
        bf_s[rf, :] = h_f
        bb_s[rb, :] = h_b
        return h_f[sub - 1:sub, :], h_b[0:1, :]

    zero = jnp.zeros((1, tc), F32)
    ctx_tiles = n_ctx // sub
    all_tiles = rows // sub
    carry = lax.fori_loop(0, ctx_tiles, lambda i, cr: scan_tile(i, ctx_tiles - 1 - i, cr), (zero, zero),
                          unroll=2)
    lax.fori_loop(ctx_tiles, all_tiles, lambda i, cr: scan_tile(i, all_tiles - 1 + ctx_tiles - i, cr), carry,
                  unroll=2)

    k0 = 0.7978845608028654
    for c in range(rows // tr):
        sl = slice(c * tr, (c + 1) * tr)
        y = y_ref[sl, :].astype(F32)
        gelu = 0.5 * y * (1.0 + jnp.tanh(k0 * (y + 0.044715 * (y * y * y))))
        o_ref[sl, :] = ((bf_s[sl, :] + bb_s[sl, :]) * gelu).astype(o_ref.dtype)


def _lru_call(p, conv_w, conv_b, w_a, b_a, w_i, b_i, lam, layer, n_ctx):
    b, r, _ = p.shape
    tc = 512
    nb = tc // LRU_BS
    chan = lambda bi, j: (layer, 0, j)
    return pl.pallas_call(
        functools.partial(_lru_kernel, n_ctx=n_ctx, tc=tc, tr=n_ctx),
        grid=(b, LRU_W // tc),
        in_specs=[
            pl.BlockSpec((None, r, tc), lambda bi, j: (bi, 0, COL_LX // tc + j)),
            pl.BlockSpec((None, r, tc), lambda bi, j: (bi, 0, COL_LY // tc + j)),
            pl.BlockSpec((None, CONV_W, tc), chan),
            pl.BlockSpec((None, 1, tc), chan),
            pl.BlockSpec((None, 2, nb, LRU_BS, LRU_BS), lambda bi, j: (layer, 0, j, 0, 0)),
            pl.BlockSpec((None, 2, tc), chan),
            pl.BlockSpec((None, 2, nb, LRU_BS, LRU_BS), lambda bi, j: (layer, 0, j, 0, 0)),
            pl.BlockSpec((None, 2, tc), chan),
            pl.BlockSpec((None, 2, tc), chan),
        ],
        out_specs=pl.BlockSpec((None, r, tc), lambda bi, j: (bi, 0, j)),
        out_shape=jax.ShapeDtypeStruct((b, r, LRU_W), BF16),
        scratch_shapes=[pltpu.VMEM((r + 3 * LRU_PAD, tc), F32)] + [pltpu.VMEM((r, tc), F32)] * 4,
        compiler_params=_cparams(2),
        name="rglru",
    )(p, p, conv_w, conv_b, w_a, b_a, w_i, b_i, lam)


def _merge_kernel(xa_ref, xb_ref, xc_ref, wa_ref, wb_ref, wc_ref, ga_ref, gb_ref, gc_ref, bm_ref, o_ref):
    acc = None
    for n, (x_ref, w_ref, g_ref) in enumerate(((xa_ref, wa_ref, ga_ref), (xb_ref, wb_ref, gb_ref),
                                               (xc_ref, wc_ref, gc_ref))):
        proj = jnp.dot(x_ref[...], w_ref[...], preferred_element_type=F32)
        term = _sigmoid(g_ref[...].astype(F32) + bm_ref[n:n + 1, :]) * proj
        acc = term if acc is None else acc + term
    o_ref[...] = acc.astype(o_ref.dtype)


def _merge_call(o_gla, o_att, o_lru, wbr, p, b_merge, layer, tm, tn):
    b, r, d = o_gla.shape
    ncol = d // tn
    x_spec = pl.BlockSpec((None, tm, d), lambda j, bi, i: (bi, i, 0))

    def w_spec(n):
        return pl.BlockSpec((None, None, d, tn), lambda j, bi, i: (layer, n, 0, j))

    def g_spec(n):
        return pl.BlockSpec((None, tm, tn), lambda j, bi, i: (bi, i, COL_GT // tn + n * ncol + j))

    return pl.pallas_call(
        _merge_kernel,
        grid=(ncol, b, r // tm),
        in_specs=[x_spec, x_spec, x_spec, w_spec(0), w_spec(1), w_spec(2), g_spec(0), g_spec(1), g_spec(2),
                  pl.BlockSpec((None, N_BRANCH, tn), lambda j, bi, i: (layer, 0, j))],
        out_specs=pl.BlockSpec((None, tm, tn), lambda j, bi, i: (bi, i, j)),
        out_shape=jax.ShapeDtypeStruct((b, r, d), BF16),
        compiler_params=_cparams(3),
        name="merge",
    )(o_gla, o_att, o_lru, wbr, wbr, wbr, p, p, p, b_merge)


def _final_norm_kernel(h_ref, g_ref, o_ref):
    x = h_ref[...]
    var = jnp.mean(x * x, axis=-1, keepdims=True)
    o_ref[...] = x * lax.rsqrt(var + NORM_EPS) * g_ref[...]


def _final_norm_call(h, g, n_ctx):
    b, r, d = h.shape
    tr = n_ctx
    return pl.pallas_call(
        _final_norm_kernel,
        grid=(b, (r - n_ctx) // tr),
        in_specs=[
            pl.BlockSpec((None, tr, d), lambda bi, i: (bi, i + 1, 0)),
            pl.BlockSpec((1, d), lambda bi, i: (0, 0)),
        ],
        out_specs=pl.BlockSpec((None, tr, d), lambda bi, i: (bi, i, 0)),
        out_shape=jax.ShapeDtypeStruct((b, r - n_ctx, d), F32),
        compiler_params=_cparams(2),
        name="final_norm",
    )(h, g)


def _rope_tables(n_tokens):
    rows = n_tokens // GRID_W
    row = jnp.repeat(jnp.arange(rows, dtype=F32), GRID_W)
    col = jnp.tile(jnp.arange(GRID_W, dtype=F32), rows)
    inv = ROPE_THETA ** (-jnp.arange(ROPE_AXIS_PAIRS, dtype=F32) / ROPE_AXIS_PAIRS)
    ang = jnp.concatenate([row[:, None] * inv, col[:, None] * inv], axis=-1)
    cos, sin = jnp.cos(ang), jnp.sin(ang)
    cos_t = jnp.repeat(cos, 2, axis=-1)
    sin_t = jnp.stack([-sin, sin], axis=-1).reshape(n_tokens, ATT_HD)
    return cos_t, sin_t


def kernel(x, c, ctx, c_ctx, w_mod, b_mod, norm_mix_g, norm_ffn_g, w_in, gla_w_decay, gla_b_decay, gla_norm_g, q_norm_g, k_norm_g, conv_w, conv_b, lru_w_a, lru_b_a, lru_w_i, lru_b_i, lru_lambda, b_merge, w_branch, w_out, w_ffn_in, w_ffn_out, final_norm_g):
    depth = w_in.shape[0]
    batch, t_lat, d = x.shape
    n_ctx = ctx.shape[1]
    rows = n_ctx + t_lat
    ctx_row = batch
    assert batch < MOD_ROWS and rows % n_ctx == 0 and n_ctx % GLA_CHUNK == 0
    tm = 768 if rows % 768 == 0 else n_ctx
    tm_big = 1152 if rows % 1152 == 0 else tm
    tm_small = 384 if rows % 384 == 0 else n_ctx
    tn_in, tn_merge, tn_out, tn_ffn_in, tn_ffn_out = 1024, 512, d, 512, 1024

    w_in_t = jnp.swapaxes(w_in, 1, 2)
    w_branch_b = w_branch.astype(BF16)
    w_out_b = w_out.astype(BF16)
    w_ffn_out_b = w_ffn_out.astype(BF16)
    wd_pad = jnp.zeros((depth, 2, DEC_PAD, GLA_DK), F32)
    wd_pad = wd_pad.at[:, 0, :GLA_RANK].set(gla_w_decay[:, 0])
    wd_pad = wd_pad.at[:, 1, GLA_RANK:2 * GLA_RANK].set(gla_w_decay[:, 1])
    bd = gla_b_decay.reshape(depth, 2, 1, GLA_DK)
    cos_t, sin_t = _rope_tables(t_lat)

    c16 = jnp.zeros((MOD_ROWS, d), F32).at[:batch].set(c).at[ctx_row].set(c_ctx)
    mods4 = _mods_call(c16, w_mod, b_mod).reshape(depth, MOD_ROWS, 1, N_MOD * d)

    g_mix = norm_mix_g.reshape(depth, 1, d)
    g_ffn = norm_ffn_g.reshape(depth, 1, d)
    g_gla = gla_norm_g.reshape(depth, 1, GLA_HDV)
    g_q = q_norm_g.reshape(depth, 1, ATT_HD)
    g_k = k_norm_g.reshape(depth, 1, ATT_HD)
    cb3 = conv_b.reshape(depth, 1, LRU_W)

    h = jnp.concatenate([ctx, x], axis=1)
    for l in range(depth):
        u, dec = _normmod_call(h, g_mix, mods4, w_in_t, l, 0, 1, n_ctx, ctx_row, tm)
        p = _in_proj_call(u, w_in_t, l, rows, tn_in)
        o_att = _attn_call(p, g_q, g_k, cos_t, sin_t, l, n_ctx)
        o_gla = _gla_call(p, dec, wd_pad, bd, g_gla, l, n_ctx)
        o_lru = _lru_call(p, conv_w, cb3, lru_w_a, lru_b_a, lru_w_i, lru_b_i, lru_lambda, l, n_ctx)
        merged = _merge_call(o_gla, o_att, o_lru, w_branch_b, p, b_merge, l, tm, tn_merge)
        h, u2 = _linear_res_call(merged, w_out_b, l, h, mods4, 2, n_ctx, ctx_row, tm, tn_out, "out_proj",
                                 norm=(g_ffn, l, 3, 4))
        a = _swiglu_call(u2, w_ffn_in, l, tm_big, tn_ffn_in)
        h = _linear_res_call(a, w_ffn_out_b, l, h, mods4, 5, n_ctx, ctx_row, tm_small, tn_ffn_out, "ffn_out")
    return _final_norm_call(h, final_norm_g.reshape(1, d), n_ctx)
```

```python
import functools

import jax
import jax.numpy as jnp
import numpy as np
from jax import lax
from jax.experimental import pallas as pl
from jax.experimental.pallas import tpu as pltpu

F32 = jnp.float32
BF16 = jnp.bfloat16

D_MODEL = 2048
GRID_W = 64
NORM_EPS = 1e-6
N_MOD = 6

GLA_HEADS = 4
GLA_DK = D_MODEL // 2
GLA_DV = D_MODEL
GLA_HDK = GLA_DK // GLA_HEADS
GLA_HDV = GLA_DV // GLA_HEADS
GLA_RANK = 16
GLA_GATE_NORM = 16.0
GLA_CHUNK = 64
GLA_HEADS_PER_STEP = 2

ATT_HEADS = 16
ATT_KV_HEADS = 4
ATT_HD = 128
ATT_GROUP = ATT_HEADS // ATT_KV_HEADS
ATT_Q = ATT_HEADS * ATT_HD
ATT_KV = ATT_KV_HEADS * ATT_HD
ROPE_THETA = 10000.0
ROPE_AXIS_PAIRS = ATT_HD // 4

LRU_W = D_MODEL
LRU_BLOCKS = 16
LRU_BS = LRU_W // LRU_BLOCKS
LRU_C = 8.0
CONV_W = 4
CONV_LEFT = 2

N_BRANCH = 3
FFN_HIDDEN = -(-8 * D_MODEL // (3 * 256)) * 256

COL_GQ = 0
COL_GK = COL_GQ + GLA_DK
COL_GV = COL_GK + GLA_DK
COL_GR = COL_GV + GLA_DV
COL_AQ = COL_GR + GLA_DV
COL_AK = COL_AQ + ATT_Q
COL_AV = COL_AK + ATT_KV
COL_LX = COL_AV + ATT_KV
COL_LY = COL_LX + LRU_W
COL_GT = COL_LY + LRU_W
N_MAIN = COL_GT + N_BRANCH * D_MODEL
DEC_ORIG = 2 * GLA_DK + 2 * GLA_DV
DEC_PAD = 128

V7X_VMEM_LIMIT_BYTES = 56 * 1024 * 1024
MOD_ROWS = 16


def _cparams(n_axes, vmem=V7X_VMEM_LIMIT_BYTES):
    return pltpu.CompilerParams(dimension_semantics=("arbitrary",) * n_axes, vmem_limit_bytes=vmem)


def _sigmoid(x):
    return 0.5 * jnp.tanh(0.5 * x) + 0.5


def _silu(x):
    h = 0.5 * x
    return h + h * jnp.tanh(h)


def _mods_kernel(c_ref, w_ref, b_ref, o_ref):
    c = c_ref[...]
    sc = _silu(c).astype(BF16)
    o_ref[...] = jnp.dot(sc, w_ref[...].astype(BF16), preferred_element_type=F32) + b_ref[...]


def _mods_call(c16, w_mod, b_mod):
    depth, d, n = w_mod.shape
    tn = 1024
    return pl.pallas_call(
        _mods_kernel,
        grid=(depth, n // tn),
        in_specs=[
            pl.BlockSpec((MOD_ROWS, d), lambda l, j: (0, 0)),
            pl.BlockSpec((None, d, tn), lambda l, j: (l, 0, j)),
            pl.BlockSpec((None, 1, tn), lambda l, j: (l, 0, j)),
        ],
        out_specs=pl.BlockSpec((None, MOD_ROWS, tn), lambda l, j: (l, 0, j)),
        out_shape=jax.ShapeDtypeStruct((depth, MOD_ROWS, n), F32),
        compiler_params=_cparams(2),
        name="mods",
    )(c16, w_mod, b_mod.reshape(depth, 1, n))


def _normmod_kernel(h_ref, g_ref, shc_ref, shl_ref, scc_ref, scl_ref, wt_ref, o_ref, dec_ref, *, tr, n_ctx):
    x = h_ref[...]
    var = jnp.mean(x * x, axis=-1, keepdims=True)
    y = x * lax.rsqrt(var + NORM_EPS) * g_ref[...]
    is_ctx = (pl.program_id(1) * tr + lax.broadcasted_iota(jnp.int32, (tr, 1), 0)) < n_ctx
    shift = jnp.where(is_ctx, shc_ref[...], shl_ref[...])
    scale = jnp.where(is_ctx, scc_ref[...], scl_ref[...])
    u = (y * (1.0 + scale) + shift).astype(o_ref.dtype)
    o_ref[...] = u
    dec_ref[...] = lax.dot_general(u, wt_ref[...].astype(BF16), (((1,), (1,)), ((), ())),
                                   preferred_element_type=F32)


def _normmod_call(h, g, mods4, w_in_t, layer, k_shift, k_scale, n_ctx, ctx_row, tr):
    b, r, d = h.shape

    def mod_spec(k, is_ctx):
        return pl.BlockSpec((None, None, 1, d), lambda bi, i: (layer, ctx_row if is_ctx else bi, 0, k))

    return pl.pallas_call(
        functools.partial(_normmod_kernel, tr=tr, n_ctx=n_ctx),
        grid=(b, r // tr),
        in_specs=[
            pl.BlockSpec((None, tr, d), lambda bi, i: (bi, i, 0)),
            pl.BlockSpec((None, 1, d), lambda bi, i: (layer, 0, 0)),
            mod_spec(k_shift, True), mod_spec(k_shift, False),
            mod_spec(k_scale, True), mod_spec(k_scale, False),
            pl.BlockSpec((None, DEC_PAD, d), lambda bi, i: (layer, DEC_ORIG // DEC_PAD, 0)),
        ],
        out_specs=[pl.BlockSpec((None, tr, d), lambda bi, i: (bi, i, 0)),
                   pl.BlockSpec((None, tr, DEC_PAD), lambda bi, i: (bi, i, 0))],
        out_shape=[jax.ShapeDtypeStruct((b, r, d), BF16), jax.ShapeDtypeStruct((b, r, DEC_PAD), F32)],
        compiler_params=_cparams(2),
        name="normmod",
    )(h, g, mods4, mods4, mods4, mods4, w_in_t)


def _in_proj_kernel(x_ref, wa_ref, wb_ref, o_ref, *, n_plain, shift):
    j = pl.program_id(0)
    nt = (((1,), (1,)), ((), ()))

    @pl.when(j < n_plain)
    def _before_decay_columns():
        w = wa_ref[...].astype(BF16)
        o_ref[...] = lax.dot_general(x_ref[...], w, nt, preferred_element_type=F32).astype(o_ref.dtype)

    @pl.when(j >= n_plain)
    def _after_decay_columns():
        w = jnp.concatenate([wa_ref[shift:, :], wb_ref[...]], axis=0).astype(BF16)
        o_ref[...] = lax.dot_general(x_ref[...], w, nt, preferred_element_type=F32).astype(o_ref.dtype)


def _in_proj_call(x, w_in_t, layer, tm, tn):
    b, r, k = x.shape
    shift = 2 * GLA_RANK
    assert DEC_ORIG % tn == 0 and N_MAIN % tn == 0 and tn % shift == 0
    return pl.pallas_call(
        functools.partial(_in_proj_kernel, n_plain=DEC_ORIG // tn, shift=shift),
        grid=(N_MAIN // tn, b, r // tm),
        in_specs=[
            pl.BlockSpec((None, tm, k), lambda j, bi, i: (bi, i, 0)),
            pl.BlockSpec((None, tn, k), lambda j, bi, i: (layer, j, 0)),
            pl.BlockSpec((None, shift, k), lambda j, bi, i: (layer, (j + 1) * (tn // shift), 0)),
        ],
        out_specs=pl.BlockSpec((None, tm, tn), lambda j, bi, i: (bi, i, j)),
        out_shape=jax.ShapeDtypeStruct((b, r, N_MAIN), BF16),
        compiler_params=_cparams(3),
        name="in_proj",
    )(x, w_in_t, w_in_t)


def _linear_res_kernel(*refs, tm, n_ctx, with_norm):
    x_ref, w_ref, h_ref, gc_ref, gl_ref = refs[:5]
    if with_norm:
        ng_ref, shc_ref, shl_ref, scc_ref, scl_ref, o_ref, u_ref = refs[5:]
    else:
        (o_ref,) = refs[5:]
    i = pl.program_id(2)
    y = jnp.dot(x_ref[...], w_ref[...], preferred_element_type=F32)
    is_ctx = (i * tm + lax.broadcasted_iota(jnp.int32, (tm, 1), 0)) < n_ctx
    h_new = h_ref[...] + jnp.where(is_ctx, gc_ref[...], gl_ref[...]) * y
    o_ref[...] = h_new
    if with_norm:
        var = jnp.mean(h_new * h_new, axis=-1, keepdims=True)
        yn = h_new * lax.rsqrt(var + NORM_EPS) * ng_ref[...]
        shift = jnp.where(is_ctx, shc_ref[...], shl_ref[...])
        scale = jnp.where(is_ctx, scc_ref[...], scl_ref[...])
        u_ref[...] = (yn * (1.0 + scale) + shift).astype(u_ref.dtype)


def _linear_res_call(x, w3, layer, h, mods4, k_gate, n_ctx, ctx_row, tm, tn, name, norm=None):
    b, r, k = x.shape
    n = w3.shape[-1]
    ngate = n // tn

    def mod_spec(lyr, row_of, kk):
        return pl.BlockSpec((None, None, 1, tn), lambda j, bi, i: (lyr, row_of(bi), 0, kk * ngate + j))

    ctx_of = lambda bi: ctx_row
    lat_of = lambda bi: bi
    w_mode = dict(pipeline_mode=pl.Buffered(1)) if tn == n else {}
    in_specs = [
        pl.BlockSpec((None, tm, k), lambda j, bi, i: (bi, i, 0)),
        pl.BlockSpec((None, k, tn), lambda j, bi, i: (layer, 0, j), **w_mode),
        pl.BlockSpec((None, tm, tn), lambda j, bi, i: (bi, i, j)),
        mod_spec(layer, ctx_of, k_gate),
        mod_spec(layer, lat_of, k_gate),
    ]
    args = [x, w3, h, mods4, mods4]
    out_specs = [pl.BlockSpec((None, tm, tn), lambda j, bi, i: (bi, i, j))]
    out_shape = [jax.ShapeDtypeStruct((b, r, n), F32)]
    if norm is not None:
        gains, n_layer, k_shift, k_scale = norm
        assert tn == n
        in_specs += [
            pl.BlockSpec((None, 1, n), lambda j, bi, i: (n_layer, 0, 0)),
            mod_spec(n_layer, ctx_of, k_shift), mod_spec(n_layer, lat_of, k_shift),
            mod_spec(n_layer, ctx_of, k_scale), mod_spec(n_layer, lat_of, k_scale),
        ]
        args += [gains, mods4, mods4, mods4, mods4]
        out_specs.append(pl.BlockSpec((None, tm, tn), lambda j, bi, i: (bi, i, j)))
        out_shape.append(jax.ShapeDtypeStruct((b, r, n), BF16))
    res = pl.pallas_call(
        functools.partial(_linear_res_kernel, tm=tm, n_ctx=n_ctx, with_norm=norm is not None),
        grid=(n // tn, b, r // tm),
        in_specs=in_specs,
        out_specs=out_specs,
        out_shape=out_shape,
        compiler_params=_cparams(3),
        name=name,
    )(*args)
    return res if norm is not None else res[0]


def _swiglu_kernel(x_ref, wg_ref, wu_ref, o_ref):
    x = x_ref[...]
    g = jnp.dot(x, wg_ref[...].astype(BF16), preferred_element_type=F32)
    up = jnp.dot(x, wu_ref[...].astype(BF16), preferred_element_type=F32)
    o_ref[...] = (g * _sigmoid(g) * up).astype(o_ref.dtype)


def _swiglu_call(x, w3, layer, tm, tn):
    b, r, k = x.shape
    hid = w3.shape[-1] // 2
    nb = hid // tn
    return pl.pallas_call(
        _swiglu_kernel,
        grid=(nb, b, r // tm),
        in_specs=[
            pl.BlockSpec((None, tm, k), lambda j, bi, i: (bi, i, 0)),
            pl.BlockSpec((None, k, tn), lambda j, bi, i: (layer, 0, j)),
            pl.BlockSpec((None, k, tn), lambda j, bi, i: (layer, 0, nb + j)),
        ],
        out_specs=pl.BlockSpec((None, tm, tn), lambda j, bi, i: (bi, i, j)),
        out_shape=jax.ShapeDtypeStruct((b, r, hid), BF16),
        compiler_params=_cparams(3),
        name="ffn_in",
    )(x, w3, w3)


def _norm_rope(x, gain, cs, sn):
    var = jnp.mean(x * x, axis=-1, keepdims=True)
    y = x * lax.rsqrt(var + NORM_EPS) * gain
    if cs is None:
        return y
    lane = lax.broadcasted_iota(jnp.int32, y.shape, 1)
    nxt = pltpu.roll(y, ATT_HD - 1, 1)
    prv = pltpu.roll(y, 1, 1)
    partner = jnp.where(lane % 2 == 0, nxt, prv)
    return y * cs + partner * sn


def _attn_kernel(q_ref, k_ref, v_ref, qg_ref, kg_ref, cos_ref, sin_ref, o_ref, kt_s, vx_s, *, n_ctx, tq):
    qi = pl.program_id(2)
    rows = k_ref.shape[0]
    scale = ATT_HD ** -0.5 * float(np.log2(np.e))

    @pl.when(qi == 0)
    def _prepare_kv():
        kg = kg_ref[...]
        for c in range(rows // tq):
            kc = k_ref[c * tq:(c + 1) * tq, :].astype(F32)
            if c == 0:
                kn = _norm_rope(kc, kg, None, None)
            else:
                kn = _norm_rope(kc, kg, cos_ref[(c - 1) * tq:c * tq, :], sin_ref[(c - 1) * tq:c * tq, :])
            kt_s[:, c * tq:(c + 1) * tq] = kn.T.astype(BF16)
        lane = lax.broadcasted_iota(jnp.int32, (rows, ATT_HD), 1)
        vx_s[:, :ATT_HD] = v_ref[...]
        vx_s[:, ATT_HD:] = jnp.where(lane == 0, 1.0, 0.0).astype(BF16)

    def attend(cs, sn, n_keys):
        qg = qg_ref[...]
        scores = []
        for g in range(ATT_GROUP):
            q = _norm_rope(q_ref[:, g * ATT_HD:(g + 1) * ATT_HD].astype(F32), qg, cs, sn) * scale
            scores.append(jnp.dot(q.astype(BF16), kt_s[:, :n_keys], preferred_element_type=F32))
        for g in range(ATT_GROUP):
            s = scores[g]
            p = jnp.exp2(s - jnp.max(s, axis=-1, keepdims=True))
            ox = jnp.dot(p.astype(BF16), vx_s[:n_keys, :], preferred_element_type=F32)
            o = ox[:, :ATT_HD] / ox[:, ATT_HD:ATT_HD + 1]
            o_ref[:, g * ATT_HD:(g + 1) * ATT_HD] = o.astype(o_ref.dtype)

    @pl.when(qi == 0)
    def _context_queries():
        attend(None, None, n_ctx)

    @pl.when(qi > 0)
    def _latent_queries():
        off = pl.multiple_of((qi - 1) * tq, tq)
        attend(cos_ref[pl.ds(off, tq), :], sin_ref[pl.ds(off, tq), :], rows)


def _attn_call(p, q_g, k_g, cos_t, sin_t, layer, n_ctx):
    b, r, _ = p.shape
    tq = n_ctx
    t = r - n_ctx
    qw = ATT_GROUP * ATT_HD
    return pl.pallas_call(
        functools.partial(_attn_kernel, n_ctx=n_ctx, tq=tq),
        grid=(b, ATT_KV_HEADS, r // tq),
        in_specs=[
            pl.BlockSpec((None, tq, qw), lambda bi, kv, qi: (bi, qi, COL_AQ // qw + kv)),
            pl.BlockSpec((None, r, ATT_HD), lambda bi, kv, qi: (bi, 0, COL_AK // ATT_HD + kv)),
            pl.BlockSpec((None, r, ATT_HD), lambda bi, kv, qi: (bi, 0, COL_AV // ATT_HD + kv)),
            pl.BlockSpec((None, 1, ATT_HD), lambda bi, kv, qi: (layer, 0, 0)),
            pl.BlockSpec((None, 1, ATT_HD), lambda bi, kv, qi: (layer, 0, 0)),
            pl.BlockSpec((t, ATT_HD), lambda bi, kv, qi: (0, 0)),
            pl.BlockSpec((t, ATT_HD), lambda bi, kv, qi: (0, 0)),
        ],
        out_specs=pl.BlockSpec((None, tq, qw), lambda bi, kv, qi: (bi, qi, kv)),
        out_shape=jax.ShapeDtypeStruct((b, r, ATT_Q), BF16),
        scratch_shapes=[pltpu.VMEM((ATT_HD, r), BF16), pltpu.VMEM((r, 2 * ATT_HD), BF16)],
        compiler_params=_cparams(3),
        name="gqa",
    )(p, p, p, q_g, k_g, cos_t, sin_t)


def _gla_direction(q, k, v, dec, wd, bd, vis, st_s, reverse, n_chunks):
    c_len = GLA_CHUNK

    tri = vis > 0.5
    tri_b = vis.astype(BF16)
    nt = (((1,), (1,)), ((), ()))
    tn = (((0,), (0,)), ((), ()))

    x = jnp.dot(dec, wd, preferred_element_type=F32) + bd
    g = (jnp.minimum(x, 0.0) - jnp.log(1.0 + jnp.exp(-jnp.abs(x)))) * (1.0 / GLA_GATE_NORM)
    g_hi = g.astype(BF16)
    g_lo = (g - g_hi.astype(F32)).astype(BF16)
    bcum = jnp.dot(tri_b, g_hi, preferred_element_type=F32) + jnp.dot(tri_b, g_lo, preferred_element_type=F32)
    q = q.astype(F32) * (GLA_HDK ** -0.5)
    k = k.astype(F32)
    q_dec = (q * jnp.exp(bcum)).astype(BF16)
    k_inv = (k * jnp.exp(-bcum)).astype(BF16)
    att = lax.dot_general(q_dec, k_inv, nt, preferred_element_type=F32)
    att = jnp.where(tri, att, 0.0).astype(BF16)
    o_intra = jnp.dot(att, v, preferred_element_type=F32)
    b_last, k_dec = [], []
    for c in range(n_chunks):
        sl = slice(c * c_len, (c + 1) * c_len)
        end = c * c_len if reverse else (c + 1) * c_len - 1
        b_last.append(bcum[end:end + 1, :])
        k_dec.append((k[sl, :] * jnp.exp(b_last[c] - bcum[sl, :])).astype(BF16))

    outs = [None] * n_chunks
    chunk_order = range(n_chunks - 1, -1, -1) if reverse else range(n_chunks)
    st = st_s[...]
    for c in chunk_order:
        sl = slice(c * c_len, (c + 1) * c_len)
        outs[c] = o_intra[sl, :] + lax.dot_general(q_dec[sl, :], st.astype(BF16), nt, preferred_element_type=F32)
        st = st * jnp.exp(b_last[c]) + lax.dot_general(v[sl, :], k_dec[c], tn, preferred_element_type=F32)
    st_s[...] = st
    return jnp.concatenate(outs, axis=0)


def _gla_kernel(qf_ref, kf_ref, vf_ref, decf_ref, qb_ref, kb_ref, vb_ref, decb_ref,
                wdf_ref, bdf_ref, wdb_ref, bdb_ref, visf_ref, visb_ref, r_ref, ng_ref, o_ref,
                stf_s, stb_s, of_s, ob_s, *, n_chunks, nblk):
    t = pl.program_id(2)
    tb = n_chunks * GLA_CHUNK

    @pl.when(t == 0)
    def _zero_states():
        stf_s[...] = jnp.zeros_like(stf_s)
        stb_s[...] = jnp.zeros_like(stb_s)

    blk_b = jnp.where(t == 0, 0, nblk - t)
    rows_f = pl.ds(pl.multiple_of(t * tb, tb), tb)
    rows_b = pl.ds(pl.multiple_of(blk_b * tb, tb), tb)
    decf = decf_ref[...].astype(BF16)
    decb = decb_ref[...].astype(BF16)
    visf = visf_ref[...]
    visb = visb_ref[...]
    for hh in range(GLA_HEADS_PER_STEP):
        kcol = slice(hh * GLA_HDK, (hh + 1) * GLA_HDK)
        vcol = slice(hh * GLA_HDV, (hh + 1) * GLA_HDV)
        of_s[rows_f, vcol] = _gla_direction(
            qf_ref[:, kcol], kf_ref[:, kcol], vf_ref[:, vcol], decf, wdf_ref[:, kcol].astype(BF16),
            bdf_ref[:, kcol], visf, stf_s.at[hh], False, n_chunks)
        ob_s[rows_b, vcol] = _gla_direction(
            qb_ref[:, kcol], kb_ref[:, kcol], vb_ref[:, vcol], decb, wdb_ref[:, kcol].astype(BF16),
            bdb_ref[:, kcol], visb, stb_s.at[hh], True, n_chunks)

    @pl.when(t == nblk - 1)
    def _combine_norm_gate():
        for c in range(nblk):
            sl = slice(c * tb, (c + 1) * tb)
            for hh in range(GLA_HEADS_PER_STEP):
                vcol = slice(hh * GLA_HDV, (hh + 1) * GLA_HDV)
                o = of_s[sl, vcol] + ob_s[sl, vcol]
                var = jnp.mean(o * o, axis=-1, keepdims=True)
                y = o * lax.rsqrt(var + NORM_EPS) * ng_ref[...]
                r = r_ref[sl, vcol].astype(F32)
                o_ref[sl, vcol] = (y * _silu(r)).astype(o_ref.dtype)


def _gla_call(p, dec, wd_pad, bd, norm_g, layer, n_ctx):
    b, r, _ = p.shape
    tb = n_ctx
    nblk = r // tb

    def blk(t, direction):
        if direction == 0:
            return t
        return jnp.where(t == 0, 0, nblk - t)

    kw = GLA_HEADS_PER_STEP * GLA_HDK
    vw = GLA_HEADS_PER_STEP * GLA_HDV

    def stream_specs(direction):
        return [
            pl.BlockSpec((None, tb, kw), lambda bi, h, t: (bi, blk(t, direction), COL_GQ // kw + h)),
            pl.BlockSpec((None, tb, kw), lambda bi, h, t: (bi, blk(t, direction), COL_GK // kw + h)),
            pl.BlockSpec((None, tb, vw), lambda bi, h, t: (bi, blk(t, direction), COL_GV // vw + h)),
            pl.BlockSpec((None, tb, DEC_PAD), lambda bi, h, t: (bi, blk(t, direction), 0)),
        ]

    def decay_specs(direction):
        return [
            pl.BlockSpec((None, None, DEC_PAD, kw), lambda bi, h, t: (layer, direction, 0, h)),
            pl.BlockSpec((None, None, 1, kw), lambda bi, h, t: (layer, direction, 0, h)),
        ]

    idx = np.arange(tb)
    same_chunk = (idx[:, None] // GLA_CHUNK) == (idx[None, :] // GLA_CHUNK)
    vis = jnp.asarray(np.stack([same_chunk & (idx[:, None] >= idx[None, :]),
                                same_chunk & (idx[None, :] >= idx[:, None])]).astype(np.float32))

    return pl.pallas_call(
        functools.partial(_gla_kernel, n_chunks=tb // GLA_CHUNK, nblk=nblk),
        grid=(b, GLA_HEADS // GLA_HEADS_PER_STEP, nblk),
        in_specs=stream_specs(0) + stream_specs(1) + decay_specs(0) + decay_specs(1) + [
            pl.BlockSpec((None, tb, tb), lambda bi, h, t: (0, 0, 0)),
            pl.BlockSpec((None, tb, tb), lambda bi, h, t: (1, 0, 0)),
            pl.BlockSpec((None, r, vw), lambda bi, h, t: (bi, 0, COL_GR // vw + h)),
            pl.BlockSpec((None, 1, GLA_HDV), lambda bi, h, t: (layer, 0, 0)),
        ],
        out_specs=pl.BlockSpec((None, r, vw), lambda bi, h, t: (bi, 0, h)),
        out_shape=jax.ShapeDtypeStruct((b, r, GLA_DV), BF16),
        scratch_shapes=([pltpu.VMEM((GLA_HEADS_PER_STEP, GLA_HDV, GLA_HDK), F32)] * 2
                        + [pltpu.VMEM((r, vw), F32)] * 2),
        compiler_params=_cparams(3),
        name="gla",
    )(p, p, p, dec, p, p, p, dec, wd_pad, bd, wd_pad, bd, vis, vis, p, norm_g)


LRU_PAD = 8


def _lru_kernel(x_ref, y_ref, cw_ref, cb_ref, wa_ref, ba_ref, wi_ref, bi_ref, lam_ref, o_ref,
                xp_s, af_s, bf_s, ab_s, bb_s, *, n_ctx, tc, tr):
    rows = x_ref.shape[0]
    n_lat = rows - n_ctx
    sub = 8
    ctx0 = LRU_PAD
    lat0 = 2 * LRU_PAD + n_ctx
    zpad = jnp.zeros((LRU_PAD, tc), F32)
    xp_s[0:LRU_PAD, :] = zpad
    xp_s[ctx0 + n_ctx:lat0, :] = zpad
    xp_s[lat0 + n_lat:lat0 + n_lat + LRU_PAD, :] = zpad
    xp_s[ctx0:ctx0 + n_ctx, :] = x_ref[0:n_ctx, :].astype(F32)
    xp_s[lat0:lat0 + n_lat, :] = x_ref[n_ctx:rows, :].astype(F32)

    cw = cw_ref[...]
    cb = cb_ref[...]
    lam = lam_ref[...]
    neg_sp = -(jnp.maximum(-lam, 0.0) + jnp.log1p(jnp.exp(-jnp.abs(lam))))
    log2_a_half = (0.5 * LRU_C * float(np.log2(np.e))) * neg_sp
    row_id = lax.broadcasted_iota(jnp.int32, (tr, 1), 0)
    n_kb = tc // LRU_BS
    wa_h = [[(0.5 * wa_ref[d, kb]).astype(BF16) for kb in range(n_kb)] for d in range(2)]
    wi_h = [[(0.5 * wi_ref[d, kb]).astype(BF16) for kb in range(n_kb)] for d in range(2)]
    ba_h = 0.5 * ba_ref[...]
    bi_h = 0.5 * bi_ref[...]

    for c in range(rows // tr):
        r0 = c * tr
        p0 = (ctx0 if r0 < n_ctx else lat0 - n_ctx) + r0
        n_win = tr + 2 * sub
        win = xp_s[p0 - sub:p0 + tr + sub, :]
        xc = cb + win[sub:sub + tr, :] * cw[CONV_LEFT:CONV_LEFT + 1, :]
        for j in range(CONV_W):
            off = j - CONV_LEFT
            if off != 0:
                xc = xc + pltpu.roll(win, (-off) % n_win, 0)[sub:sub + tr, :] * cw[j:j + 1, :]
        xh = 0.5 * xc
        xb = [xc[:, kb * LRU_BS:(kb + 1) * LRU_BS].astype(BF16) for kb in range(n_kb)]
        for d, (a_s, b_s) in enumerate(((af_s, bf_s), (ab_s, bb_s))):
            za = jnp.concatenate([jnp.dot(xb[kb], wa_h[d][kb], preferred_element_type=F32)
                                  for kb in range(n_kb)], axis=1)
            zi = jnp.concatenate([jnp.dot(xb[kb], wi_h[d][kb], preferred_element_type=F32)
                                  for kb in range(n_kb)], axis=1)
            ta = jnp.tanh(za + ba_h[d:d + 1, :])
            ti = jnp.tanh(zi + bi_h[d:d + 1, :])
            la_h = log2_a_half[d:d + 1, :]
            a = jnp.exp2(la_h + la_h * ta)
            q = 1.0 - a * a
            mult = jnp.where(q > 0.0, q * lax.rsqrt(q), 0.0)
            first = 0 if d == 0 else n_ctx - 1
            if r0 <= first < r0 + tr:
                mult = jnp.where(row_id == first - r0, 1.0, mult)
            a_s[r0:r0 + tr, :] = a
            b_s[r0:r0 + tr, :] = (mult * xh) * (1.0 + ti)

    row8 = lax.broadcasted_iota(jnp.int32, (sub, tc), 0)

    def tile_maps(a, b, reverse):
        for dlt in (1, 2, 4):
            shift = sub - dlt if reverse else dlt
            valid = (row8 < sub - dlt) if reverse else (row8 >= dlt)
            a_sh = jnp.where(valid, pltpu.roll(a, shift, 0), 1.0)
            b_sh = jnp.where(valid, pltpu.roll(b, shift, 0), 0.0)
            b = b + a * b_sh
            a = a * a_sh
        return a, b

    def scan_tile(tile_f, tile_b, carry):
        hf, hb = carry
        rf = pl.ds(pl.multiple_of(tile_f * sub, sub), sub)
        rb = pl.ds(pl.multiple_of(tile_b * sub, sub), sub)
        a_f, b_f = tile_maps(af_s[rf, :], bf_s[rf, :], False)
        a_b, b_b = tile_maps(ab_s[rb, :], bb_s[rb, :], True)
        h_f = b_f + a_f * hf
        h_b = b_b + a_b * hb
        bf_s[rf, :] = h_f
        bb_s[rb, :] = h_b
        return h_f[sub - 1:sub, :], h_b[0:1, :]

    zero = jnp.zeros((1, tc), F32)
    ctx_tiles = n_ctx // sub
    all_tiles = rows // sub
    carry = lax.fori_loop(0, ctx_tiles, lambda i, cr: scan_tile(i, ctx_tiles - 1 - i, cr), (zero, zero),
                          unroll=2)
    lax.fori_loop(ctx_tiles, all_tiles, lambda i, cr: scan_tile(i, all_tiles - 1 + ctx_tiles - i, cr), carry,
                  unroll=2)

    k0 = 0.7978845608028654
    for c in range(rows // tr):
        sl = slice(c * tr, (c + 1) * tr)
        y = y_ref[sl, :].astype(F32)
        gelu = 0.5 * y * (1.0 + jnp.tanh(k0 * (y + 0.044715 * (y * y * y))))
        o_ref[sl, :] = ((bf_s[sl, :] + bb_s[sl, :]) * gelu).astype(o_ref.dtype)


def _lru_call(p, conv_w, conv_b, w_a, b_a, w_i, b_i, lam, layer, n_ctx):
    b, r, _ = p.shape
    tc = 512
    nb = tc // LRU_BS
    chan = lambda bi, j: (layer, 0, j)
    return pl.pallas_call(
        functools.partial(_lru_kernel, n_ctx=n_ctx, tc=tc, tr=n_ctx),
        grid=(b, LRU_W // tc),
        in_specs=[
            pl.BlockSpec((None, r, tc), lambda bi, j: (bi, 0, COL_LX // tc + j)),
            pl.BlockSpec((None, r, tc), lambda bi, j: (bi, 0, COL_LY // tc + j)),
            pl.BlockSpec((None, CONV_W, tc), chan),
            pl.BlockSpec((None, 1, tc), chan),
            pl.BlockSpec((None, 2, nb, LRU_BS, LRU_BS), lambda bi, j: (layer, 0, j, 0, 0)),
            pl.BlockSpec((None, 2, tc), chan),
            pl.BlockSpec((None, 2, nb, LRU_BS, LRU_BS), lambda bi, j: (layer, 0, j, 0, 0)),
            pl.BlockSpec((None, 2, tc), chan),
            pl.BlockSpec((None, 2, tc), chan),
        ],
        out_specs=pl.BlockSpec((None, r, tc), lambda bi, j: (bi, 0, j)),
        out_shape=jax.ShapeDtypeStruct((b, r, LRU_W), BF16),
        scratch_shapes=[pltpu.VMEM((r + 3 * LRU_PAD, tc), F32)] + [pltpu.VMEM((r, tc), F32)] * 4,
        compiler_params=_cparams(2),
        name="rglru",
    )(p, p, conv_w, conv_b, w_a, b_a, w_i, b_i, lam)


def _merge_kernel(xa_ref, xb_ref, xc_ref, wa_ref, wb_ref, wc_ref, ga_ref, gb_ref, gc_ref, bm_ref, o_ref):
    acc = None
    for n, (x_ref, w_ref, g_ref) in enumerate(((xa_ref, wa_ref, ga_ref), (xb_ref, wb_ref, gb_ref),
                                               (xc_ref, wc_ref, gc_ref))):
        proj = jnp.dot(x_ref[...], w_ref[...], preferred_element_type=F32)
        term = _sigmoid(g_ref[...].astype(F32) + bm_ref[n:n + 1, :]) * proj
        acc = term if acc is None else acc + term
    o_ref[...] = acc.astype(o_ref.dtype)


def _merge_call(o_gla, o_att, o_lru, wbr, p, b_merge, layer, tm, tn):
    b, r, d = o_gla.shape
    ncol = d // tn
    x_spec = pl.BlockSpec((None, tm, d), lambda j, bi, i: (bi, i, 0))

    def w_spec(n):
        return pl.BlockSpec((None, None, d, tn), lambda j, bi, i: (layer, n, 0, j))

    def g_spec(n):
        return pl.BlockSpec((None, tm, tn), lambda j, bi, i: (bi, i, COL_GT // tn + n * ncol + j))

    return pl.pallas_call(
        _merge_kernel,
        grid=(ncol, b, r // tm),
        in_specs=[x_spec, x_spec, x_spec, w_spec(0), w_spec(1), w_spec(2), g_spec(0), g_spec(1), g_spec(2),
                  pl.BlockSpec((None, N_BRANCH, tn), lambda j, bi, i: (layer, 0, j))],
        out_specs=pl.BlockSpec((None, tm, tn), lambda j, bi, i: (bi, i, j)),
        out_shape=jax.ShapeDtypeStruct((b, r, d), BF16),
        compiler_params=_cparams(3),
        name="merge",
    )(o_gla, o_att, o_lru, wbr, wbr, wbr, p, p, p, b_merge)


def _merge_out_kernel(xa_ref, xb_ref, xc_ref, wa_ref, wb_ref, wc_ref, ga_ref, gb_ref, gc_ref, bm_ref, wo_ref,
                      h_ref, gc_gate_ref, gl_gate_ref, ng_ref, shc_ref, shl_ref, scc_ref, scl_ref,
                      o_ref, u_ref, acc_s, *, tm, n_ctx, n_col):
    j = pl.program_id(2)
    merged = None
    for n, (x_ref, w_ref, g_ref) in enumerate(((xa_ref, wa_ref, ga_ref), (xb_ref, wb_ref, gb_ref),
                                               (xc_ref, wc_ref, gc_ref))):
        proj = jnp.dot(x_ref[...], w_ref[...], preferred_element_type=F32)
        term = _sigmoid(g_ref[...].astype(F32) + bm_ref[n:n + 1, :]) * proj
        merged = term if merged is None else merged + term
    part = jnp.dot(merged.astype(BF16), wo_ref[...], preferred_element_type=F32)

    @pl.when(j == 0)
    def _start():
        acc_s[...] = part

    @pl.when(j > 0)
    def _accumulate():
        acc_s[...] += part

    @pl.when(j == n_col - 1)
    def _residual_and_norm():
        is_ctx = (pl.program_id(1) * tm + lax.broadcasted_iota(jnp.int32, (tm, 1), 0)) < n_ctx
        h_new = h_ref[...] + jnp.where(is_ctx, gc_gate_ref[...], gl_gate_ref[...]) * acc_s[...]
        o_ref[...] = h_new
        var = jnp.mean(h_new * h_new, axis=-1, keepdims=True)
        yn = h_new * lax.rsqrt(var + NORM_EPS) * ng_ref[...]
        shift = jnp.where(is_ctx, shc_ref[...], shl_ref[...])
        scale = jnp.where(is_ctx, scc_ref[...], scl_ref[...])
        u_ref[...] = (yn * (1.0 + scale) + shift).astype(u_ref.dtype)


def _merge_out_call(o_gla, o_att, o_lru, wbr, p, b_merge, w_out, h, mods4, g_ffn, layer, n_ctx, ctx_row, tm, tn):
    b, r, d = o_gla.shape
    ncol = d // tn
    x_spec = pl.BlockSpec((None, tm, d), lambda bi, i, j: (bi, i, 0))
    row_spec = pl.BlockSpec((None, tm, d), lambda bi, i, j: (bi, i, 0))

    def w_spec(n):
        return pl.BlockSpec((None, None, d, tn), lambda bi, i, j: (layer, n, 0, j))

    def g_spec(n):
        return pl.BlockSpec((None, tm, tn), lambda bi, i, j: (bi, i, COL_GT // tn + n * ncol + j))

    def mod_spec(k, is_ctx):
        return pl.BlockSpec((None, None, 1, d), lambda bi, i, j: (layer, ctx_row if is_ctx else bi, 0, k))

    return pl.pallas_call(
        functools.partial(_merge_out_kernel, tm=tm, n_ctx=n_ctx, n_col=ncol),
        grid=(b, r // tm, ncol),
        in_specs=[x_spec, x_spec, x_spec, w_spec(0), w_spec(1), w_spec(2), g_spec(0), g_spec(1), g_spec(2),
                  pl.BlockSpec((None, N_BRANCH, tn), lambda bi, i, j: (layer, 0, j)),
                  pl.BlockSpec((None, tn, d), lambda bi, i, j: (layer, j, 0)),
                  row_spec, mod_spec(2, True), mod_spec(2, False),
                  pl.BlockSpec((None, 1, d), lambda bi, i, j: (layer, 0, 0)),
                  mod_spec(3, True), mod_spec(3, False), mod_spec(4, True), mod_spec(4, False)],
        out_specs=[row_spec, row_spec],
        out_shape=[jax.ShapeDtypeStruct((b, r, d), F32), jax.ShapeDtypeStruct((b, r, d), BF16)],
        scratch_shapes=[pltpu.VMEM((tm, d), F32)],
        compiler_params=_cparams(3, vmem=V7X_VMEM_LIMIT_BYTES + 4 * 1024 * 1024),
        name="merge_out",
    )(o_gla, o_att, o_lru, wbr, wbr, wbr, p, p, p, b_merge, w_out, h, mods4, mods4, g_ffn,
      mods4, mods4, mods4, mods4)


def _final_norm_kernel(h_ref, g_ref, o_ref):
    x = h_ref[...]
    var = jnp.mean(x * x, axis=-1, keepdims=True)
    o_ref[...] = x * lax.rsqrt(var + NORM_EPS) * g_ref[...]


def _final_norm_call(h, g, n_ctx):
    b, r, d = h.shape
    tr = n_ctx
    return pl.pallas_call(
        _final_norm_kernel,
        grid=(b, (r - n_ctx) // tr),
        in_specs=[
            pl.BlockSpec((None, tr, d), lambda bi, i: (bi, i + 1, 0)),
            pl.BlockSpec((1, d), lambda bi, i: (0, 0)),
        ],
        out_specs=pl.BlockSpec((None, tr, d), lambda bi, i: (bi, i, 0)),
        out_shape=jax.ShapeDtypeStruct((b, r - n_ctx, d), F32),
        compiler_params=_cparams(2),
        name="final_norm",
    )(h, g)


def _rope_tables(n_tokens):
    rows = n_tokens // GRID_W
    row = jnp.repeat(jnp.arange(rows, dtype=F32), GRID_W)
    col = jnp.tile(jnp.arange(GRID_W, dtype=F32), rows)
    inv = ROPE_THETA ** (-jnp.arange(ROPE_AXIS_PAIRS, dtype=F32) / ROPE_AXIS_PAIRS)
    ang = jnp.concatenate([row[:, None] * inv, col[:, None] * inv], axis=-1)
    cos, sin = jnp.cos(ang), jnp.sin(ang)
    cos_t = jnp.repeat(cos, 2, axis=-1)
    sin_t = jnp.stack([-sin, sin], axis=-1).reshape(n_tokens, ATT_HD)
    return cos_t, sin_t


def kernel(x, c, ctx, c_ctx, w_mod, b_mod, norm_mix_g, norm_ffn_g, w_in, gla_w_decay, gla_b_decay, gla_norm_g, q_norm_g, k_norm_g, conv_w, conv_b, lru_w_a, lru_b_a, lru_w_i, lru_b_i, lru_lambda, b_merge, w_branch, w_out, w_ffn_in, w_ffn_out, final_norm_g):
    depth = w_in.shape[0]
    batch, t_lat, d = x.shape
    n_ctx = ctx.shape[1]
    rows = n_ctx + t_lat
    ctx_row = batch
    assert batch < MOD_ROWS and rows % n_ctx == 0 and n_ctx % GLA_CHUNK == 0
    tm = 768 if rows % 768 == 0 else n_ctx
    tm_big = 1152 if rows % 1152 == 0 else tm
    tm_small = 384 if rows % 384 == 0 else n_ctx
    tn_in, tn_merge, tn_out, tn_ffn_in, tn_ffn_out = 1024, 512, d, 512, 1024

    w_in_t = jnp.swapaxes(w_in, 1, 2)
    w_branch_b = w_branch.astype(BF16)
    w_out_b = w_out.astype(BF16)
    w_ffn_out_b = w_ffn_out.astype(BF16)
    wd_pad = jnp.zeros((depth, 2, DEC_PAD, GLA_DK), F32)
    wd_pad = wd_pad.at[:, 0, :GLA_RANK].set(gla_w_decay[:, 0])
    wd_pad = wd_pad.at[:, 1, GLA_RANK:2 * GLA_RANK].set(gla_w_decay[:, 1])
    bd = gla_b_decay.reshape(depth, 2, 1, GLA_DK)
    cos_t, sin_t = _rope_tables(t_lat)

    c16 = jnp.zeros((MOD_ROWS, d), F32).at[:batch].set(c).at[ctx_row].set(c_ctx)
    mods4 = _mods_call(c16, w_mod, b_mod).reshape(depth, MOD_ROWS, 1, N_MOD * d)

    g_mix = norm_mix_g.reshape(depth, 1, d)
    g_ffn = norm_ffn_g.reshape(depth, 1, d)
    g_gla = gla_norm_g.reshape(depth, 1, GLA_HDV)
    g_q = q_norm_g.reshape(depth, 1, ATT_HD)
    g_k = k_norm_g.reshape(depth, 1, ATT_HD)
    cb3 = conv_b.reshape(depth, 1, LRU_W)

    h = jnp.concatenate([ctx, x], axis=1)
    for l in range(depth):
        u, dec = _normmod_call(h, g_mix, mods4, w_in_t, l, 0, 1, n_ctx, ctx_row, tm)
        p = _in_proj_call(u, w_in_t, l, rows, tn_in)
        o_att = _attn_call(p, g_q, g_k, cos_t, sin_t, l, n_ctx)
        o_gla = _gla_call(p, dec, wd_pad, bd, g_gla, l, n_ctx)
        o_lru = _lru_call(p, conv_w, cb3, lru_w_a, lru_b_a, lru_w_i, lru_b_i, lru_lambda, l, n_ctx)
        h, u2 = _merge_out_call(o_gla, o_att, o_lru, w_branch_b, p, b_merge, w_out_b, h, mods4, g_ffn, l,
                                n_ctx, ctx_row, tm_small, tn_merge)
        a = _swiglu_call(u2, w_ffn_in, l, tm_big, tn_ffn_in)
        h = _linear_res_call(a, w_ffn_out_b, l, h, mods4, 5, n_ctx, ctx_row, tm_small, tn_ffn_out, "ffn_out")
    return _final_norm_call(h, final_norm_g.reshape(1, d), n_ctx)
```

```python
import functools

import jax
import jax.numpy as jnp
import numpy as np
from jax import lax
from jax.experimental import pallas as pl
from jax.experimental.pallas import tpu as pltpu

F32 = jnp.float32
BF16 = jnp.bfloat16

D_MODEL = 2048
GRID_W = 64
NORM_EPS = 1e-6
N_MOD = 6

GLA_HEADS = 4
GLA_DK = D_MODEL // 2
GLA_DV = D_MODEL
GLA_HDK = GLA_DK // GLA_HEADS
GLA_HDV = GLA_DV // GLA_HEADS
GLA_RANK = 16
GLA_GATE_NORM = 16.0
GLA_CHUNK = 64
GLA_HEADS_PER_STEP = 2

ATT_HEADS = 16
ATT_KV_HEADS = 4
ATT_HD = 128
ATT_GROUP = ATT_HEADS // ATT_KV_HEADS
ATT_Q = ATT_HEADS * ATT_HD
ATT_KV = ATT_KV_HEADS * ATT_HD
ROPE_THETA = 10000.0
ROPE_AXIS_PAIRS = ATT_HD // 4

LRU_W = D_MODEL
LRU_BLOCKS = 16
LRU_BS = LRU_W // LRU_BLOCKS
LRU_C = 8.0
CONV_W = 4
CONV_LEFT = 2

N_BRANCH = 3
FFN_HIDDEN = -(-8 * D_MODEL // (3 * 256)) * 256

COL_GQ = 0
COL_GK = COL_GQ + GLA_DK
COL_GV = COL_GK + GLA_DK
COL_GR = COL_GV + GLA_DV
COL_AQ = COL_GR + GLA_DV
COL_AK = COL_AQ + ATT_Q
COL_AV = COL_AK + ATT_KV
COL_LX = COL_AV + ATT_KV
COL_LY = COL_LX + LRU_W
COL_GT = COL_LY + LRU_W
N_MAIN = COL_GT + N_BRANCH * D_MODEL
DEC_ORIG = 2 * GLA_DK + 2 * GLA_DV
DEC_PAD = 128

V7X_VMEM_LIMIT_BYTES = 56 * 1024 * 1024
MOD_ROWS = 16


def _cparams(n_axes, vmem=V7X_VMEM_LIMIT_BYTES):
    return pltpu.CompilerParams(dimension_semantics=("arbitrary",) * n_axes, vmem_limit_bytes=vmem)


def _sigmoid(x):
    return 0.5 * jnp.tanh(0.5 * x) + 0.5


def _silu(x):
    h = 0.5 * x
    return h + h * jnp.tanh(h)


def _mods_kernel(c_ref, w_ref, b_ref, o_ref):
    c = c_ref[...]
    sc = _silu(c).astype(BF16)
    o_ref[...] = jnp.dot(sc, w_ref[...].astype(BF16), preferred_element_type=F32) + b_ref[...]


def _mods_call(c16, w_mod, b_mod):
    depth, d, n = w_mod.shape
    tn = 1024
    return pl.pallas_call(
        _mods_kernel,
        grid=(depth, n // tn),
        in_specs=[
            pl.BlockSpec((MOD_ROWS, d), lambda l, j: (0, 0)),
            pl.BlockSpec((None, d, tn), lambda l, j: (l, 0, j)),
            pl.BlockSpec((None, 1, tn), lambda l, j: (l, 0, j)),
        ],
        out_specs=pl.BlockSpec((None, MOD_ROWS, tn), lambda l, j: (l, 0, j)),
        out_shape=jax.ShapeDtypeStruct((depth, MOD_ROWS, n), F32),
        compiler_params=_cparams(2),
        name="mods",
    )(c16, w_mod, b_mod.reshape(depth, 1, n))


def _normmod_kernel(h_ref, g_ref, shc_ref, shl_ref, scc_ref, scl_ref, wt_ref, o_ref, dec_ref, *, tr, n_ctx):
    x = h_ref[...]
    var = jnp.mean(x * x, axis=-1, keepdims=True)
    y = x * lax.rsqrt(var + NORM_EPS) * g_ref[...]
    is_ctx = (pl.program_id(1) * tr + lax.broadcasted_iota(jnp.int32, (tr, 1), 0)) < n_ctx
    shift = jnp.where(is_ctx, shc_ref[...], shl_ref[...])
    scale = jnp.where(is_ctx, scc_ref[...], scl_ref[...])
    u = (y * (1.0 + scale) + shift).astype(o_ref.dtype)
    o_ref[...] = u
    dec_ref[...] = lax.dot_general(u, wt_ref[...].astype(BF16), (((1,), (1,)), ((), ())),
                                   preferred_element_type=F32)


def _normmod_call(h, g, mods4, w_in_t, layer, k_shift, k_scale, n_ctx, ctx_row, tr):
    b, r, d = h.shape

    def mod_spec(k, is_ctx):
        return pl.BlockSpec((None, None, 1, d), lambda bi, i: (layer, ctx_row if is_ctx else bi, 0, k))

    return pl.pallas_call(
        functools.partial(_normmod_kernel, tr=tr, n_ctx=n_ctx),
        grid=(b, r // tr),
        in_specs=[
            pl.BlockSpec((None, tr, d), lambda bi, i: (bi, i, 0)),
            pl.BlockSpec((None, 1, d), lambda bi, i: (layer, 0, 0)),
            mod_spec(k_shift, True), mod_spec(k_shift, False),
            mod_spec(k_scale, True), mod_spec(k_scale, False),
            pl.BlockSpec((None, DEC_PAD, d), lambda bi, i: (layer, DEC_ORIG // DEC_PAD, 0)),
        ],
        out_specs=[pl.BlockSpec((None, tr, d), lambda bi, i: (bi, i, 0)),
                   pl.BlockSpec((None, tr, DEC_PAD), lambda bi, i: (bi, i, 0))],
        out_shape=[jax.ShapeDtypeStruct((b, r, d), BF16), jax.ShapeDtypeStruct((b, r, DEC_PAD), F32)],
        compiler_params=_cparams(2),
        name="normmod",
    )(h, g, mods4, mods4, mods4, mods4, w_in_t)


def _in_proj_kernel(x_ref, wa_ref, wb_ref, o_ref, *, n_plain, shift):
    j = pl.program_id(0)
    nt = (((1,), (1,)), ((), ()))

    @pl.when(j < n_plain)
    def _before_decay_columns():
        w = wa_ref[...].astype(BF16)
        o_ref[...] = lax.dot_general(x_ref[...], w, nt, preferred_element_type=F32).astype(o_ref.dtype)

    @pl.when(j >= n_plain)
    def _after_decay_columns():
        w = jnp.concatenate([wa_ref[shift:, :], wb_ref[...]], axis=0).astype(BF16)
        o_ref[...] = lax.dot_general(x_ref[...], w, nt, preferred_element_type=F32).astype(o_ref.dtype)


def _in_proj_call(x, w_in_t, layer, tm, tn):
    b, r, k = x.shape
    shift = 2 * GLA_RANK
    assert DEC_ORIG % tn == 0 and N_MAIN % tn == 0 and tn % shift == 0
    return pl.pallas_call(
        functools.partial(_in_proj_kernel, n_plain=DEC_ORIG // tn, shift=shift),
        grid=(N_MAIN // tn, b, r // tm),
        in_specs=[
            pl.BlockSpec((None, tm, k), lambda j, bi, i: (bi, i, 0)),
            pl.BlockSpec((None, tn, k), lambda j, bi, i: (layer, j, 0)),
            pl.BlockSpec((None, shift, k), lambda j, bi, i: (layer, (j + 1) * (tn // shift), 0)),
        ],
        out_specs=pl.BlockSpec((None, tm, tn), lambda j, bi, i: (bi, i, j)),
        out_shape=jax.ShapeDtypeStruct((b, r, N_MAIN), BF16),
        compiler_params=_cparams(3),
        name="in_proj",
    )(x, w_in_t, w_in_t)


def _linear_res_kernel(*refs, tm, n_ctx, with_norm):
    x_ref, w_ref, h_ref, gc_ref, gl_ref = refs[:5]
    if with_norm:
        ng_ref, shc_ref, shl_ref, scc_ref, scl_ref, o_ref, u_ref = refs[5:]
    else:
        (o_ref,) = refs[5:]
    i = pl.program_id(2)
    y = jnp.dot(x_ref[...], w_ref[...], preferred_element_type=F32)
    is_ctx = (i * tm + lax.broadcasted_iota(jnp.int32, (tm, 1), 0)) < n_ctx
    h_new = h_ref[...] + jnp.where(is_ctx, gc_ref[...], gl_ref[...]) * y
    o_ref[...] = h_new
    if with_norm:
        var = jnp.mean(h_new * h_new, axis=-1, keepdims=True)
        yn = h_new * lax.rsqrt(var + NORM_EPS) * ng_ref[...]
        shift = jnp.where(is_ctx, shc_ref[...], shl_ref[...])
        scale = jnp.where(is_ctx, scc_ref[...], scl_ref[...])
        u_ref[...] = (yn * (1.0 + scale) + shift).astype(u_ref.dtype)


def _linear_res_call(x, w3, layer, h, mods4, k_gate, n_ctx, ctx_row, tm, tn, name, norm=None):
    b, r, k = x.shape
    n = w3.shape[-1]
    ngate = n // tn

    def mod_spec(lyr, row_of, kk):
        return pl.BlockSpec((None, None, 1, tn), lambda j, bi, i: (lyr, row_of(bi), 0, kk * ngate + j))

    ctx_of = lambda bi: ctx_row
    lat_of = lambda bi: bi
    w_mode = dict(pipeline_mode=pl.Buffered(1)) if tn == n else {}
    in_specs = [
        pl.BlockSpec((None, tm, k), lambda j, bi, i: (bi, i, 0)),
        pl.BlockSpec((None, k, tn), lambda j, bi, i: (layer, 0, j), **w_mode),
        pl.BlockSpec((None, tm, tn), lambda j, bi, i: (bi, i, j)),
        mod_spec(layer, ctx_of, k_gate),
        mod_spec(layer, lat_of, k_gate),
    ]
    args = [x, w3, h, mods4, mods4]
    out_specs = [pl.BlockSpec((None, tm, tn), lambda j, bi, i: (bi, i, j))]
    out_shape = [jax.ShapeDtypeStruct((b, r, n), F32)]
    if norm is not None:
        gains, n_layer, k_shift, k_scale = norm
        assert tn == n
        in_specs += [
            pl.BlockSpec((None, 1, n), lambda j, bi, i: (n_layer, 0, 0)),
            mod_spec(n_layer, ctx_of, k_shift), mod_spec(n_layer, lat_of, k_shift),
            mod_spec(n_layer, ctx_of, k_scale), mod_spec(n_layer, lat_of, k_scale),
        ]
        args += [gains, mods4, mods4, mods4, mods4]
        out_specs.append(pl.BlockSpec((None, tm, tn), lambda j, bi, i: (bi, i, j)))
        out_shape.append(jax.ShapeDtypeStruct((b, r, n), BF16))
    res = pl.pallas_call(
        functools.partial(_linear_res_kernel, tm=tm, n_ctx=n_ctx, with_norm=norm is not None),
        grid=(n // tn, b, r // tm),
        in_specs=in_specs,
        out_specs=out_specs,
        out_shape=out_shape,
        compiler_params=_cparams(3),
        name=name,
    )(*args)
    return res if norm is not None else res[0]


def _swiglu_kernel(x_ref, wg_ref, wu_ref, o_ref):
    x = x_ref[...]
    g = jnp.dot(x, wg_ref[...].astype(BF16), preferred_element_type=F32)
    up = jnp.dot(x, wu_ref[...].astype(BF16), preferred_element_type=F32)
    o_ref[...] = (g * _sigmoid(g) * up).astype(o_ref.dtype)


def _swiglu_call(x, w3, layer, tm, tn):
    b, r, k = x.shape
    hid = w3.shape[-1] // 2
    nb = hid // tn
    return pl.pallas_call(
        _swiglu_kernel,
        grid=(nb, b, r // tm),
        in_specs=[
            pl.BlockSpec((None, tm, k), lambda j, bi, i: (bi, i, 0)),
            pl.BlockSpec((None, k, tn), lambda j, bi, i: (layer, 0, j)),
            pl.BlockSpec((None, k, tn), lambda j, bi, i: (layer, 0, nb + j)),
        ],
        out_specs=pl.BlockSpec((None, tm, tn), lambda j, bi, i: (bi, i, j)),
        out_shape=jax.ShapeDtypeStruct((b, r, hid), BF16),
        compiler_params=_cparams(3),
        name="ffn_in",
    )(x, w3, w3)


def _norm_rope(x, gain, cs, sn):
    var = jnp.mean(x * x, axis=-1, keepdims=True)
    y = x * lax.rsqrt(var + NORM_EPS) * gain
    if cs is None:
        return y
    lane = lax.broadcasted_iota(jnp.int32, y.shape, 1)
    nxt = pltpu.roll(y, ATT_HD - 1, 1)
    prv = pltpu.roll(y, 1, 1)
    partner = jnp.where(lane % 2 == 0, nxt, prv)
    return y * cs + partner * sn


def _attn_kernel(q_ref, k_ref, v_ref, qg_ref, kg_ref, cos_ref, sin_ref, o_ref, kt_s, vx_s, *, n_ctx, tq):
    qi = pl.program_id(2)
    rows = k_ref.shape[0]
    scale = ATT_HD ** -0.5 * float(np.log2(np.e))

    @pl.when(qi == 0)
    def _prepare_kv():
        kg = kg_ref[...]
        for c in range(rows // tq):
            kc = k_ref[c * tq:(c + 1) * tq, :].astype(F32)
            if c == 0:
                kn = _norm_rope(kc, kg, None, None)
            else:
                kn = _norm_rope(kc, kg, cos_ref[(c - 1) * tq:c * tq, :], sin_ref[(c - 1) * tq:c * tq, :])
            kt_s[:, c * tq:(c + 1) * tq] = kn.T.astype(BF16)
        lane = lax.broadcasted_iota(jnp.int32, (rows, ATT_HD), 1)
        vx_s[:, :ATT_HD] = v_ref[...]
        vx_s[:, ATT_HD:] = jnp.where(lane == 0, 1.0, 0.0).astype(BF16)

    def attend(cs, sn, n_keys):
        qg = qg_ref[...]
        scores = []
        for g in range(ATT_GROUP):
            q = _norm_rope(q_ref[:, g * ATT_HD:(g + 1) * ATT_HD].astype(F32), qg, cs, sn) * scale
            scores.append(jnp.dot(q.astype(BF16), kt_s[:, :n_keys], preferred_element_type=F32))
        for g in range(ATT_GROUP):
            s = scores[g]
            p = jnp.exp2(s - jnp.max(s, axis=-1, keepdims=True))
            ox = jnp.dot(p.astype(BF16), vx_s[:n_keys, :], preferred_element_type=F32)
            o = ox[:, :ATT_HD] / ox[:, ATT_HD:ATT_HD + 1]
            o_ref[:, g * ATT_HD:(g + 1) * ATT_HD] = o.astype(o_ref.dtype)

    @pl.when(qi == 0)
    def _context_queries():
        attend(None, None, n_ctx)

    @pl.when(qi > 0)
    def _latent_queries():
        off = pl.multiple_of((qi - 1) * tq, tq)
        attend(cos_ref[pl.ds(off, tq), :], sin_ref[pl.ds(off, tq), :], rows)


def _attn_call(p, q_g, k_g, cos_t, sin_t, layer, n_ctx):
    b, r, _ = p.shape
    tq = n_ctx
    t = r - n_ctx
    qw = ATT_GROUP * ATT_HD
    return pl.pallas_call(
        functools.partial(_attn_kernel, n_ctx=n_ctx, tq=tq),
        grid=(b, ATT_KV_HEADS, r // tq),
        in_specs=[
            pl.BlockSpec((None, tq, qw), lambda bi, kv, qi: (bi, qi, COL_AQ // qw + kv)),
            pl.BlockSpec((None, r, ATT_HD), lambda bi, kv, qi: (bi, 0, COL_AK // ATT_HD + kv)),
            pl.BlockSpec((None, r, ATT_HD), lambda bi, kv, qi: (bi, 0, COL_AV // ATT_HD + kv)),
            pl.BlockSpec((None, 1, ATT_HD), lambda bi, kv, qi: (layer, 0, 0)),
            pl.BlockSpec((None, 1, ATT_HD), lambda bi, kv, qi: (layer, 0, 0)),
            pl.BlockSpec((t, ATT_HD), lambda bi, kv, qi: (0, 0)),
            pl.BlockSpec((t, ATT_HD), lambda bi, kv, qi: (0, 0)),
        ],
        out_specs=pl.BlockSpec((None, tq, qw), lambda bi, kv, qi: (bi, qi, kv)),
        out_shape=jax.ShapeDtypeStruct((b, r, ATT_Q), BF16),
        scratch_shapes=[pltpu.VMEM((ATT_HD, r), BF16), pltpu.VMEM((r, 2 * ATT_HD), BF16)],
        compiler_params=_cparams(3),
        name="gqa",
    )(p, p, p, q_g, k_g, cos_t, sin_t)


def _gla_direction(q, k, v, dec, wd, bd, vis, st_s, reverse, n_chunks):
    c_len = GLA_CHUNK

    tri = vis > 0.5
    tri_b = vis.astype(BF16)
    nt = (((1,), (1,)), ((), ()))
    tn = (((0,), (0,)), ((), ()))

    x = jnp.dot(dec, wd, preferred_element_type=F32) + bd
    g = (jnp.minimum(x, 0.0) - jnp.log(1.0 + jnp.exp(-jnp.abs(x)))) * (1.0 / GLA_GATE_NORM)
    g_hi = g.astype(BF16)
    g_lo = (g - g_hi.astype(F32)).astype(BF16)
    bcum = jnp.dot(tri_b, g_hi, preferred_element_type=F32) + jnp.dot(tri_b, g_lo, preferred_element_type=F32)
    q = q.astype(F32) * (GLA_HDK ** -0.5)
    k = k.astype(F32)
    q_dec = (q * jnp.exp(bcum)).astype(BF16)
    k_inv = (k * jnp.exp(-bcum)).astype(BF16)
    att = lax.dot_general(q_dec, k_inv, nt, preferred_element_type=F32)
    att = jnp.where(tri, att, 0.0).astype(BF16)
    o_intra = jnp.dot(att, v, preferred_element_type=F32)
    b_last, k_dec = [], []
    for c in range(n_chunks):
        sl = slice(c * c_len, (c + 1) * c_len)
        end = c * c_len if reverse else (c + 1) * c_len - 1
        b_last.append(bcum[end:end + 1, :])
        k_dec.append((k[sl, :] * jnp.exp(b_last[c] - bcum[sl, :])).astype(BF16))

    outs = [None] * n_chunks
    chunk_order = range(n_chunks - 1, -1, -1) if reverse else range(n_chunks)
    st = st_s[...]
    for c in chunk_order:
        sl = slice(c * c_len, (c + 1) * c_len)
        outs[c] = o_intra[sl, :] + lax.dot_general(q_dec[sl, :], st.astype(BF16), nt, preferred_element_type=F32)
        st = st * jnp.exp(b_last[c]) + lax.dot_general(v[sl, :], k_dec[c], tn, preferred_element_type=F32)
    st_s[...] = st
    return jnp.concatenate(outs, axis=0)


def _gla_kernel(qf_ref, kf_ref, vf_ref, decf_ref, qb_ref, kb_ref, vb_ref, decb_ref,
                wdf_ref, bdf_ref, wdb_ref, bdb_ref, visf_ref, visb_ref, r_ref, ng_ref, o_ref,
                stf_s, stb_s, of_s, ob_s, *, n_chunks, nblk):
    t = pl.program_id(2)
    tb = n_chunks * GLA_CHUNK

    @pl.when(t == 0)
    def _zero_states():
        stf_s[...] = jnp.zeros_like(stf_s)
        stb_s[...] = jnp.zeros_like(stb_s)

    blk_b = jnp.where(t == 0, 0, nblk - t)
    rows_f = pl.ds(pl.multiple_of(t * tb, tb), tb)
    rows_b = pl.ds(pl.multiple_of(blk_b * tb, tb), tb)
    decf = decf_ref[...].astype(BF16)
    decb = decb_ref[...].astype(BF16)
    visf = visf_ref[...]
    visb = visb_ref[...]
    for hh in range(GLA_HEADS_PER_STEP):
        kcol = slice(hh * GLA_HDK, (hh + 1) * GLA_HDK)
        vcol = slice(hh * GLA_HDV, (hh + 1) * GLA_HDV)
        of_s[rows_f, vcol] = _gla_direction(
            qf_ref[:, kcol], kf_ref[:, kcol], vf_ref[:, vcol], decf, wdf_ref[:, kcol].astype(BF16),
            bdf_ref[:, kcol], visf, stf_s.at[hh], False, n_chunks)
        ob_s[rows_b, vcol] = _gla_direction(
            qb_ref[:, kcol], kb_ref[:, kcol], vb_ref[:, vcol], decb, wdb_ref[:, kcol].astype(BF16),
            bdb_ref[:, kcol], visb, stb_s.at[hh], True, n_chunks)

    @pl.when(t == nblk - 1)
    def _combine_norm_gate():
        for c in range(nblk):
            sl = slice(c * tb, (c + 1) * tb)
            for hh in range(GLA_HEADS_PER_STEP):
                vcol = slice(hh * GLA_HDV, (hh + 1) * GLA_HDV)
                o = of_s[sl, vcol] + ob_s[sl, vcol]
                var = jnp.mean(o * o, axis=-1, keepdims=True)
                y = o * lax.rsqrt(var + NORM_EPS) * ng_ref[...]
                r = r_ref[sl, vcol].astype(F32)
                o_ref[sl, vcol] = (y * _silu(r)).astype(o_ref.dtype)


def _gla_call(p, dec, wd_pad, bd, norm_g, layer, n_ctx):
    b, r, _ = p.shape
    tb = n_ctx
    nblk = r // tb

    def blk(t, direction):
        if direction == 0:
            return t
        return jnp.where(t == 0, 0, nblk - t)

    kw = GLA_HEADS_PER_STEP * GLA_HDK
    vw = GLA_HEADS_PER_STEP * GLA_HDV

    def stream_specs(direction):
        return [
            pl.BlockSpec((None, tb, kw), lambda bi, h, t: (bi, blk(t, direction), COL_GQ // kw + h)),
            pl.BlockSpec((None, tb, kw), lambda bi, h, t: (bi, blk(t, direction), COL_GK // kw + h)),
            pl.BlockSpec((None, tb, vw), lambda bi, h, t: (bi, blk(t, direction), COL_GV // vw + h)),
            pl.BlockSpec((None, tb, DEC_PAD), lambda bi, h, t: (bi, blk(t, direction), 0)),
        ]

    def decay_specs(direction):
        return [
            pl.BlockSpec((None, None, DEC_PAD, kw), lambda bi, h, t: (layer, direction, 0, h)),
            pl.BlockSpec((None, None, 1, kw), lambda bi, h, t: (layer, direction, 0, h)),
        ]

    idx = np.arange(tb)
    same_chunk = (idx[:, None] // GLA_CHUNK) == (idx[None, :] // GLA_CHUNK)
    vis = jnp.asarray(np.stack([same_chunk & (idx[:, None] >= idx[None, :]),
                                same_chunk & (idx[None, :] >= idx[:, None])]).astype(np.float32))

    return pl.pallas_call(
        functools.partial(_gla_kernel, n_chunks=tb // GLA_CHUNK, nblk=nblk),
        grid=(b, GLA_HEADS // GLA_HEADS_PER_STEP, nblk),
        in_specs=stream_specs(0) + stream_specs(1) + decay_specs(0) + decay_specs(1) + [
            pl.BlockSpec((None, tb, tb), lambda bi, h, t: (0, 0, 0)),
            pl.BlockSpec((None, tb, tb), lambda bi, h, t: (1, 0, 0)),
            pl.BlockSpec((None, r, vw), lambda bi, h, t: (bi, 0, COL_GR // vw + h)),
            pl.BlockSpec((None, 1, GLA_HDV), lambda bi, h, t: (layer, 0, 0)),
        ],
        out_specs=pl.BlockSpec((None, r, vw), lambda bi, h, t: (bi, 0, h)),
        out_shape=jax.ShapeDtypeStruct((b, r, GLA_DV), BF16),
        scratch_shapes=([pltpu.VMEM((GLA_HEADS_PER_STEP, GLA_HDV, GLA_HDK), F32)] * 2
                        + [pltpu.VMEM((r, vw), F32)] * 2),
        compiler_params=_cparams(3),
        name="gla",
    )(p, p, p, dec, p, p, p, dec, wd_pad, bd, wd_pad, bd, vis, vis, p, norm_g)


LRU_PAD = 8


def _lru_kernel(x_ref, y_ref, cw_ref, cb_ref, wa_ref, ba_ref, wi_ref, bi_ref, lam_ref, o_ref,
                xp_s, af_s, bf_s, ab_s, bb_s, *, n_ctx, tc, tr):
    rows = x_ref.shape[0]
    n_lat = rows - n_ctx
    sub = 8
    ctx0 = LRU_PAD
    lat0 = 2 * LRU_PAD + n_ctx
    zpad = jnp.zeros((LRU_PAD, tc), F32)
    xp_s[0:LRU_PAD, :] = zpad
    xp_s[ctx0 + n_ctx:lat0, :] = zpad
    xp_s[lat0 + n_lat:lat0 + n_lat + LRU_PAD, :] = zpad
    xp_s[ctx0:ctx0 + n_ctx, :] = x_ref[0:n_ctx, :].astype(F32)
    xp_s[lat0:lat0 + n_lat, :] = x_ref[n_ctx:rows, :].astype(F32)

    cw = cw_ref[...]
    cb = cb_ref[...]
    lam = lam_ref[...]
    neg_sp = -(jnp.maximum(-lam, 0.0) + jnp.log1p(jnp.exp(-jnp.abs(lam))))
    log2_a_half = (0.5 * LRU_C * float(np.log2(np.e))) * neg_sp
    row_id = lax.broadcasted_iota(jnp.int32, (tr, 1), 0)
    n_kb = tc // LRU_BS
    wa_h = [[(0.5 * wa_ref[d, kb]).astype(BF16) for kb in range(n_kb)] for d in range(2)]
    wi_h = [[(0.5 * wi_ref[d, kb]).astype(BF16) for kb in range(n_kb)] for d in range(2)]
    ba_h = 0.5 * ba_ref[...]
    bi_h = 0.5 * bi_ref[...]

    for c in range(rows // tr):
        r0 = c * tr
        p0 = (ctx0 if r0 < n_ctx else lat0 - n_ctx) + r0
        n_win = tr + 2 * sub
        win = xp_s[p0 - sub:p0 + tr + sub, :]
        xc = cb + win[sub:sub + tr, :] * cw[CONV_LEFT:CONV_LEFT + 1, :]
        for j in range(CONV_W):
            off = j - CONV_LEFT
            if off != 0:
                xc = xc + pltpu.roll(win, (-off) % n_win, 0)[sub:sub + tr, :] * cw[j:j + 1, :]
        xh = 0.5 * xc
        xb = [xc[:, kb * LRU_BS:(kb + 1) * LRU_BS].astype(BF16) for kb in range(n_kb)]
        for d, (a_s, b_s) in enumerate(((af_s, bf_s), (ab_s, bb_s))):
            za = jnp.concatenate([jnp.dot(xb[kb], wa_h[d][kb], preferred_element_type=F32)
                                  for kb in range(n_kb)], axis=1)
            zi = jnp.concatenate([jnp.dot(xb[kb], wi_h[d][kb], preferred_element_type=F32)
                                  for kb in range(n_kb)], axis=1)
            ta = jnp.tanh(za + ba_h[d:d + 1, :])
            ti = jnp.tanh(zi + bi_h[d:d + 1, :])
            la_h = log2_a_half[d:d + 1, :]
            a = jnp.exp2(la_h + la_h * ta)
            q = 1.0 - a * a
            mult = jnp.where(q > 0.0, q * lax.rsqrt(q), 0.0)
            first = 0 if d == 0 else n_ctx - 1
            if r0 <= first < r0 + tr:
                mult = jnp.where(row_id == first - r0, 1.0, mult)
            a_s[r0:r0 + tr, :] = a
            b_s[r0:r0 + tr, :] = (mult * xh) * (1.0 + ti)

    row8 = lax.broadcasted_iota(jnp.int32, (sub, tc), 0)

    def tile_maps(a, b, reverse):
        for dlt in (1, 2, 4):
            shift = sub - dlt if reverse else dlt
            valid = (row8 < sub - dlt) if reverse else (row8 >= dlt)
            a_sh = jnp.where(valid, pltpu.roll(a, shift, 0), 1.0)
            b_sh = jnp.where(valid, pltpu.roll(b, shift, 0), 0.0)
            b = b + a * b_sh
            a = a * a_sh
        return a, b

    def scan_tile(tile_f, tile_b, carry):
        hf, hb = carry
        rf = pl.ds(pl.multiple_of(tile_f * sub, sub), sub)
        rb = pl.ds(pl.multiple_of(tile_b * sub, sub), sub)
        a_f, b_f = tile_maps(af_s[rf, :], bf_s[rf, :], False)
        a_b, b_b = tile_maps(ab_s[rb, :], bb_s[rb, :], True)
        h_f = b_f + a_f * hf
        h_b = b_b + a_b * hb
        bf_s[rf, :] = h_f
        bb_s[rb, :] = h_b
        return h_f[sub - 1:sub, :], h_b[0:1, :]

    zero = jnp.zeros((1, tc), F32)
    ctx_tiles = n_ctx // sub
    all_tiles = rows // sub
    carry = lax.fori_loop(0, ctx_tiles, lambda i, cr: scan_tile(i, ctx_tiles - 1 - i, cr), (zero, zero),
                          unroll=2)
    lax.fori_loop(ctx_tiles, all_tiles, lambda i, cr: scan_tile(i, all_tiles - 1 + ctx_tiles - i, cr), carry,
                  unroll=2)

    k0 = 0.7978845608028654
    for c in range(rows // tr):
        sl = slice(c * tr, (c + 1) * tr)
        y = y_ref[sl, :].astype(F32)
        gelu = 0.5 * y * (1.0 + jnp.tanh(k0 * (y + 0.044715 * (y * y * y))))
        o_ref[sl, :] = ((bf_s[sl, :] + bb_s[sl, :]) * gelu).astype(o_ref.dtype)


def _lru_call(p, conv_w, conv_b, w_a, b_a, w_i, b_i, lam, layer, n_ctx):
    b, r, _ = p.shape
    tc = 512
    nb = tc // LRU_BS
    chan = lambda bi, j: (layer, 0, j)
    return pl.pallas_call(
        functools.partial(_lru_kernel, n_ctx=n_ctx, tc=tc, tr=n_ctx),
        grid=(b, LRU_W // tc),
        in_specs=[
            pl.BlockSpec((None, r, tc), lambda bi, j: (bi, 0, COL_LX // tc + j)),
            pl.BlockSpec((None, r, tc), lambda bi, j: (bi, 0, COL_LY // tc + j)),
            pl.BlockSpec((None, CONV_W, tc), chan),
            pl.BlockSpec((None, 1, tc), chan),
            pl.BlockSpec((None, 2, nb, LRU_BS, LRU_BS), lambda bi, j: (layer, 0, j, 0, 0)),
            pl.BlockSpec((None, 2, tc), chan),
            pl.BlockSpec((None, 2, nb, LRU_BS, LRU_BS), lambda bi, j: (layer, 0, j, 0, 0)),
            pl.BlockSpec((None, 2, tc), chan),
            pl.BlockSpec((None, 2, tc), chan),
        ],
        out_specs=pl.BlockSpec((None, r, tc), lambda bi, j: (bi, 0, j)),
        out_shape=jax.ShapeDtypeStruct((b, r, LRU_W), BF16),
        scratch_shapes=[pltpu.VMEM((r + 3 * LRU_PAD, tc), F32)] + [pltpu.VMEM((r, tc), F32)] * 4,
        compiler_params=_cparams(2),
        name="rglru",
    )(p, p, conv_w, conv_b, w_a, b_a, w_i, b_i, lam)


def _merge_kernel(xa_ref, xb_ref, xc_ref, wa_ref, wb_ref, wc_ref, ga_ref, gb_ref, gc_ref, bm_ref, o_ref):
    acc = None
    for n, (x_ref, w_ref, g_ref) in enumerate(((xa_ref, wa_ref, ga_ref), (xb_ref, wb_ref, gb_ref),
                                               (xc_ref, wc_ref, gc_ref))):
        proj = jnp.dot(x_ref[...], w_ref[...], preferred_element_type=F32)
        term = _sigmoid(g_ref[...].astype(F32) + bm_ref[n:n + 1, :]) * proj
        acc = term if acc is None else acc + term
    o_ref[...] = acc.astype(o_ref.dtype)


def _merge_call(o_gla, o_att, o_lru, wbr, p, b_merge, layer, tm, tn):
    b, r, d = o_gla.shape
    ncol = d // tn
    x_spec = pl.BlockSpec((None, tm, d), lambda j, bi, i: (bi, i, 0))

    def w_spec(n):
        return pl.BlockSpec((None, None, d, tn), lambda j, bi, i: (layer, n, 0, j))

    def g_spec(n):
        return pl.BlockSpec((None, tm, tn), lambda j, bi, i: (bi, i, COL_GT // tn + n * ncol + j))

    return pl.pallas_call(
        _merge_kernel,
        grid=(ncol, b, r // tm),
        in_specs=[x_spec, x_spec, x_spec, w_spec(0), w_spec(1), w_spec(2), g_spec(0), g_spec(1), g_spec(2),
                  pl.BlockSpec((None, N_BRANCH, tn), lambda j, bi, i: (layer, 0, j))],
        out_specs=pl.BlockSpec((None, tm, tn), lambda j, bi, i: (bi, i, j)),
        out_shape=jax.ShapeDtypeStruct((b, r, d), BF16),
        compiler_params=_cparams(3),
        name="merge",
    )(o_gla, o_att, o_lru, wbr, wbr, wbr, p, p, p, b_merge)


def _final_norm_kernel(h_ref, g_ref, o_ref):
    x = h_ref[...]
    var = jnp.mean(x * x, axis=-1, keepdims=True)
    o_ref[...] = x * lax.rsqrt(var + NORM_EPS) * g_ref[...]


def _final_norm_call(h, g, n_ctx):
    b, r, d = h.shape
    tr = n_ctx
    return pl.pallas_call(
        _final_norm_kernel,
        grid=(b, (r - n_ctx) // tr),
        in_specs=[
            pl.BlockSpec((None, tr, d), lambda bi, i: (bi, i + 1, 0)),
            pl.BlockSpec((1, d), lambda bi, i: (0, 0)),
        ],
        out_specs=pl.BlockSpec((None, tr, d), lambda bi, i: (bi, i, 0)),
        out_shape=jax.ShapeDtypeStruct((b, r - n_ctx, d), F32),
        compiler_params=_cparams(2),
        name="final_norm",
    )(h, g)


def _rope_tables(n_tokens):
    rows = n_tokens // GRID_W
    row = jnp.repeat(jnp.arange(rows, dtype=F32), GRID_W)
    col = jnp.tile(jnp.arange(GRID_W, dtype=F32), rows)
    inv = ROPE_THETA ** (-jnp.arange(ROPE_AXIS_PAIRS, dtype=F32) / ROPE_AXIS_PAIRS)
    ang = jnp.concatenate([row[:, None] * inv, col[:, None] * inv], axis=-1)
    cos, sin = jnp.cos(ang), jnp.sin(ang)
    cos_t = jnp.repeat(cos, 2, axis=-1)
    sin_t = jnp.stack([-sin, sin], axis=-1).reshape(n_tokens, ATT_HD)
    return cos_t, sin_t


def kernel(x, c, ctx, c_ctx, w_mod, b_mod, norm_mix_g, norm_ffn_g, w_in, gla_w_decay, gla_b_decay, gla_norm_g, q_norm_g, k_norm_g, conv_w, conv_b, lru_w_a, lru_b_a, lru_w_i, lru_b_i, lru_lambda, b_merge, w_branch, w_out, w_ffn_in, w_ffn_out, final_norm_g):
    depth = w_in.shape[0]
    batch, t_lat, d = x.shape
    n_ctx = ctx.shape[1]
    rows = n_ctx + t_lat
    ctx_row = batch
    assert batch < MOD_ROWS and rows % n_ctx == 0 and n_ctx % GLA_CHUNK == 0
    tm = 768 if rows % 768 == 0 else n_ctx
    tm_big = 1152 if rows % 1152 == 0 else tm
    tm_small = 576 if rows % 576 == 0 else n_ctx
    tn_in, tn_merge, tn_out, tn_ffn_in, tn_ffn_out = 1024, 512, d, 512, 1024

    w_in_t = jnp.swapaxes(w_in, 1, 2)
    w_branch_b = w_branch.astype(BF16)
    w_out_b = w_out.astype(BF16)
    w_ffn_out_b = w_ffn_out.astype(BF16)
    wd_pad = jnp.zeros((depth, 2, DEC_PAD, GLA_DK), F32)
    wd_pad = wd_pad.at[:, 0, :GLA_RANK].set(gla_w_decay[:, 0])
    wd_pad = wd_pad.at[:, 1, GLA_RANK:2 * GLA_RANK].set(gla_w_decay[:, 1])
    bd = gla_b_decay.reshape(depth, 2, 1, GLA_DK)
    cos_t, sin_t = _rope_tables(t_lat)

    c16 = jnp.zeros((MOD_ROWS, d), F32).at[:batch].set(c).at[ctx_row].set(c_ctx)
    mods4 = _mods_call(c16, w_mod, b_mod).reshape(depth, MOD_ROWS, 1, N_MOD * d)

    g_mix = norm_mix_g.reshape(depth, 1, d)
    g_ffn = norm_ffn_g.reshape(depth, 1, d)
    g_gla = gla_norm_g.reshape(depth, 1, GLA_HDV)
    g_q = q_norm_g.reshape(depth, 1, ATT_HD)
    g_k = k_norm_g.reshape(depth, 1, ATT_HD)
    cb3 = conv_b.reshape(depth, 1, LRU_W)

    h = jnp.concatenate([ctx, x], axis=1)
    for l in range(depth):
        u, dec = _normmod_call(h, g_mix, mods4, w_in_t, l, 0, 1, n_ctx, ctx_row, tm)
        p = _in_proj_call(u, w_in_t, l, rows, tn_in)
        o_att = _attn_call(p, g_q, g_k, cos_t, sin_t, l, n_ctx)
        o_gla = _gla_call(p, dec, wd_pad, bd, g_gla, l, n_ctx)
        o_lru = _lru_call(p, conv_w, cb3, lru_w_a, lru_b_a, lru_w_i, lru_b_i, lru_lambda, l, n_ctx)
        merged = _merge_call(o_gla, o_att, o_lru, w_branch_b, p, b_merge, l, tm, tn_merge)
        h, u2 = _linear_res_call(merged, w_out_b, l, h, mods4, 2, n_ctx, ctx_row, tm, tn_out, "out_proj",
                                 norm=(g_ffn, l, 3, 4))
        a = _swiglu_call(u2, w_ffn_in, l, tm_big, tn_ffn_in)
        h = _linear_res_call(a, w_ffn_out_b, l, h, mods4, 5, n_ctx, ctx_row, tm_small, tn_ffn_out, "ffn_out")
    return _final_norm_call(h, final_norm_g.reshape(1, d), n_ctx)
```
